```python
import math
import jax
import jax.numpy as jnp
from jax import lax
import numpy as np

D_MODEL = 1024
BATCH = 32
SEQ = 256
DEPTH = 2
DEC_BATCH = 2
DEC_SEQ = 2048
PAST_LEN = 512

GRID_W = 64
EPS = 1e-6
NA_HEADS = 8
NA_HEAD_DIM = 64
NA_WIN_H = 8
NA_WIN_W = 16
NA_KEY_COLS = 2 * NA_WIN_W
NA_Q_BLOCK = 128
DN_HEADS = 4
DN_HEAD_DIM = 128
DN_CONV = 3
DN_CHUNK = 64
GLA_HEADS = 4
GLA_KEY_DIM = 64
GLA_VAL_DIM = 128
GLA_GATE_RANK = 16
GLA_TAU = 16.0
GLA_CHUNK = 32
D_FF = 2816
FFN_CONV = 3

NA_W = NA_HEADS * NA_HEAD_DIM
DN_W = DN_HEADS * DN_HEAD_DIM
GLA_KW = GLA_HEADS * GLA_KEY_DIM
GLA_VW = GLA_HEADS * GLA_VAL_DIM
BRANCH_W = 512
N_BRANCH = 3
PROJ_SIZES = (NA_W, NA_W, NA_W,
              DN_W, DN_W, DN_W, DN_W,
              2 * DN_HEADS, 2 * DN_HEADS,
              GLA_KW, GLA_KW, GLA_VW, GLA_VW,
              2 * GLA_GATE_RANK,
              D_MODEL, D_MODEL, D_MODEL)
PROJ_DIM = sum(PROJ_SIZES)

kernel_name = 'hybrid_na_deltanet_gla_diffusion_step'


def rms_norm(x, g):
    xf = x.astype(jnp.float32)
    y = xf * lax.rsqrt(jnp.mean(xf * xf, axis=-1, keepdims=True) + EPS)
    return (y * g.astype(jnp.float32)).astype(x.dtype)


def l2_norm(x):
    return x * lax.rsqrt(jnp.sum(x * x, axis=-1, keepdims=True) + EPS)


def rev(x):
    return jnp.flip(x, axis=1)


def dw_conv(x, w):
    k = w.shape[0]
    return lax.conv_general_dilated(
        x, w[:, None, :].astype(x.dtype), window_strides=(1,),
        padding=[((k - 1) // 2, k // 2)],
        dimension_numbers=('NWC', 'WIO', 'NWC'),
        feature_group_count=x.shape[-1])


def adaln(cvec, w, b):
    return jax.nn.silu(cvec) @ w + b


def to_chunks(x, c):
    b, t, h = x.shape[:3]
    return jnp.moveaxis(x.reshape(b, t // c, c, h, *x.shape[3:]), 3, 1)


def context_attention(q, k, v):
    B, S, H, d = q.shape
    q_blocks = jnp.moveaxis(q.reshape(B, S // NA_Q_BLOCK, NA_Q_BLOCK, H, d), 1, 0)

    def one_block(q_b):
        s = jnp.einsum('bqhd,bkhd->bhqk', q_b, k).astype(jnp.float32)
        p = jax.nn.softmax(s, axis=-1).astype(v.dtype)
        return jnp.einsum('bhqk,bkhd->bqhd', p, v)

    o = lax.map(one_block, q_blocks)
    return jnp.moveaxis(o, 0, 1).reshape(B, S, H, d)


def neighbourhood_attention(q, k, v, k_ctx, v_ctx, rpb):
    B, T, H, d = q.shape
    rows = T // GRID_W
    kh = min(NA_WIN_H, rows)
    ncb = GRID_W // NA_WIN_W
    r = np.arange(rows)
    key_rows = np.clip(r - kh // 2, 0, rows - kh)[:, None] + np.arange(kh)
    q_cols = np.arange(GRID_W).reshape(ncb, NA_WIN_W)
    blk_start = np.clip(q_cols[:, 0] - NA_WIN_W // 2, 0, GRID_W - NA_KEY_COLS)
    key_cols = blk_start[:, None] + np.arange(NA_KEY_COLS)
    win_start = np.clip(q_cols - NA_WIN_W // 2, 0, GRID_W - NA_WIN_W)
    nk = kh * NA_KEY_COLS
    idx = (key_rows[:, None, :, None] * GRID_W + key_cols[None, :, None, :]).reshape(rows, ncb, nk).astype(np.int32)
    kc_b = key_cols[:, None, :]
    col_ok = (kc_b >= win_start[:, :, None]) & (kc_b < win_start[:, :, None] + NA_WIN_W)
    mask = np.broadcast_to(col_ok[:, :, None, :], (ncb, NA_WIN_W, kh, NA_KEY_COLS)).reshape(ncb, NA_WIN_W, nk)
    dr = key_rows - r[:, None] + NA_WIN_H - 1
    dc = np.clip(kc_b - q_cols[:, :, None], 1 - NA_WIN_W, NA_WIN_W - 1) + NA_WIN_W - 1
    bias = rpb.astype(jnp.float32)[:, dr[:, None, None, :, None], dc[None, :, :, None, :]]
    bias = jnp.moveaxis(bias.reshape(H, rows, ncb, NA_WIN_W, nk), 0, 1)
    q_rows = jnp.moveaxis(q.reshape(B, rows, ncb, NA_WIN_W, H, d), 1, 0)

    def one_row(args):
        q_r, idx_r, bias_r = args
        k_r = k[:, idx_r]
        v_r = v[:, idx_r]
        s_loc = jnp.einsum('bjqhd,bjkhd->bhjqk', q_r, k_r).astype(jnp.float32) + bias_r
        s_loc = jnp.where(mask, s_loc, -jnp.inf)
        s_ctx = jnp.einsum('bjqhd,blhd->bhjql', q_r, k_ctx).astype(jnp.float32)
        p = jax.nn.softmax(jnp.concatenate([s_loc, s_ctx], axis=-1), axis=-1).astype(v.dtype)
        return (jnp.einsum('bhjqk,bjkhd->bjqhd', p[..., :nk], v_r)
                + jnp.einsum('bhjql,blhd->bjqhd', p[..., nk:], v_ctx))

    o = lax.map(one_row, (q_rows, jnp.asarray(idx), bias))
    return jnp.moveaxis(o, 0, 1).reshape(B, T, H, d)


def gated_delta_chunked(q, k, v, g, beta, s0):
    B, T, H, dk = k.shape
    dv = v.shape[-1]
    C = DN_CHUNK
    qc, kc, vc = to_chunks(q, C), to_chunks(k, C), to_chunks(v, C)
    gc = jnp.cumsum(to_chunks(g, C), axis=-1)
    bc = to_chunks(beta, C)
    tri_incl = np.tril(np.ones((C, C), dtype=bool))
    tri_strict = np.tril(np.ones((C, C), dtype=bool), -1)
    decay = jnp.exp(jnp.where(tri_incl, gc[..., :, None] - gc[..., None, :], -jnp.inf))
    kk = jnp.einsum('bhnid,bhnjd->bhnij', kc, kc)
    low = jnp.where(tri_strict, bc[..., :, None] * kk * decay, 0.0)
    rhs = jnp.concatenate([vc * bc[..., None], kc * (bc * jnp.exp(gc))[..., None]], axis=-1)
    sol = lax.linalg.triangular_solve(low, rhs, left_side=True, lower=True, unit_diagonal=True)
    u, w = sol[..., :dv], sol[..., dv:]
    attn = jnp.einsum('bhnid,bhnjd->bhnij', qc, kc) * decay
    q_dec = qc * jnp.exp(gc)[..., None]
    g_last = gc[..., -1]
    k_dec = kc * jnp.exp(g_last[..., None] - gc)[..., None]

    def step(s, xs):
        u_n, w_n, a_n, qd_n, kd_n, gl_n = xs
        v_new = u_n - jnp.einsum('bhcd,bhde->bhce', w_n, s)
        o_n = jnp.einsum('bhcd,bhde->bhce', qd_n, s) + jnp.einsum('bhij,bhje->bhie', a_n, v_new)
        s = s * jnp.exp(gl_n)[..., None, None] + jnp.einsum('bhcd,bhce->bhde', kd_n, v_new)
        return s, o_n

    xs = tuple(jnp.moveaxis(a, 2, 0) for a in (u, w, attn, q_dec, k_dec, g_last))
    s_fin, o = lax.scan(step, s0, xs)
    o = jnp.moveaxis(o, 0, 2)
    return jnp.moveaxis(o, 1, 3).reshape(B, T, H, dv), s_fin


def gla_chunked(q, k, v, log_a, s0):
    B, T, H, dk = k.shape
    dv = v.shape[-1]
    C = GLA_CHUNK
    qc, kc, vc = to_chunks(q, C), to_chunks(k, C), to_chunks(v, C)
    b = jnp.cumsum(to_chunks(log_a, C), axis=3)
    tri = np.tril(np.ones((C, C), dtype=bool))[:, :, None]
    dec = jnp.exp(jnp.where(tri, b[..., :, None, :] - b[..., None, :, :], -jnp.inf))
    attn = jnp.einsum('bhnid,bhnjd,bhnijd->bhnij', qc, kc, dec)
    o = jnp.einsum('bhnij,bhnje->bhnie', attn, vc)
    b_last = b[..., -1, :]
    d_state = jnp.einsum('bhncd,bhnce->bhnde', kc * jnp.exp(b_last[..., None, :] - b), vc)

    def step(s, xs):
        ds_n, bl_n = xs
        return s * jnp.exp(bl_n)[..., None] + ds_n, s

    s_fin, s_start = lax.scan(step, s0, (jnp.moveaxis(d_state, 2, 0), jnp.moveaxis(b_last, 2, 0)))
    o = o + jnp.einsum('bhncd,bhnde->bhnce', qc * jnp.exp(b), jnp.moveaxis(s_start, 0, 2))
    return jnp.moveaxis(o, 1, 3).reshape(B, T, H, dv), s_fin


def trunk_layer(x, mod, lw, ctx):
    (w_in, q_g, k_g, rpb, dn_conv, a_log, dt_bias, dn_g, gla_w, gla_b, gla_g,
     w_br, w_o, n1, n2, w_up, f_conv, w_dn) = lw
    f32 = jnp.float32
    B, T, _ = x.shape
    sh1, sc1, gt1, sh2, sc2, gt2 = jnp.split(mod, 6, axis=-1)
    h = rms_norm(x, n1) * (1 + sc1) + sh1
    split_at = np.cumsum(PROJ_SIZES)[:-1].tolist()
    (na_q, na_k, na_v, dn_q, dn_k, dn_v, dn_gate, dn_beta, dn_a,
     gla_q, gla_k, gla_v, gla_gate, gla_lr, m_a, m_b, m_c) = jnp.split(h @ w_in, split_at, axis=-1)

    hs = (B, T, NA_HEADS, NA_HEAD_DIM)
    qa = rms_norm(na_q.reshape(hs), q_g) * (NA_HEAD_DIM ** -0.5)
    ka = rms_norm(na_k.reshape(hs), k_g)
    va = na_v.reshape(hs)
    if ctx is None:
        o_a = context_attention(qa, ka, va)
    else:
        o_a = neighbourhood_attention(qa, ka, va, ctx[0], ctx[1], rpb)
    o_a = o_a.reshape(B, T, NA_W)

    bs = (B, T, DN_HEADS, DN_HEAD_DIM)
    qkv = jax.nn.silu(dw_conv(jnp.concatenate([dn_q, dn_k, dn_v], axis=-1), dn_conv)).astype(f32)
    q_b, k_b, v_b = [part.reshape(bs) for part in jnp.split(qkv, 3, axis=-1)]
    q_b = l2_norm(q_b) * (DN_HEAD_DIM ** -0.5)
    k_b = l2_norm(k_b)
    beta = jax.nn.sigmoid(dn_beta.astype(f32)).reshape(B, T, 2, DN_HEADS)
    g = -jnp.exp(a_log.astype(f32)) * jax.nn.softplus(
        dn_a.astype(f32).reshape(B, T, 2, DN_HEADS) + dt_bias.astype(f32))
    if ctx is None:
        s0_dn = jnp.zeros((B, 2, DN_HEADS, DN_HEAD_DIM, DN_HEAD_DIM), f32)
    else:
        s0_dn = ctx[2].astype(f32)
    o_fw, s_fw = gated_delta_chunked(q_b, k_b, v_b, g[:, :, 0], beta[:, :, 0], s0_dn[:, 0])
    o_bw, s_bw = gated_delta_chunked(rev(q_b), rev(k_b), rev(v_b), rev(g[:, :, 1]), rev(beta[:, :, 1]), s0_dn[:, 1])
    o_b = rms_norm(o_fw + rev(o_bw), dn_g) * jax.nn.silu(dn_gate.astype(f32).reshape(bs))
    o_b = o_b.reshape(B, T, DN_W).astype(x.dtype)

    kshape = (B, T, GLA_HEADS, GLA_KEY_DIM)
    vshape = (B, T, GLA_HEADS, GLA_VAL_DIM)
    gl = jnp.einsum('btzr,zrk->btzk', gla_lr.reshape(B, T, 2, GLA_GATE_RANK), gla_w) + gla_b
    log_a = (jax.nn.log_sigmoid(gl.astype(f32)) / GLA_TAU).reshape(B, T, 2, GLA_HEADS, GLA_KEY_DIM)
    q_c = gla_q.astype(f32).reshape(kshape) * (GLA_KEY_DIM ** -0.5)
    k_c = gla_k.astype(f32).reshape(kshape)
    v_c = gla_v.astype(f32).reshape(vshape)
    if ctx is None:
        s0_gla = jnp.zeros((B, 2, GLA_HEADS, GLA_KEY_DIM, GLA_VAL_DIM), f32)
    else:
        s0_gla = ctx[3].astype(f32)
    oc_fw, sg_fw = gla_chunked(q_c, k_c, v_c, log_a[:, :, 0], s0_gla[:, 0])
    oc_bw, sg_bw = gla_chunked(rev(q_c), rev(k_c), rev(v_c), rev(log_a[:, :, 1]), s0_gla[:, 1])
    o_c = rms_norm(oc_fw + rev(oc_bw), gla_g) * jax.nn.silu(gla_gate.astype(f32).reshape(vshape))
    o_c = o_c.reshape(B, T, GLA_VW).astype(x.dtype)

    merged = (jax.nn.sigmoid(m_a) * (o_a @ w_br[0])
              + jax.nn.sigmoid(m_b) * (o_b @ w_br[1])
              + jax.nn.sigmoid(m_c) * (o_c @ w_br[2]))
    x = x + gt1 * (merged @ w_o)

    h2 = rms_norm(x, n2) * (1 + sc2) + sh2
    u_val, u_gate = jnp.split(dw_conv(h2 @ w_up, f_conv), 2, axis=-1)
    x = x + gt2 * ((jax.nn.silu(u_gate) * u_val) @ w_dn)

    if ctx is None:
        state = (ka, va,
                 jnp.stack([s_fw, s_bw], axis=1).astype(x.dtype),
                 jnp.stack([sg_fw, sg_bw], axis=1).astype(x.dtype))
    else:
        state = None
    return x, state


def setup_inputs(seed: int = 0) -> dict:
    key = jax.random.key(seed)
    ks = jax.random.split(key, 32)
    f32 = jnp.float32
    d = D_MODEL

    def nrm(k, shape, scale):
        return jax.random.normal(k, shape, f32) * scale

    dt = jnp.exp(jax.random.uniform(ks[15], (DEPTH, 2, DN_HEADS), f32, math.log(1e-3), math.log(0.1)))
    return {
        'x_prompt': nrm(ks[0], (BATCH, SEQ, d), 1.0),
        'x_sample': nrm(ks[1], (DEC_BATCH, DEC_SEQ, d), 1.0),
        'cache_k': nrm(ks[2], (DEC_BATCH, DEPTH, PAST_LEN, NA_HEADS, NA_HEAD_DIM), 1.0),
        'cache_v': nrm(ks[3], (DEC_BATCH, DEPTH, PAST_LEN, NA_HEADS, NA_HEAD_DIM), 1.0),
        'state_dn': nrm(ks[4], (DEC_BATCH, DEPTH, 2, DN_HEADS, DN_HEAD_DIM, DN_HEAD_DIM), 0.1),
        'state_gla': nrm(ks[5], (DEC_BATCH, DEPTH, 2, GLA_HEADS, GLA_KEY_DIM, GLA_VAL_DIM), 0.5),
        'c': nrm(ks[6], (DEC_BATCH, d), 1.0),
        'c_ctx': nrm(ks[7], (d,), 1.0),
        'w_ada': nrm(ks[8], (DEPTH, d, 6 * d), 0.5 * d ** -0.5),
        'b_ada': nrm(ks[9], (DEPTH, 6 * d), 0.02),
        'norm1': 1.0 + nrm(ks[10], (DEPTH, d), 0.05),
        'w_in': nrm(ks[11], (DEPTH, d, PROJ_DIM), d ** -0.5),
        'na_q_norm': 1.0 + nrm(ks[12], (DEPTH, NA_HEAD_DIM), 0.05),
        'na_k_norm': 1.0 + nrm(ks[13], (DEPTH, NA_HEAD_DIM), 0.05),
        'na_rpb': nrm(ks[14], (DEPTH, NA_HEADS, 2 * NA_WIN_H - 1, 2 * NA_WIN_W - 1), 0.1),
        'dn_conv': nrm(ks[16], (DEPTH, DN_CONV, 3 * DN_W), DN_CONV ** -0.5),
        'dn_a_log': jnp.log(jax.random.uniform(ks[17], (DEPTH, 2, DN_HEADS), f32, 1.0, 16.0)),
        'dn_dt_bias': dt + jnp.log(-jnp.expm1(-dt)),
        'dn_out_norm': 1.0 + nrm(ks[18], (DEPTH, DN_HEAD_DIM), 0.05),
        'gla_w_gate': nrm(ks[19], (DEPTH, 2, GLA_GATE_RANK, GLA_KW), GLA_GATE_RANK ** -0.5),
        'gla_b_gate': nrm(ks[20], (DEPTH, 2, GLA_KW), 0.1),
        'gla_out_norm': 1.0 + nrm(ks[21], (DEPTH, GLA_VAL_DIM), 0.05),
        'w_branch': nrm(ks[22], (DEPTH, N_BRANCH, BRANCH_W, d), BRANCH_W ** -0.5),
        'w_out': nrm(ks[23], (DEPTH, d, d), d ** -0.5),
        'norm2': 1.0 + nrm(ks[24], (DEPTH, d), 0.05),
        'w_up': nrm(ks[25], (DEPTH, d, 2 * D_FF), d ** -0.5),
        'ffn_conv': nrm(ks[26], (DEPTH, FFN_CONV, 2 * D_FF), FFN_CONV ** -0.5),
        'w_down': nrm(ks[27], (DEPTH, D_FF, d), D_FF ** -0.5),
    }


def reference(x_prompt, x_sample, cache_k, cache_v, state_dn, state_gla, c, c_ctx,
              w_ada, b_ada, norm1, w_in, na_q_norm, na_k_norm, na_rpb,
              dn_conv, dn_a_log, dn_dt_bias, dn_out_norm,
              gla_w_gate, gla_b_gate, gla_out_norm,
              w_branch, w_out, norm2, w_up, ffn_conv, w_down):
    y_p, y_s = x_prompt, x_sample
    new_k, new_v, new_dn, new_gla = [], [], [], []
    for l in range(DEPTH):
        lw = (w_in[l], na_q_norm[l], na_k_norm[l], na_rpb[l], dn_conv[l], dn_a_log[l], dn_dt_bias[l],
              dn_out_norm[l], gla_w_gate[l], gla_b_gate[l], gla_out_norm[l], w_branch[l], w_out[l],
              norm1[l], norm2[l], w_up[l], ffn_conv[l], w_down[l])
        mod_ctx = adaln(c_ctx[None, None, :], w_ada[l], b_ada[l]).astype(y_p.dtype)
        mod_lat = adaln(c[:, None, :], w_ada[l], b_ada[l]).astype(y_s.dtype)
        y_p, (k_l, v_l, dn_l, gla_l) = trunk_layer(y_p, mod_ctx, lw, None)
        y_s, _ = trunk_layer(y_s, mod_lat, lw,
                             (cache_k[:, l], cache_v[:, l], state_dn[:, l], state_gla[:, l]))
        new_k.append(k_l)
        new_v.append(v_l)
        new_dn.append(dn_l)
        new_gla.append(gla_l)
    return (y_p, y_s, jnp.stack(new_k, axis=1), jnp.stack(new_v, axis=1),
            jnp.stack(new_dn, axis=1), jnp.stack(new_gla, axis=1))
```

```python
import functools

import numpy as np
import jax
import jax.numpy as jnp
from jax import lax
from jax.experimental import pallas as pl
from jax.experimental.pallas import tpu as pltpu

F32 = jnp.float32
BF16 = jnp.bfloat16

D_MODEL = 1024
DEPTH = 2
GRID_W = 64
EPS = 1e-6
NA_HEADS = 8
NA_HEAD_DIM = 64
NA_WIN_H = 8
NA_WIN_W = 16
DN_HEADS = 4
DN_HEAD_DIM = 128
DN_CHUNK = 64
GLA_HEADS = 4
GLA_KEY_DIM = 64
GLA_VAL_DIM = 128
GLA_GATE_RANK = 16
GLA_TAU = 16.0
GLA_CHUNK = 32
D_FF = 2816

OFF_NA_Q, OFF_NA_K, OFF_NA_V = 0, 512, 1024
OFF_DN_Q, OFF_DN_K, OFF_DN_V, OFF_DN_GATE = 1536, 2048, 2560, 3072
OFF_GLA_QK, OFF_GLA_V, OFF_GLA_GATE = 3584, 4096, 4608
OFF_M = 5120
OFF_SMALL = 8192
PROJ_PAD = 8320
SMALL_BETA, SMALL_DECAY, SMALL_LR = 0, 8, 16

GROUP = 256
NEG = -1e30
VMEM_LIMIT = 56 * 1024 * 1024

NA_QROWS = 4
NA_KROWS = 12


def _bdot(a, b):
    return jnp.dot(a.astype(BF16), b.astype(BF16), preferred_element_type=F32)


def _bdot_nt(a, b):
    return lax.dot_general(a.astype(BF16), b.astype(BF16), (((1,), (1,)), ((), ())), preferred_element_type=F32)


def _bdot_tn(a, b):
    return lax.dot_general(a.astype(BF16), b.astype(BF16), (((0,), (0,)), ((), ())), preferred_element_type=F32)


def _split3(x):
    hi = x.astype(BF16)
    r = x - hi.astype(F32)
    mid = r.astype(BF16)
    lo = (r - mid.astype(F32)).astype(BF16)
    return hi, mid, lo


def _dot_mask_f32(t, x):
    hi, mid, lo = _split3(x)
    d = lambda p: jnp.dot(t, p, preferred_element_type=F32)
    return d(hi) + d(mid) + d(lo)


def _dot3(a, b):
    ah = a.astype(BF16)
    al = (a - ah.astype(F32)).astype(BF16)
    bh = b.astype(BF16)
    bl = (b - bh.astype(F32)).astype(BF16)
    d = lambda p, q: jnp.dot(p, q, preferred_element_type=F32)
    return d(ah, bh) + d(ah, bl) + d(al, bh)


_dn_sq = _bdot
_dn_ap = _bdot
_dn_scan = _bdot


def _sigmoid(x):
    return 1.0 / (1.0 + jnp.exp(-x))


def _silu(x):
    return x * _sigmoid(x)


def _softplus(x):
    return jnp.maximum(x, 0.0) + jnp.log(1.0 + jnp.exp(-jnp.abs(x)))


def _params(*sem):
    return pltpu.CompilerParams(dimension_semantics=sem, vmem_limit_bytes=VMEM_LIMIT)


def _adaln_kernel(c_ref, w_ref, b_ref, o_ref):
    o_ref[...] = _bdot(_silu(c_ref[...]), w_ref[...]) + b_ref[...]


def _adaln(cvec8, w_ada, b_ada):
    return pl.pallas_call(
        _adaln_kernel,
        grid=(DEPTH, 6),
        in_specs=[pl.BlockSpec((8, D_MODEL), lambda l, j: (0, 0)),
                  pl.BlockSpec((None, D_MODEL, D_MODEL), lambda l, j: (l, 0, j)),
                  pl.BlockSpec((None, None, 1, D_MODEL), lambda l, j: (l, j, 0, 0))],
        out_specs=pl.BlockSpec((None, None, 8, D_MODEL), lambda l, j: (l, j, 0, 0)),
        out_shape=jax.ShapeDtypeStruct((DEPTH, 6, 8, D_MODEL), F32),
        compiler_params=_params("arbitrary", "arbitrary"),
        name="adaln",
    )(cvec8, w_ada, b_ada.reshape(DEPTH, 6, 1, D_MODEL))


def _mod_row(tile, tiles_per_seq, latent):
    return (1 + tile // tiles_per_seq) if latent else 0


def _norm_mod(x, gain, shift, scale):
    y = x * lax.rsqrt(jnp.mean(x * x, axis=-1, keepdims=True) + EPS) * gain
    return y * (1.0 + scale) + shift


def _inproj_kernel(x_ref, mod_ref, n1_ref, w_ref, o_ref, h_scr, *, tiles_per_seq, latent):
    @pl.when(pl.program_id(1) == 0)
    def _():
        r = _mod_row(pl.program_id(0), tiles_per_seq, latent)
        h = _norm_mod(x_ref[...], n1_ref[...], mod_ref[0, pl.ds(r, 1), :], mod_ref[1, pl.ds(r, 1), :])
        h_scr[...] = h.astype(BF16)

    o_ref[...] = jnp.dot(h_scr[...], w_ref[...], preferred_element_type=F32)


def _inproj(x, mod_l, n1, w_in_l, seq_len, latent, tm=1024, tn=640):
    rows = x.shape[0]
    kern = functools.partial(_inproj_kernel, tiles_per_seq=max(seq_len // tm, 1), latent=latent)
    return pl.pallas_call(
        kern,
        grid=(rows // tm, PROJ_PAD // tn),
        in_specs=[pl.BlockSpec((tm, D_MODEL), lambda m, n: (m, 0)),
                  pl.BlockSpec((6, 8, D_MODEL), lambda m, n: (0, 0, 0)),
                  pl.BlockSpec((1, D_MODEL), lambda m, n: (0, 0)),
                  pl.BlockSpec((D_MODEL, tn), lambda m, n: (0, n))],
        out_specs=pl.BlockSpec((tm, tn), lambda m, n: (m, n)),
        out_shape=jax.ShapeDtypeStruct((rows, PROJ_PAD), F32),
        scratch_shapes=[pltpu.VMEM((tm, D_MODEL), BF16)],
        compiler_params=_params("arbitrary", "arbitrary"),
        name="inproj",
    )(x, mod_l, n1, w_in_l)


def _head_rms(x, gain):
    return x * lax.rsqrt(jnp.mean(x * x, axis=-1, keepdims=True) + EPS) * gain


def _ctx_attn_kernel(q_ref, k_ref, v_ref, qg_ref, kg_ref, o_ref, ka_ref, va_ref):
    for h in range(NA_HEADS):
        sl = slice(h * NA_HEAD_DIM, (h + 1) * NA_HEAD_DIM)
        qn = _head_rms(q_ref[:, sl], qg_ref[...]) * (NA_HEAD_DIM ** -0.5)
        kn = _head_rms(k_ref[:, sl], kg_ref[...])
        v = v_ref[:, sl]
        ka_ref[:, sl] = kn
        s = _bdot_nt(qn, kn)
        p = jnp.exp(s - jnp.max(s, axis=-1, keepdims=True))
        o_ref[:, sl] = _bdot(p, v) / jnp.sum(p, axis=-1, keepdims=True)
    va_ref[...] = v_ref[...]


def _ctx_attn(p_ctx, qg, kg, batch, seq):
    rows = batch * seq
    blk = lambda j: pl.BlockSpec((seq, 512), lambda b: (b, j))
    out = jax.ShapeDtypeStruct((rows, 512), F32)
    return pl.pallas_call(
        _ctx_attn_kernel,
        grid=(batch,),
        in_specs=[blk(OFF_NA_Q // 512), blk(OFF_NA_K // 512), blk(OFF_NA_V // 512),
                  pl.BlockSpec((1, NA_HEAD_DIM), lambda b: (0, 0)),
                  pl.BlockSpec((1, NA_HEAD_DIM), lambda b: (0, 0))],
        out_specs=[blk(0), blk(0), blk(0)],
        out_shape=[out, out, out],
        compiler_params=_params("arbitrary"),
        name="ctx_attn",
    )(p_ctx, p_ctx, p_ctx, qg, kg)


def _na_tables():
    rows = 32
    nblk = rows // NA_QROWS
    dr = np.zeros((nblk, NA_QROWS, NA_KROWS), np.int32)
    ok = np.zeros((nblk, NA_QROWS, NA_KROWS), bool)
    for m in range(nblk):
        kstart = int(np.clip(NA_QROWS * m - NA_WIN_H // 2, 0, rows - NA_KROWS))
        for i in range(NA_QROWS):
            r = NA_QROWS * m + i
            rs = int(np.clip(r - NA_WIN_H // 2, 0, rows - NA_WIN_H))
            for j in range(NA_KROWS):
                kr = kstart + j
                ok[m, i, j] = rs <= kr < rs + NA_WIN_H
                dr[m, i, j] = np.clip(kr - r + NA_WIN_H - 1, 0, 2 * NA_WIN_H - 2)
    for m in range(2, nblk - 1):
        assert (dr[m][ok[m]] == dr[1][ok[1]]).all() and (ok[m] == ok[1]).all()
    cls = [0, 1, nblk - 1]
    qc = np.arange(GRID_W)
    cs = np.clip(qc - NA_WIN_W // 2, 0, GRID_W - NA_WIN_W)
    col_ok = (qc[None, :] >= cs[:, None]) & (qc[None, :] < cs[:, None] + NA_WIN_W)
    return dr[cls], ok[cls], col_ok


def _na_bias_table(rpb):
    dr, ok, col_ok = _na_tables()
    nd = 2 * NA_WIN_W - 1
    lo = GRID_W - NA_WIN_W
    w = jnp.full(rpb.shape[:3] + (2 * GRID_W,), NEG, F32)
    w = w.at[..., lo:lo + nd].set(rpb.astype(F32))
    t = jnp.tile(w, (1, 1, 1, GRID_W))[..., :GRID_W * (2 * GRID_W - 1)]
    t = t.reshape(rpb.shape[:3] + (GRID_W, 2 * GRID_W - 1))[..., GRID_W - 1:]
    t = jnp.where(col_ok, t, NEG)
    big = t[:, :, dr]
    big = jnp.where(ok[..., None, None], big, NEG)
    big = jnp.transpose(big, (0, 2, 1, 3, 5, 4, 6))
    return big.reshape(rpb.shape[0], 3, NA_HEADS, NA_QROWS * GRID_W, NA_KROWS * GRID_W)


def _lat_attn_kernel(q_ref, k_ref, v_ref, ck_ref, cv_ref, bias_ref, qg_ref, kg_ref, o_ref, kn_scr, *, rows):
    m = pl.program_id(1)

    @pl.when(m == 0)
    def _():
        for h in range(NA_HEADS):
            sl = slice(h * NA_HEAD_DIM, (h + 1) * NA_HEAD_DIM)
            kn_scr[:, sl] = _head_rms(k_ref[:, sl], kg_ref[...]).astype(BF16)

    kstart = jnp.clip(NA_QROWS * m - NA_WIN_H // 2, 0, rows - NA_KROWS) * GRID_W
    kstart = pl.multiple_of(kstart, 256)
    nk = NA_KROWS * GRID_W
    for h in range(NA_HEADS):
        sl = slice(h * NA_HEAD_DIM, (h + 1) * NA_HEAD_DIM)
        qn = (_head_rms(q_ref[:, sl], qg_ref[...]) * (NA_HEAD_DIM ** -0.5)).astype(BF16)
        kl = kn_scr[pl.ds(kstart, nk), sl]
        vl = v_ref[pl.ds(kstart, nk), sl]
        s_loc = _bdot_nt(qn, kl) + bias_ref[h]
        s_ctx = _bdot_nt(qn, ck_ref[:, sl])
        mx = jnp.maximum(jnp.max(s_loc, axis=-1, keepdims=True), jnp.max(s_ctx, axis=-1, keepdims=True))
        p_loc = jnp.exp(s_loc - mx)
        p_ctx = jnp.exp(s_ctx - mx)
        den = jnp.sum(p_loc, axis=-1, keepdims=True) + jnp.sum(p_ctx, axis=-1, keepdims=True)
        o_ref[:, sl] = (_bdot(p_loc, vl) + _bdot(p_ctx, cv_ref[:, sl])) / den


def _lat_attn(p_lat, cache_k, cache_v, bias, qg, kg, layer, batch, seq):
    rows = seq // GRID_W
    nblk = rows // NA_QROWS
    qtok = NA_QROWS * GRID_W
    past = cache_k.shape[2]
    cls = lambda m: jnp.where(m == 0, 0, jnp.where(m == nblk - 1, 2, 1))
    cache_spec = pl.BlockSpec((None, None, past, 512), lambda b, m: (b, layer, 0, 0))
    return pl.pallas_call(
        functools.partial(_lat_attn_kernel, rows=rows),
        grid=(batch, nblk),
        in_specs=[pl.BlockSpec((qtok, 512), lambda b, m: (b * nblk + m, OFF_NA_Q // 512)),
                  pl.BlockSpec((seq, 512), lambda b, m: (b, OFF_NA_K // 512)),
                  pl.BlockSpec((seq, 512), lambda b, m: (b, OFF_NA_V // 512)),
                  cache_spec, cache_spec,
                  pl.BlockSpec((None, None, NA_HEADS, qtok, NA_KROWS * GRID_W),
                               lambda b, m: (layer, cls(m), 0, 0, 0)),
                  pl.BlockSpec((1, NA_HEAD_DIM), lambda b, m: (0, 0)),
                  pl.BlockSpec((1, NA_HEAD_DIM), lambda b, m: (0, 0))],
        out_specs=pl.BlockSpec((qtok, 512), lambda b, m: (b * nblk + m, 0)),
        out_shape=jax.ShapeDtypeStruct((batch * seq, 512), F32),
        scratch_shapes=[pltpu.VMEM((seq, 512), BF16)],
        compiler_params=_params("arbitrary", "arbitrary"),
        name="lat_attn",
    )(p_lat, p_lat, p_lat, cache_k, cache_v, bias, qg, kg)


def _group_masks(chunk):
    r = lax.broadcasted_iota(jnp.int32, (GROUP, GROUP), 0)
    c = lax.broadcasted_iota(jnp.int32, (GROUP, GROUP), 1)
    same = (r // chunk) == (c // chunk)
    return (same & (c <= r), same & (c >= r), same & (c < r), same & (c > r), same)


def _halving_masks(chunk, lower):
    r = lax.broadcasted_iota(jnp.int32, (GROUP, GROUP), 0)
    c = lax.broadcasted_iota(jnp.int32, (GROUP, GROUP), 1)
    if not lower:
        r, c = c, r
    masks = []
    s = 1
    while s < chunk:
        masks.append(((r // (2 * s)) == (c // (2 * s))) & ((r // s) % 2 == 1) & ((c // s) % 2 == 0))
        s *= 2
    return masks


def _unit_tri_inverse(low, masks):
    r = lax.broadcasted_iota(jnp.int32, (GROUP, GROUP), 0)
    c = lax.broadcasted_iota(jnp.int32, (GROUP, GROUP), 1)
    inv = jnp.where(r == c, 1.0, 0.0) - jnp.where(masks[0], low, 0.0)
    for m in masks[1:]:
        inv = inv - _dn_sq(_dn_sq(inv, jnp.where(m, low, 0.0)), inv)
    return inv


def _sel_col(a, lane):
    idx = lax.broadcasted_iota(jnp.int32, a.shape, 1)
    return jnp.sum(jnp.where(idx == lane, a, 0.0), axis=1, keepdims=True)


def _sel_row(a, row):
    idx = lax.broadcasted_iota(jnp.int32, a.shape, 0)
    return jnp.sum(jnp.where(idx == row, a, 0.0), axis=0, keepdims=True)


def _conv3_silu(x, w):
    t = x.shape[0]
    row = lax.broadcasted_iota(jnp.int32, x.shape, 0)
    xm = jnp.where(row == 0, 0.0, pltpu.roll(x, 1, 0))
    xp = jnp.where(row == t - 1, 0.0, pltpu.roll(x, t - 1, 0))
    return _silu(xm * w[0:1, :] + x * w[1:2, :] + xp * w[2:3, :])


def _deltanet_kernel(*refs, has_s0, emit_state):
    (q_ref, k_ref, v_ref, gate_ref, sm_ref, cq_ref, ck_ref, cv_ref, alog_ref, dtb_ref, og_ref), refs = refs[:11], refs[11:]
    if has_s0:
        s0_ref, refs = refs[0], refs[1:]
    o_ref, refs = refs[0], refs[1:]
    if emit_state:
        sfin_ref, refs = refs[0], refs[1:]
    qn_scr, kn_scr, vc_scr, of_scr, ob_scr, s_scr = refs

    h = pl.program_id(1)
    seq = q_ref.shape[0]
    ngroups = seq // GROUP
    nchunk = GROUP // DN_CHUNK

    q = _conv3_silu(q_ref[...], cq_ref[...])
    qn_scr[...] = q * lax.rsqrt(jnp.sum(q * q, axis=-1, keepdims=True) + EPS) * (DN_HEAD_DIM ** -0.5)
    k = _conv3_silu(k_ref[...], ck_ref[...])
    kn_scr[...] = k * lax.rsqrt(jnp.sum(k * k, axis=-1, keepdims=True) + EPS)
    vc_scr[...] = _conv3_silu(v_ref[...], cv_ref[...])

    if has_s0:
        s_scr[...] = s0_ref[...]
    else:
        s_scr[...] = jnp.zeros_like(s_scr)

    m_fi, m_bi, m_fs, m_bs, m_same = _group_masks(DN_CHUNK)
    incl = (m_fi, m_bi)
    strict = (m_fs, m_bs)
    tri = (m_fi.astype(BF16), m_bi.astype(BF16))
    ones_bd = m_same.astype(BF16)
    halving = (_halving_masks(DN_CHUNK, True), _halving_masks(DN_CHUNK, False))

    def group_step(t, carry):
        for d in range(2):
            gi = t if d == 0 else ngroups - 1 - t
            r0 = pl.multiple_of(gi * GROUP, GROUP)
            sm = sm_ref[pl.ds(r0, GROUP), :]
            g128 = -jnp.exp(alog_ref[...]) * _softplus(sm + dtb_ref[...])
            gcum = _dot_mask_f32(tri[d], g128)
            gtot = _dot_mask_f32(ones_bd, g128)
            lane_g = SMALL_DECAY + DN_HEADS * d + h
            gc_col = _sel_col(gcum, lane_g)
            gt_col = _sel_col(gtot, lane_g)
            gc_row = _sel_row(gcum.T, lane_g)
            beta = _sel_col(_sigmoid(sm), SMALL_BETA + DN_HEADS * d + h)

            dec = jnp.exp(jnp.where(incl[d], gc_col - gc_row, NEG))
            qg = qn_scr[pl.ds(r0, GROUP), :]
            kg = kn_scr[pl.ds(r0, GROUP), :]
            vg = vc_scr[pl.ds(r0, GROUP), :]
            kk = _bdot_nt(kg, kg)
            low = jnp.where(strict[d], beta * kk * dec, 0.0)
            attn = _bdot_nt(qg, kg) * dec
            x = jnp.concatenate([vg * beta, kg * (beta * jnp.exp(gc_col))], axis=1)
            x = _dn_ap(_unit_tri_inverse(low, halving[d]), x)
            u = x[:, :DN_HEAD_DIM]
            w = x[:, DN_HEAD_DIM:]
            qd = qg * jnp.exp(gc_col)
            kd = kg * jnp.exp(gt_col - gc_col)
            o_scr = of_scr if d == 0 else ob_scr
            for cc in range(nchunk):
                c = cc if d == 0 else nchunk - 1 - cc
                rs = slice(c * DN_CHUNK, (c + 1) * DN_CHUNK)
                s = s_scr[d]
                v_new = u[rs] - _dn_scan(w[rs], s)
                o_c = _dn_scan(qd[rs], s) + _dn_scan(attn[rs, rs], v_new)
                s_scr[d] = s * jnp.exp(gt_col[c * DN_CHUNK:c * DN_CHUNK + 1, :]) + _bdot_tn(kd[rs], v_new)
                o_scr[pl.ds(r0 + c * DN_CHUNK, DN_CHUNK), :] = o_c
        return carry

    if ngroups == 1:
        group_step(0, 0)
    else:
        lax.fori_loop(0, ngroups, group_step, 0)

    o = of_scr[...] + ob_scr[...]
    o = o * lax.rsqrt(jnp.mean(o * o, axis=-1, keepdims=True) + EPS) * og_ref[...]
    o_ref[...] = o * _silu(gate_ref[...])
    if emit_state:
        sfin_ref[...] = s_scr[...]


def _deltanet(p, conv_w, alog128, dtb128, out_g, s0, layer, batch, seq, emit_state):
    rows = batch * seq
    hd = DN_HEAD_DIM
    col = lambda off: pl.BlockSpec((seq, hd), lambda b, h: (b, off // hd + h))
    cw = lambda part: pl.BlockSpec((3, hd), lambda b, h: (0, part * DN_HEADS + h))
    row128 = pl.BlockSpec((1, 128), lambda b, h: (0, 0))
    in_specs = [col(OFF_DN_Q), col(OFF_DN_K), col(OFF_DN_V), col(OFF_DN_GATE),
                pl.BlockSpec((seq, 128), lambda b, h: (b, OFF_SMALL // 128)),
                cw(0), cw(1), cw(2), row128, row128, row128]
    args = [p, p, p, p, p, conv_w, conv_w, conv_w, alog128, dtb128, out_g]
    if s0 is not None:
        in_specs.append(pl.BlockSpec((None, None, 2, None, hd, hd), lambda b, h: (b, layer, 0, h, 0, 0)))
        args.append(s0)
    out_specs = [pl.BlockSpec((seq, hd), lambda b, h: (b, h))]
    out_shape = [jax.ShapeDtypeStruct((rows, DN_HEADS * hd), F32)]
    if emit_state:
        out_specs.append(pl.BlockSpec((None, 2, None, hd, hd), lambda b, h: (b, 0, h, 0, 0)))
        out_shape.append(jax.ShapeDtypeStruct((batch, 2, DN_HEADS, hd, hd), F32))
    res = pl.pallas_call(
        functools.partial(_deltanet_kernel, has_s0=s0 is not None, emit_state=emit_state),
        grid=(batch, DN_HEADS),
        in_specs=in_specs,
        out_specs=out_specs,
        out_shape=out_shape,
        scratch_shapes=[pltpu.VMEM((seq, hd), F32)] * 5 + [pltpu.VMEM((2, hd, hd), F32)],
        compiler_params=_params("arbitrary", "arbitrary"),
        name="deltanet",
    )(*args)
    return res if emit_state else (res[0], None)


def _gla_kernel(*refs, has_s0, emit_state):
    (qk_ref, v_ref, gate_ref, sm_ref, wg_ref, bg_ref, og_ref), refs = refs[:7], refs[7:]
    if has_s0:
        s0_ref, refs = refs[0], refs[1:]
    o_ref, refs = refs[0], refs[1:]
    if emit_state:
        sfin_ref, refs = refs[0], refs[1:]
    of_scr, ob_scr, s_scr = refs

    seq = qk_ref.shape[0]
    ngroups = seq // GROUP
    nchunk = GROUP // GLA_CHUNK
    dk = GLA_KEY_DIM

    if has_s0:
        s_scr[...] = s0_ref[...]
    else:
        s_scr[...] = jnp.zeros_like(s_scr)

    m_fi, m_bi, _, _, m_same = _group_masks(GLA_CHUNK)
    incl = (m_fi, m_bi)
    tri = (m_fi.astype(BF16), m_bi.astype(BF16))
    ones_bd = m_same.astype(BF16)

    def group_step(t, carry):
        for d in range(2):
            gi = t if d == 0 else ngroups - 1 - t
            r0 = pl.multiple_of(gi * GROUP, GROUP)
            gl = _bdot(sm_ref[pl.ds(r0, GROUP), :], wg_ref[...]) + bg_ref[...]
            log_a = -_softplus(-gl) / GLA_TAU
            lanes = slice(d * dk, (d + 1) * dk)
            b = _dot_mask_f32(tri[d], log_a)[:, lanes]
            btot = _dot_mask_f32(ones_bd, log_a)[:, lanes]
            qk = qk_ref[pl.ds(r0, GROUP), :]
            qg = qk[:, :dk] * (dk ** -0.5)
            kg = qk[:, dk:]
            vg = v_ref[pl.ds(r0, GROUP), :]
            qe = qg * jnp.exp(b)
            attn = jnp.where(incl[d], _bdot_nt(qe, kg * jnp.exp(-b)), 0.0)
            o_in = _bdot(attn, vg)
            kd = kg * jnp.exp(btot - b)
            o_scr = of_scr if d == 0 else ob_scr
            for cc in range(nchunk):
                c = cc if d == 0 else nchunk - 1 - cc
                rs = slice(c * GLA_CHUNK, (c + 1) * GLA_CHUNK)
                st = s_scr[d]
                o_scr[pl.ds(r0 + c * GLA_CHUNK, GLA_CHUNK), :] = o_in[rs] + _bdot_nt(qe[rs], st)
                s_scr[d] = st * jnp.exp(btot[c * GLA_CHUNK:c * GLA_CHUNK + 1, :]) + _bdot_tn(vg[rs], kd[rs])
        return carry

    if ngroups == 1:
        group_step(0, 0)
    else:
        lax.fori_loop(0, ngroups, group_step, 0)

    o = of_scr[...] + ob_scr[...]
    o = o * lax.rsqrt(jnp.mean(o * o, axis=-1, keepdims=True) + EPS) * og_ref[...]
    o_ref[...] = o * _silu(gate_ref[...])
    if emit_state:
        sfin_ref[...] = s_scr[...]


def _gla(p, wg, bg, out_g, s0t, layer, batch, seq, emit_state):
    rows = batch * seq
    dv, dk = GLA_VAL_DIM, GLA_KEY_DIM
    col = lambda off: pl.BlockSpec((seq, 128), lambda b, h: (b, off // 128 + h))
    in_specs = [col(OFF_GLA_QK), col(OFF_GLA_V), col(OFF_GLA_GATE),
                pl.BlockSpec((seq, 128), lambda b, h: (b, OFF_SMALL // 128)),
                pl.BlockSpec((None, 128, 128), lambda b, h: (h, 0, 0)),
                pl.BlockSpec((None, 1, 128), lambda b, h: (h, 0, 0)),
                pl.BlockSpec((1, 128), lambda b, h: (0, 0))]
    args = [p, p, p, p, wg, bg, out_g]
    if s0t is not None:
        in_specs.append(pl.BlockSpec((None, None, 2, None, dv, dk), lambda b, h: (b, layer, 0, h, 0, 0)))
        args.append(s0t)
    out_specs = [pl.BlockSpec((seq, dv), lambda b, h: (b, h))]
    out_shape = [jax.ShapeDtypeStruct((rows, GLA_HEADS * dv), F32)]
    if emit_state:
        out_specs.append(pl.BlockSpec((None, 2, None, dv, dk), lambda b, h: (b, 0, h, 0, 0)))
        out_shape.append(jax.ShapeDtypeStruct((batch, 2, GLA_HEADS, dv, dk), F32))
    res = pl.pallas_call(
        functools.partial(_gla_kernel, has_s0=s0t is not None, emit_state=emit_state),
        grid=(batch, GLA_HEADS),
        in_specs=in_specs,
        out_specs=out_specs,
        out_shape=out_shape,
        scratch_shapes=[pltpu.VMEM((seq, dv), F32)] * 2 + [pltpu.VMEM((2, dv, dk), F32)],
        compiler_params=_params("arbitrary", "arbitrary"),
        name="gla",
    )(*args)
    return res if emit_state else (res[0], None)


def _merge_kernel(x_ref, oa_ref, ob_ref, oc_ref, ma_ref, mb_ref, mc_ref, mod_ref, wbr_ref, wo_ref, o_ref,
                  *, tiles_per_seq, latent):
    r = _mod_row(pl.program_id(0), tiles_per_seq, latent)
    merged = (_sigmoid(ma_ref[...]) * _bdot(oa_ref[...], wbr_ref[0])
              + _sigmoid(mb_ref[...]) * _bdot(ob_ref[...], wbr_ref[1])
              + _sigmoid(mc_ref[...]) * _bdot(oc_ref[...], wbr_ref[2]))
    o_ref[...] = x_ref[...] + mod_ref[2, pl.ds(r, 1), :] * _bdot(merged, wo_ref[...])


def _merge(x, o_a, o_b, o_c, p, mod_l, w_br, w_o, seq_len, latent, tm=512):
    rows = x.shape[0]
    row_blk = lambda w: pl.BlockSpec((tm, w), lambda m: (m, 0))
    gate_blk = lambda j: pl.BlockSpec((tm, D_MODEL), lambda m: (m, OFF_M // D_MODEL + j))
    return pl.pallas_call(
        functools.partial(_merge_kernel, tiles_per_seq=max(seq_len // tm, 1), latent=latent),
        grid=(rows // tm,),
        in_specs=[row_blk(D_MODEL), row_blk(512), row_blk(512), row_blk(512),
                  gate_blk(0), gate_blk(1), gate_blk(2),
                  pl.BlockSpec((6, 8, D_MODEL), lambda m: (0, 0, 0)),
                  pl.BlockSpec((3, 512, D_MODEL), lambda m: (0, 0, 0)),
                  pl.BlockSpec((D_MODEL, D_MODEL), lambda m: (0, 0))],
        out_specs=row_blk(D_MODEL),
        out_shape=jax.ShapeDtypeStruct((rows, D_MODEL), F32),
        compiler_params=_params("arbitrary"),
        name="merge",
    )(x, o_a, o_b, o_c, p, p, p, mod_l, w_br, w_o)


FFN_HALO = 16


def _ffn_kernel(x_ref, xp_ref, xn_ref, mod_ref, n2_ref, wv_ref, wg_ref, fv_ref, fg_ref, wd_ref, o_ref,
                h_scr, acc_scr, *, tm, seq_len, tiles_per_seq, latent):
    n = pl.program_id(1)
    r = _mod_row(pl.program_id(0), tiles_per_seq, latent)

    @pl.when(n == 0)
    def _():
        shift = mod_ref[3, pl.ds(r, 1), :]
        scale = mod_ref[4, pl.ds(r, 1), :]
        nm = lambda x: _norm_mod(x, n2_ref[...], shift, scale).astype(BF16)
        h_scr[0:FFN_HALO, :] = nm(xp_ref[...])
        h_scr[FFN_HALO:FFN_HALO + tm, :] = nm(x_ref[...])
        h_scr[FFN_HALO + tm:, :] = nm(xn_ref[...])
        acc_scr[...] = jnp.zeros_like(acc_scr)

    hh = h_scr[...]
    ext = tm + 2 * FFN_HALO
    t_in_seq = (pl.program_id(0) * tm + lax.broadcasted_iota(jnp.int32, (tm, 1), 0)) % seq_len
    first = t_in_seq == 0
    last = t_in_seq == seq_len - 1

    def conv(u, f):
        up = pltpu.roll(u, 1, 0)[FFN_HALO:FFN_HALO + tm]
        un = pltpu.roll(u, ext - 1, 0)[FFN_HALO:FFN_HALO + tm]
        return (jnp.where(first, 0.0, up) * f[0:1, :] + u[FFN_HALO:FFN_HALO + tm] * f[1:2, :]
                + jnp.where(last, 0.0, un) * f[2:3, :])

    u_val = conv(jnp.dot(hh, wv_ref[...], preferred_element_type=F32), fv_ref[...])
    u_gate = conv(jnp.dot(hh, wg_ref[...], preferred_element_type=F32), fg_ref[...])
    acc_scr[...] += _bdot(_silu(u_gate) * u_val, wd_ref[...])

    @pl.when(n == pl.num_programs(1) - 1)
    def _():
        o_ref[...] = x_ref[...] + mod_ref[5, pl.ds(r, 1), :] * acc_scr[...]


def _ffn(x, mod_l, n2, w_up, f_conv, w_dn, seq_len, latent, tm=1024, tn=256):
    rows = x.shape[0]
    nt = D_FF // tn
    hb = tm // FFN_HALO
    last_hb = rows // FFN_HALO - 1
    kern = functools.partial(_ffn_kernel, tm=tm, seq_len=seq_len, tiles_per_seq=max(seq_len // tm, 1), latent=latent)
    return pl.pallas_call(
        kern,
        grid=(rows // tm, nt),
        in_specs=[pl.BlockSpec((tm, D_MODEL), lambda m, n: (m, 0)),
                  pl.BlockSpec((FFN_HALO, D_MODEL), lambda m, n: (jnp.maximum(m * hb - 1, 0), 0)),
                  pl.BlockSpec((FFN_HALO, D_MODEL), lambda m, n: (jnp.minimum((m + 1) * hb, last_hb), 0)),
                  pl.BlockSpec((6, 8, D_MODEL), lambda m, n: (0, 0, 0)),
                  pl.BlockSpec((1, D_MODEL), lambda m, n: (0, 0)),
                  pl.BlockSpec((D_MODEL, tn), lambda m, n: (0, n)),
                  pl.BlockSpec((D_MODEL, tn), lambda m, n: (0, nt + n)),
                  pl.BlockSpec((3, tn), lambda m, n: (0, n)),
                  pl.BlockSpec((3, tn), lambda m, n: (0, nt + n)),
                  pl.BlockSpec((tn, D_MODEL), lambda m, n: (n, 0))],
        out_specs=pl.BlockSpec((tm, D_MODEL), lambda m, n: (m, 0)),
        out_shape=jax.ShapeDtypeStruct((rows, D_MODEL), F32),
        scratch_shapes=[pltpu.VMEM((tm + 2 * FFN_HALO, D_MODEL), BF16), pltpu.VMEM((tm, D_MODEL), F32)],
        compiler_params=_params("arbitrary", "arbitrary"),
        name="ffn",
    )(x, x, x, mod_l, n2, w_up, w_up, f_conv, f_conv, w_dn)


def _permute_w_in(w_in):
    o = np.cumsum([0, 512, 512, 512, 512, 512, 512, 512, 8, 8, 256, 256, 512, 512, 32, 1024, 1024, 1024])
    piece = lambda i: w_in[:, :, o[i]:o[i + 1]]
    gq = piece(9).reshape(DEPTH, D_MODEL, GLA_HEADS, GLA_KEY_DIM)
    gk = piece(10).reshape(DEPTH, D_MODEL, GLA_HEADS, GLA_KEY_DIM)
    gqk = jnp.concatenate([gq, gk], axis=-1).reshape(DEPTH, D_MODEL, 2 * GLA_HEADS * GLA_KEY_DIM)
    small = jnp.concatenate([piece(7), piece(8), piece(13),
                             jnp.zeros((DEPTH, D_MODEL, PROJ_PAD - OFF_SMALL - 48), w_in.dtype)], axis=-1)
    cols = [piece(i) for i in range(7)] + [gqk, piece(11), piece(12), piece(14), piece(15), piece(16), small]
    return jnp.concatenate(cols, axis=-1).astype(BF16)


def _gla_gate_weights(gla_w, gla_b):
    r, dk = GLA_GATE_RANK, GLA_KEY_DIM
    w = gla_w.reshape(DEPTH, 2, r, GLA_HEADS, dk)
    wg = jnp.zeros((DEPTH, GLA_HEADS, 128, 2, dk), F32)
    for d in range(2):
        wg = wg.at[:, :, SMALL_LR + d * r:SMALL_LR + (d + 1) * r, d, :].set(jnp.transpose(w[:, d], (0, 2, 1, 3)))
    bg = jnp.transpose(gla_b.reshape(DEPTH, 2, GLA_HEADS, dk), (0, 2, 1, 3)).reshape(DEPTH, GLA_HEADS, 1, 2 * dk)
    return wg.reshape(DEPTH, GLA_HEADS, 128, 2 * dk).astype(BF16), bg.astype(F32)


def _lane_row(values, offset):
    n = values.shape[-1]
    return jnp.zeros((DEPTH, 1, 128), F32).at[:, 0, offset:offset + n].set(values.astype(F32))


def kernel(x_prompt, x_sample, cache_k, cache_v, state_dn, state_gla, c, c_ctx, w_ada, b_ada, norm1, w_in,
           na_q_norm, na_k_norm, na_rpb, dn_conv, dn_a_log, dn_dt_bias, dn_out_norm, gla_w_gate, gla_b_gate,
           gla_out_norm, w_branch, w_out, norm2, w_up, ffn_conv, w_down):
    batch, seq, _ = x_prompt.shape
    dbatch, dseq, _ = x_sample.shape
    past = cache_k.shape[2]

    cvec8 = jnp.concatenate([c_ctx[None, :], c, jnp.zeros((8 - 1 - dbatch, D_MODEL), F32)], axis=0)
    mod = _adaln(cvec8, w_ada, b_ada)

    w_in_p = _permute_w_in(w_in)
    w_br = w_branch.astype(BF16)
    w_o = w_out.astype(BF16)
    w_up_b = w_up.astype(BF16)
    w_dn_b = w_down.astype(BF16)
    bias = _na_bias_table(na_rpb)
    wg, bg = _gla_gate_weights(gla_w_gate, gla_b_gate)
    alog128 = _lane_row(dn_a_log.reshape(DEPTH, 2 * DN_HEADS), SMALL_DECAY)
    dtb128 = _lane_row(dn_dt_bias.reshape(DEPTH, 2 * DN_HEADS), SMALL_DECAY)
    ck = cache_k.reshape(dbatch, DEPTH, past, NA_HEADS * NA_HEAD_DIM)
    cv = cache_v.reshape(dbatch, DEPTH, past, NA_HEADS * NA_HEAD_DIM)
    s0_gla_t = jnp.swapaxes(state_gla, -1, -2)

    y_p = x_prompt.reshape(batch * seq, D_MODEL)
    y_s = x_sample.reshape(dbatch * dseq, D_MODEL)
    new_k, new_v, new_dn, new_gla = [], [], [], []
    for l in range(DEPTH):
        qg, kg = na_q_norm[l][None, :], na_k_norm[l][None, :]
        n1, n2 = norm1[l][None, :], norm2[l][None, :]
        dn_g, gla_g = dn_out_norm[l][None, :], gla_out_norm[l][None, :]
        for latent in (False, True):
            x = y_s if latent else y_p
            b_, t_ = (dbatch, dseq) if latent else (batch, seq)
            p = _inproj(x, mod[l], n1, w_in_p[l], t_, latent)
            if latent:
                o_a = _lat_attn(p, ck, cv, bias, qg, kg, l, b_, t_)
            else:
                o_a, k_l, v_l = _ctx_attn(p, qg, kg, b_, t_)
            o_b, dn_l = _deltanet(p, dn_conv[l], alog128[l], dtb128[l], dn_g,
                                  state_dn if latent else None, l, b_, t_, not latent)
            o_c, gla_l = _gla(p, wg[l], bg[l], gla_g, s0_gla_t if latent else None, l, b_, t_, not latent)
            x = _merge(x, o_a, o_b, o_c, p, mod[l], w_br[l], w_o[l], t_, latent)
            x = _ffn(x, mod[l], n2, w_up_b[l], ffn_conv[l], w_dn_b[l], t_, latent)
            if latent:
                y_s = x
            else:
                y_p = x
                new_k.append(k_l.reshape(batch, seq, NA_HEADS, NA_HEAD_DIM))
                new_v.append(v_l.reshape(batch, seq, NA_HEADS, NA_HEAD_DIM))
                new_dn.append(dn_l)
                new_gla.append(jnp.swapaxes(gla_l, -1, -2))
    return (y_p.reshape(batch, seq, D_MODEL), y_s.reshape(dbatch, dseq, D_MODEL),
            jnp.stack(new_k, axis=1), jnp.stack(new_v, axis=1),
            jnp.stack(new_dn, axis=1), jnp.stack(new_gla, axis=1))
```

```python
import functools

import numpy as np
import jax
import jax.numpy as jnp
from jax import lax
from jax.experimental import pallas as pl
from jax.experimental.pallas import tpu as pltpu

F32 = jnp.float32
BF16 = jnp.bfloat16

D_MODEL = 1024
DEPTH = 2
GRID_W = 64
EPS = 1e-6
NA_HEADS = 8
NA_HEAD_DIM = 64
NA_WIN_H = 8
NA_WIN_W = 16
DN_HEADS = 4
DN_HEAD_DIM = 128
DN_CHUNK = 64
GLA_HEADS = 4
GLA_KEY_DIM = 64
GLA_VAL_DIM = 128
GLA_GATE_RANK = 16
GLA_TAU = 16.0
GLA_CHUNK = 32
D_FF = 2816

OFF_NA_Q, OFF_NA_K, OFF_NA_V = 0, 512, 1024
OFF_DN_Q, OFF_DN_K, OFF_DN_V, OFF_DN_GATE = 1536, 2048, 2560, 3072
OFF_GLA_QK, OFF_GLA_V, OFF_GLA_GATE = 3584, 4096, 4608
OFF_M = 5120
OFF_SMALL = 8192
PROJ_PAD = 8320
SMALL_BETA, SMALL_DECAY, SMALL_LR = 0, 8, 16

GROUP = 256
NEG = -1e30
VMEM_LIMIT = 56 * 1024 * 1024

NA_QROWS = 4
NA_KROWS = 12


def _bdot(a, b):
    return jnp.dot(a.astype(BF16), b.astype(BF16), preferred_element_type=F32)


def _bdot_nt(a, b):
    return lax.dot_general(a.astype(BF16), b.astype(BF16), (((1,), (1,)), ((), ())), preferred_element_type=F32)


def _bdot_tn(a, b):
    return lax.dot_general(a.astype(BF16), b.astype(BF16), (((0,), (0,)), ((), ())), preferred_element_type=F32)


def _split3(x):
    hi = x.astype(BF16)
    r = x - hi.astype(F32)
    mid = r.astype(BF16)
    lo = (r - mid.astype(F32)).astype(BF16)
    return hi, mid, lo


def _dot_mask_f32(t, x):
    hi, mid, lo = _split3(x)
    d = lambda p: jnp.dot(t, p, preferred_element_type=F32)
    return d(hi) + d(mid) + d(lo)


def _dot3(a, b):
    ah = a.astype(BF16)
    al = (a - ah.astype(F32)).astype(BF16)
    bh = b.astype(BF16)
    bl = (b - bh.astype(F32)).astype(BF16)
    d = lambda p, q: jnp.dot(p, q, preferred_element_type=F32)
    return d(ah, bh) + d(ah, bl) + d(al, bh)


def _sigmoid(x):
    return 1.0 / (1.0 + jnp.exp(-x))


def _silu(x):
    return x * _sigmoid(x)


def _softplus(x):
    return jnp.maximum(x, 0.0) + jnp.log(1.0 + jnp.exp(-jnp.abs(x)))


def _params(*sem):
    return pltpu.CompilerParams(dimension_semantics=sem, vmem_limit_bytes=VMEM_LIMIT)


def _adaln_kernel(c_ref, w_ref, b_ref, o_ref):
    o_ref[...] = _bdot(_silu(c_ref[...]), w_ref[...]) + b_ref[...]


def _adaln(cvec8, w_ada, b_ada):
    return pl.pallas_call(
        _adaln_kernel,
        grid=(DEPTH, 6),
        in_specs=[pl.BlockSpec((8, D_MODEL), lambda l, j: (0, 0)),
                  pl.BlockSpec((None, D_MODEL, D_MODEL), lambda l, j: (l, 0, j)),
                  pl.BlockSpec((None, None, 1, D_MODEL), lambda l, j: (l, j, 0, 0))],
        out_specs=pl.BlockSpec((None, None, 8, D_MODEL), lambda l, j: (l, j, 0, 0)),
        out_shape=jax.ShapeDtypeStruct((DEPTH, 6, 8, D_MODEL), F32),
        compiler_params=_params("arbitrary", "arbitrary"),
        name="adaln",
    )(cvec8, w_ada, b_ada.reshape(DEPTH, 6, 1, D_MODEL))


def _mod_row(tile, tiles_per_seq, latent):
    return (1 + tile // tiles_per_seq) if latent else 0


def _norm_mod(x, gain, shift, scale):
    y = x * lax.rsqrt(jnp.mean(x * x, axis=-1, keepdims=True) + EPS) * gain
    return y * (1.0 + scale) + shift


def _inproj_kernel(x_ref, mod_ref, n1_ref, w_ref, o_ref, h_scr, *, tiles_per_seq, latent):
    @pl.when(pl.program_id(1) == 0)
    def _():
        r = _mod_row(pl.program_id(0), tiles_per_seq, latent)
        h = _norm_mod(x_ref[...], n1_ref[...], mod_ref[0, pl.ds(r, 1), :], mod_ref[1, pl.ds(r, 1), :])
        h_scr[...] = h.astype(BF16)

    o_ref[...] = jnp.dot(h_scr[...], w_ref[...], preferred_element_type=F32)


def _inproj(x, mod_l, n1, w_in_l, seq_len, latent, tm=1024, tn=640):
    rows = x.shape[0]
    kern = functools.partial(_inproj_kernel, tiles_per_seq=max(seq_len // tm, 1), latent=latent)
    return pl.pallas_call(
        kern,
        grid=(rows // tm, PROJ_PAD // tn),
        in_specs=[pl.BlockSpec((tm, D_MODEL), lambda m, n: (m, 0)),
                  pl.BlockSpec((6, 8, D_MODEL), lambda m, n: (0, 0, 0)),
                  pl.BlockSpec((1, D_MODEL), lambda m, n: (0, 0)),
                  pl.BlockSpec((D_MODEL, tn), lambda m, n: (0, n))],
        out_specs=pl.BlockSpec((tm, tn), lambda m, n: (m, n)),
        out_shape=jax.ShapeDtypeStruct((rows, PROJ_PAD), F32),
        scratch_shapes=[pltpu.VMEM((tm, D_MODEL), BF16)],
        compiler_params=_params("arbitrary", "arbitrary"),
        name="inproj",
    )(x, mod_l, n1, w_in_l)


def _head_rms(x, gain):
    return x * lax.rsqrt(jnp.mean(x * x, axis=-1, keepdims=True) + EPS) * gain


def _ctx_attn_kernel(q_ref, k_ref, v_ref, qg_ref, kg_ref, o_ref, ka_ref, va_ref):
    for h in range(NA_HEADS):
        sl = slice(h * NA_HEAD_DIM, (h + 1) * NA_HEAD_DIM)
        qn = _head_rms(q_ref[:, sl], qg_ref[...]) * (NA_HEAD_DIM ** -0.5)
        kn = _head_rms(k_ref[:, sl], kg_ref[...])
        v = v_ref[:, sl]
        ka_ref[:, sl] = kn
        s = _bdot_nt(qn, kn)
        p = jnp.exp(s - jnp.max(s, axis=-1, keepdims=True))
        o_ref[:, sl] = _bdot(p, v) / jnp.sum(p, axis=-1, keepdims=True)
    va_ref[...] = v_ref[...]


def _ctx_attn(p_ctx, qg, kg, batch, seq):
    rows = batch * seq
    blk = lambda j: pl.BlockSpec((seq, 512), lambda b: (b, j))
    out = jax.ShapeDtypeStruct((rows, 512), F32)
    return pl.pallas_call(
        _ctx_attn_kernel,
        grid=(batch,),
        in_specs=[blk(OFF_NA_Q // 512), blk(OFF_NA_K // 512), blk(OFF_NA_V // 512),
                  pl.BlockSpec((1, NA_HEAD_DIM), lambda b: (0, 0)),
                  pl.BlockSpec((1, NA_HEAD_DIM), lambda b: (0, 0))],
        out_specs=[blk(0), blk(0), blk(0)],
        out_shape=[out, out, out],
        compiler_params=_params("arbitrary"),
        name="ctx_attn",
    )(p_ctx, p_ctx, p_ctx, qg, kg)


def _na_tables():
    rows = 32
    nblk = rows // NA_QROWS
    dr = np.zeros((nblk, NA_QROWS, NA_KROWS), np.int32)
    ok = np.zeros((nblk, NA_QROWS, NA_KROWS), bool)
    for m in range(nblk):
        kstart = int(np.clip(NA_QROWS * m - NA_WIN_H // 2, 0, rows - NA_KROWS))
        for i in range(NA_QROWS):
            r = NA_QROWS * m + i
            rs = int(np.clip(r - NA_WIN_H // 2, 0, rows - NA_WIN_H))
            for j in range(NA_KROWS):
                kr = kstart + j
                ok[m, i, j] = rs <= kr < rs + NA_WIN_H
                dr[m, i, j] = np.clip(kr - r + NA_WIN_H - 1, 0, 2 * NA_WIN_H - 2)
    for m in range(2, nblk - 1):
        assert (dr[m][ok[m]] == dr[1][ok[1]]).all() and (ok[m] == ok[1]).all()
    cls = [0, 1, nblk - 1]
    qc = np.arange(GRID_W)
    cs = np.clip(qc - NA_WIN_W // 2, 0, GRID_W - NA_WIN_W)
    col_ok = (qc[None, :] >= cs[:, None]) & (qc[None, :] < cs[:, None] + NA_WIN_W)
    return dr[cls], ok[cls], col_ok


def _na_bias_table(rpb):
    dr, ok, col_ok = _na_tables()
    nd = 2 * NA_WIN_W - 1
    lo = GRID_W - NA_WIN_W
    w = jnp.full(rpb.shape[:3] + (2 * GRID_W,), NEG, F32)
    w = w.at[..., lo:lo + nd].set(rpb.astype(F32))
    t = jnp.tile(w, (1, 1, 1, GRID_W))[..., :GRID_W * (2 * GRID_W - 1)]
    t = t.reshape(rpb.shape[:3] + (GRID_W, 2 * GRID_W - 1))[..., GRID_W - 1:]
    t = jnp.where(col_ok, t, NEG)
    big = t[:, :, dr]
    big = jnp.where(ok[..., None, None], big, NEG)
    big = jnp.transpose(big, (0, 2, 1, 3, 5, 4, 6))
    return big.reshape(rpb.shape[0], 3, NA_HEADS, NA_QROWS * GRID_W, NA_KROWS * GRID_W)


def _lat_attn_kernel(q_ref, k_ref, v_ref, ck_ref, cv_ref, bias_ref, qg_ref, kg_ref, o_ref, kn_scr, *, rows):
    m = pl.program_id(1)

    @pl.when(m == 0)
    def _():
        for h in range(NA_HEADS):
            sl = slice(h * NA_HEAD_DIM, (h + 1) * NA_HEAD_DIM)
            kn_scr[:, sl] = _head_rms(k_ref[:, sl], kg_ref[...]).astype(BF16)

    kstart = jnp.clip(NA_QROWS * m - NA_WIN_H // 2, 0, rows - NA_KROWS) * GRID_W
    kstart = pl.multiple_of(kstart, 256)
    nk = NA_KROWS * GRID_W
    for h in range(NA_HEADS):
        sl = slice(h * NA_HEAD_DIM, (h + 1) * NA_HEAD_DIM)
        qn = (_head_rms(q_ref[:, sl], qg_ref[...]) * (NA_HEAD_DIM ** -0.5)).astype(BF16)
        kl = kn_scr[pl.ds(kstart, nk), sl]
        vl = v_ref[pl.ds(kstart, nk), sl]
        s_loc = _bdot_nt(qn, kl) + bias_ref[h]
        s_ctx = _bdot_nt(qn, ck_ref[:, sl])
        mx = jnp.maximum(jnp.max(s_loc, axis=-1, keepdims=True), jnp.max(s_ctx, axis=-1, keepdims=True))
        p_loc = jnp.exp(s_loc - mx)
        p_ctx = jnp.exp(s_ctx - mx)
        den = jnp.sum(p_loc, axis=-1, keepdims=True) + jnp.sum(p_ctx, axis=-1, keepdims=True)
        o_ref[:, sl] = (_bdot(p_loc, vl) + _bdot(p_ctx, cv_ref[:, sl])) / den


def _lat_attn(p_lat, cache_k, cache_v, bias, qg, kg, layer, batch, seq):
    rows = seq // GRID_W
    nblk = rows // NA_QROWS
    qtok = NA_QROWS * GRID_W
    past = cache_k.shape[2]
    cls = lambda m: jnp.where(m == 0, 0, jnp.where(m == nblk - 1, 2, 1))
    cache_spec = pl.BlockSpec((None, None, past, 512), lambda b, m: (b, layer, 0, 0))
    return pl.pallas_call(
        functools.partial(_lat_attn_kernel, rows=rows),
        grid=(batch, nblk),
        in_specs=[pl.BlockSpec((qtok, 512), lambda b, m: (b * nblk + m, OFF_NA_Q // 512)),
                  pl.BlockSpec((seq, 512), lambda b, m: (b, OFF_NA_K // 512)),
                  pl.BlockSpec((seq, 512), lambda b, m: (b, OFF_NA_V // 512)),
                  cache_spec, cache_spec,
                  pl.BlockSpec((None, None, NA_HEADS, qtok, NA_KROWS * GRID_W),
                               lambda b, m: (layer, cls(m), 0, 0, 0)),
                  pl.BlockSpec((1, NA_HEAD_DIM), lambda b, m: (0, 0)),
                  pl.BlockSpec((1, NA_HEAD_DIM), lambda b, m: (0, 0))],
        out_specs=pl.BlockSpec((qtok, 512), lambda b, m: (b * nblk + m, 0)),
        out_shape=jax.ShapeDtypeStruct((batch * seq, 512), F32),
        scratch_shapes=[pltpu.VMEM((seq, 512), BF16)],
        compiler_params=_params("arbitrary", "arbitrary"),
        name="lat_attn",
    )(p_lat, p_lat, p_lat, cache_k, cache_v, bias, qg, kg)


def _group_masks(chunk):
    r = lax.broadcasted_iota(jnp.int32, (GROUP, GROUP), 0)
    c = lax.broadcasted_iota(jnp.int32, (GROUP, GROUP), 1)
    same = (r // chunk) == (c // chunk)
    return (same & (c <= r), same & (c >= r), same & (c < r), same & (c > r), same)


def _halving_masks(chunk, lower):
    r = lax.broadcasted_iota(jnp.int32, (GROUP, GROUP), 0)
    c = lax.broadcasted_iota(jnp.int32, (GROUP, GROUP), 1)
    if not lower:
        r, c = c, r
    masks = []
    s = 1
    while s < chunk:
        masks.append(((r // (2 * s)) == (c // (2 * s))) & ((r // s) % 2 == 1) & ((c // s) % 2 == 0))
        s *= 2
    return masks


def _group_eye():
    r = lax.broadcasted_iota(jnp.int32, (GROUP, GROUP), 0)
    c = lax.broadcasted_iota(jnp.int32, (GROUP, GROUP), 1)
    return jnp.where(r == c, 1.0, 0.0)


def _sel_col(a, lane):
    idx = lax.broadcasted_iota(jnp.int32, a.shape, 1)
    return jnp.sum(jnp.where(idx == lane, a, 0.0), axis=1, keepdims=True)


def _sel_row(a, row):
    idx = lax.broadcasted_iota(jnp.int32, a.shape, 0)
    return jnp.sum(jnp.where(idx == row, a, 0.0), axis=0, keepdims=True)


def _conv3_silu(x, w):
    t = x.shape[0]
    row = lax.broadcasted_iota(jnp.int32, x.shape, 0)
    xm = jnp.where(row == 0, 0.0, pltpu.roll(x, 1, 0))
    xp = jnp.where(row == t - 1, 0.0, pltpu.roll(x, t - 1, 0))
    return _silu(xm * w[0:1, :] + x * w[1:2, :] + xp * w[2:3, :])


def _deltanet_kernel(*refs, hp, has_s0, emit_state):
    (q_ref, k_ref, v_ref, gate_ref, sm_ref, cq_ref, ck_ref, cv_ref, alog_ref, dtb_ref, og_ref), refs = refs[:11], refs[11:]
    if has_s0:
        s0_ref, refs = refs[0], refs[1:]
    o_ref, refs = refs[0], refs[1:]
    if emit_state:
        sfin_ref, refs = refs[0], refs[1:]
    qn_scr, kn_scr, vc_scr, of_scr, ob_scr, s_scr = refs

    head0 = pl.program_id(1) * hp
    seq = q_ref.shape[0]
    hd = DN_HEAD_DIM
    ngroups = seq // GROUP
    nchunk = GROUP // DN_CHUNK

    q = _conv3_silu(q_ref[...], cq_ref[...])
    k = _conv3_silu(k_ref[...], ck_ref[...])
    vc_scr[...] = _conv3_silu(v_ref[...], cv_ref[...])
    for j in range(hp):
        sl = slice(j * hd, (j + 1) * hd)
        qj, kj = q[:, sl], k[:, sl]
        qn_scr[:, sl] = qj * lax.rsqrt(jnp.sum(qj * qj, axis=-1, keepdims=True) + EPS) * (hd ** -0.5)
        kn_scr[:, sl] = kj * lax.rsqrt(jnp.sum(kj * kj, axis=-1, keepdims=True) + EPS)

    if has_s0:
        s_scr[...] = s0_ref[...]
    else:
        s_scr[...] = jnp.zeros_like(s_scr)

    m_fi, m_bi, m_fs, m_bs, m_same = _group_masks(DN_CHUNK)
    incl = (m_fi, m_bi)
    strict = (m_fs, m_bs)
    tri = (m_fi.astype(BF16), m_bi.astype(BF16))
    ones_bd = m_same.astype(BF16)
    halving = (_halving_masks(DN_CHUNK, True), _halving_masks(DN_CHUNK, False))

    eye = _group_eye()
    chains = [(d, j) for d in range(2) for j in range(hp)]
    nc = len(chains)

    def group_step(t, carry):
        shared = []
        for d in range(2):
            gi = t if d == 0 else ngroups - 1 - t
            r0 = pl.multiple_of(gi * GROUP, GROUP)
            sm = sm_ref[pl.ds(r0, GROUP), :]
            g128 = -jnp.exp(alog_ref[...]) * _softplus(sm + dtb_ref[...])
            gcum = _dot_mask_f32(tri[d], g128)
            gtot = _dot_mask_f32(ones_bd, g128)
            shared.append((r0, gcum, gtot, gcum.T, _sigmoid(sm)))

        qg, kg, vg, beta, gc_col, gt_col, dec = [], [], [], [], [], [], []
        for d, j in chains:
            r0, gcum, gtot, gcum_t, beta128 = shared[d]
            sl = slice(j * hd, (j + 1) * hd)
            lane_g = SMALL_DECAY + DN_HEADS * d + head0 + j
            gc_col.append(_sel_col(gcum, lane_g))
            gt_col.append(_sel_col(gtot, lane_g))
            beta.append(_sel_col(beta128, SMALL_BETA + DN_HEADS * d + head0 + j))
            dec.append(jnp.exp(jnp.where(incl[d], gc_col[-1] - _sel_row(gcum_t, lane_g), NEG)))
            qg.append(qn_scr[pl.ds(r0, GROUP), sl])
            kg.append(kn_scr[pl.ds(r0, GROUP), sl])
            vg.append(vc_scr[pl.ds(r0, GROUP), sl])
        rng = range(nc)
        kk = [_bdot_nt(kg[i], kg[i]) for i in rng]
        qk = [_bdot_nt(qg[i], kg[i]) for i in rng]
        low = [jnp.where(strict[chains[i][0]], beta[i] * kk[i] * dec[i], 0.0) for i in rng]
        attn = [qk[i] * dec[i] for i in rng]
        e_gc = [jnp.exp(gc_col[i]) for i in rng]
        x = [jnp.concatenate([vg[i] * beta[i], kg[i] * (beta[i] * e_gc[i])], axis=1) for i in rng]
        inv = [eye - jnp.where(halving[chains[i][0]][0], low[i], 0.0) for i in rng]
        for lvl in range(1, len(halving[0])):
            tmp = [_bdot(inv[i], jnp.where(halving[chains[i][0]][lvl], low[i], 0.0)) for i in rng]
            upd = [_bdot(tmp[i], inv[i]) for i in rng]
            inv = [inv[i] - upd[i] for i in rng]
        x = [_bdot(inv[i], x[i]) for i in rng]
        qd = [qg[i] * e_gc[i] for i in rng]
        kd = [kg[i] * jnp.exp(gt_col[i] - gc_col[i]) for i in rng]
        s = [s_scr[d, j] for d, j in chains]
        o_parts = [[None] * nchunk for _ in rng]
        for cc in range(nchunk):
            rs = []
            for d, _ in chains:
                c = cc if d == 0 else nchunk - 1 - cc
                rs.append(slice(c * DN_CHUNK, (c + 1) * DN_CHUNK))
            ws = [_bdot(x[i][rs[i], hd:], s[i]) for i in rng]
            qs = [_bdot(qd[i][rs[i]], s[i]) for i in rng]
            v_new = [x[i][rs[i], :hd] - ws[i] for i in rng]
            av = [_bdot(attn[i][rs[i], rs[i]], v_new[i]) for i in rng]
            kv = [_bdot_tn(kd[i][rs[i]], v_new[i]) for i in rng]
            for i in rng:
                o_parts[i][rs[i].start // DN_CHUNK] = qs[i] + av[i]
                s[i] = s[i] * jnp.exp(gt_col[i][rs[i].start:rs[i].start + 1, :]) + kv[i]
        for i, (d, j) in enumerate(chains):
            s_scr[d, j] = s[i]
            o_scr = of_scr if d == 0 else ob_scr
            o_scr[pl.ds(shared[d][0], GROUP), j * hd:(j + 1) * hd] = jnp.concatenate(o_parts[i], axis=0)
        return carry

    if ngroups == 1:
        group_step(0, 0)
    else:
        lax.fori_loop(0, ngroups, group_step, 0)

    for j in range(hp):
        sl = slice(j * hd, (j + 1) * hd)
        o = of_scr[:, sl] + ob_scr[:, sl]
        o = o * lax.rsqrt(jnp.mean(o * o, axis=-1, keepdims=True) + EPS) * og_ref[...]
        o_ref[:, sl] = o * _silu(gate_ref[:, sl])
    if emit_state:
        sfin_ref[...] = s_scr[...]


def _deltanet(p, conv_w, alog128, dtb128, out_g, s0, layer, batch, seq, emit_state, hp):
    rows = batch * seq
    hd = DN_HEAD_DIM
    w = hp * hd
    col = lambda off: pl.BlockSpec((seq, w), lambda b, h: (b, off // w + h))
    cw = lambda part: pl.BlockSpec((3, w), lambda b, h: (0, part * (DN_HEADS // hp) + h))
    row128 = pl.BlockSpec((1, 128), lambda b, h: (0, 0))
    in_specs = [col(OFF_DN_Q), col(OFF_DN_K), col(OFF_DN_V), col(OFF_DN_GATE),
                pl.BlockSpec((seq, 128), lambda b, h: (b, OFF_SMALL // 128)),
                cw(0), cw(1), cw(2), row128, row128, row128]
    args = [p, p, p, p, p, conv_w, conv_w, conv_w, alog128, dtb128, out_g]
    if s0 is not None:
        in_specs.append(pl.BlockSpec((None, None, 2, hp, hd, hd), lambda b, h: (b, layer, 0, h, 0, 0)))
        args.append(s0)
    out_specs = [pl.BlockSpec((seq, w), lambda b, h: (b, h))]
    out_shape = [jax.ShapeDtypeStruct((rows, DN_HEADS * hd), F32)]
    if emit_state:
        out_specs.append(pl.BlockSpec((None, 2, hp, hd, hd), lambda b, h: (b, 0, h, 0, 0)))
        out_shape.append(jax.ShapeDtypeStruct((batch, 2, DN_HEADS, hd, hd), F32))
    res = pl.pallas_call(
        functools.partial(_deltanet_kernel, hp=hp, has_s0=s0 is not None, emit_state=emit_state),
        grid=(batch, DN_HEADS // hp),
        in_specs=in_specs,
        out_specs=out_specs,
        out_shape=out_shape,
        scratch_shapes=[pltpu.VMEM((seq, w), F32)] * 5 + [pltpu.VMEM((2, hp, hd, hd), F32)],
        compiler_params=_params("arbitrary", "arbitrary"),
        name="deltanet",
    )(*args)
    return res if emit_state else (res[0], None)


def _gla_kernel(*refs, hp, has_s0, emit_state):
    (qk_ref, v_ref, gate_ref, sm_ref, wg_ref, bg_ref, og_ref), refs = refs[:7], refs[7:]
    if has_s0:
        s0_ref, refs = refs[0], refs[1:]
    o_ref, refs = refs[0], refs[1:]
    if emit_state:
        sfin_ref, refs = refs[0], refs[1:]
    of_scr, ob_scr, s_scr = refs

    seq = qk_ref.shape[0]
    ngroups = seq // GROUP
    nchunk = GROUP // GLA_CHUNK
    dk = GLA_KEY_DIM

    if has_s0:
        s_scr[...] = s0_ref[...]
    else:
        s_scr[...] = jnp.zeros_like(s_scr)

    m_fi, m_bi, _, _, m_same = _group_masks(GLA_CHUNK)
    incl = (m_fi, m_bi)
    tri = (m_fi.astype(BF16), m_bi.astype(BF16))
    ones_bd = m_same.astype(BF16)

    chains = [(d, j) for d in range(2) for j in range(hp)]
    rng = range(len(chains))

    def group_step(t, carry):
        shared = []
        for d in range(2):
            gi = t if d == 0 else ngroups - 1 - t
            r0 = pl.multiple_of(gi * GROUP, GROUP)
            gl = _bdot(sm_ref[pl.ds(r0, GROUP), :], wg_ref[...]) + bg_ref[...]
            log_a = -_softplus(-gl) / GLA_TAU
            b_all = _dot_mask_f32(tri[d], log_a)
            btot_all = _dot_mask_f32(ones_bd, log_a)
            shared.append((r0, b_all, btot_all))
        b, btot, kg, vg, qe = [], [], [], [], []
        for d, j in chains:
            r0, b_all, btot_all = shared[d]
            sl = slice(j * 128, (j + 1) * 128)
            lanes = slice(j * 128 + d * dk, j * 128 + (d + 1) * dk)
            b.append(b_all[:, lanes])
            btot.append(btot_all[:, lanes])
            qk = qk_ref[pl.ds(r0, GROUP), sl]
            kg.append(qk[:, dk:])
            vg.append(v_ref[pl.ds(r0, GROUP), sl])
            qe.append(qk[:, :dk] * (dk ** -0.5) * jnp.exp(b[-1]))
        attn = [jnp.where(incl[chains[i][0]], _bdot_nt(qe[i], kg[i] * jnp.exp(-b[i])), 0.0) for i in rng]
        o_in = [_bdot(attn[i], vg[i]) for i in rng]
        kd = [kg[i] * jnp.exp(btot[i] - b[i]) for i in rng]
        st = [s_scr[d, j] for d, j in chains]
        o_parts = [[None] * nchunk for _ in rng]
        for cc in range(nchunk):
            rs = []
            for d, _ in chains:
                c = cc if d == 0 else nchunk - 1 - cc
                rs.append(slice(c * GLA_CHUNK, (c + 1) * GLA_CHUNK))
            qs = [_bdot_nt(qe[i][rs[i]], st[i]) for i in rng]
            vk = [_bdot_tn(vg[i][rs[i]], kd[i][rs[i]]) for i in rng]
            for i in rng:
                o_parts[i][rs[i].start // GLA_CHUNK] = o_in[i][rs[i]] + qs[i]
                st[i] = st[i] * jnp.exp(btot[i][rs[i].start:rs[i].start + 1, :]) + vk[i]
        for i, (d, j) in enumerate(chains):
            s_scr[d, j] = st[i]
            o_scr = of_scr if d == 0 else ob_scr
            o_scr[pl.ds(shared[d][0], GROUP), j * 128:(j + 1) * 128] = jnp.concatenate(o_parts[i], axis=0)
        return carry

    if ngroups == 1:
        group_step(0, 0)
    else:
        lax.fori_loop(0, ngroups, group_step, 0)

    for j in range(hp):
        sl = slice(j * 128, (j + 1) * 128)
        o = of_scr[:, sl] + ob_scr[:, sl]
        o = o * lax.rsqrt(jnp.mean(o * o, axis=-1, keepdims=True) + EPS) * og_ref[...]
        o_ref[:, sl] = o * _silu(gate_ref[:, sl])
    if emit_state:
        sfin_ref[...] = s_scr[...]


def _gla(p, wg, bg, out_g, s0t, layer, batch, seq, emit_state, hp):
    rows = batch * seq
    dv, dk = GLA_VAL_DIM, GLA_KEY_DIM
    w = hp * 128
    col = lambda off: pl.BlockSpec((seq, w), lambda b, h: (b, off // w + h))
    in_specs = [col(OFF_GLA_QK), col(OFF_GLA_V), col(OFF_GLA_GATE),
                pl.BlockSpec((seq, 128), lambda b, h: (b, OFF_SMALL // 128)),
                pl.BlockSpec((128, w), lambda b, h: (0, h)),
                pl.BlockSpec((1, w), lambda b, h: (0, h)),
                pl.BlockSpec((1, 128), lambda b, h: (0, 0))]
    args = [p, p, p, p, wg, bg, out_g]
    if s0t is not None:
        in_specs.append(pl.BlockSpec((None, None, 2, hp, dv, dk), lambda b, h: (b, layer, 0, h, 0, 0)))
        args.append(s0t)
    out_specs = [pl.BlockSpec((seq, w), lambda b, h: (b, h))]
    out_shape = [jax.ShapeDtypeStruct((rows, GLA_HEADS * dv), F32)]
    if emit_state:
        out_specs.append(pl.BlockSpec((None, 2, hp, dv, dk), lambda b, h: (b, 0, h, 0, 0)))
        out_shape.append(jax.ShapeDtypeStruct((batch, 2, GLA_HEADS, dv, dk), F32))
    res = pl.pallas_call(
        functools.partial(_gla_kernel, hp=hp, has_s0=s0t is not None, emit_state=emit_state),
        grid=(batch, GLA_HEADS // hp),
        in_specs=in_specs,
        out_specs=out_specs,
        out_shape=out_shape,
        scratch_shapes=[pltpu.VMEM((seq, w), F32)] * 2 + [pltpu.VMEM((2, hp, dv, dk), F32)],
        compiler_params=_params("arbitrary", "arbitrary"),
        name="gla",
    )(*args)
    return res if emit_state else (res[0], None)


def _merge_kernel(x_ref, oa_ref, ob_ref, oc_ref, ma_ref, mb_ref, mc_ref, mod_ref, wbr_ref, wo_ref, o_ref,
                  *, tiles_per_seq, latent):
    r = _mod_row(pl.program_id(0), tiles_per_seq, latent)
    merged = (_sigmoid(ma_ref[...]) * _bdot(oa_ref[...], wbr_ref[0])
              + _sigmoid(mb_ref[...]) * _bdot(ob_ref[...], wbr_ref[1])
              + _sigmoid(mc_ref[...]) * _bdot(oc_ref[...], wbr_ref[2]))
    o_ref[...] = x_ref[...] + mod_ref[2, pl.ds(r, 1), :] * _bdot(merged, wo_ref[...])


def _merge(x, o_a, o_b, o_c, p, mod_l, w_br, w_o, seq_len, latent, tm=512):
    rows = x.shape[0]
    row_blk = lambda w: pl.BlockSpec((tm, w), lambda m: (m, 0))
    gate_blk = lambda j: pl.BlockSpec((tm, D_MODEL), lambda m: (m, OFF_M // D_MODEL + j))
    return pl.pallas_call(
        functools.partial(_merge_kernel, tiles_per_seq=max(seq_len // tm, 1), latent=latent),
        grid=(rows // tm,),
        in_specs=[row_blk(D_MODEL), row_blk(512), row_blk(512), row_blk(512),
                  gate_blk(0), gate_blk(1), gate_blk(2),
                  pl.BlockSpec((6, 8, D_MODEL), lambda m: (0, 0, 0)),
                  pl.BlockSpec((3, 512, D_MODEL), lambda m: (0, 0, 0)),
                  pl.BlockSpec((D_MODEL, D_MODEL), lambda m: (0, 0))],
        out_specs=row_blk(D_MODEL),
        out_shape=jax.ShapeDtypeStruct((rows, D_MODEL), F32),
        compiler_params=_params("arbitrary"),
        name="merge",
    )(x, o_a, o_b, o_c, p, p, p, mod_l, w_br, w_o)


FFN_HALO = 16


def _ffn_kernel(x_ref, xp_ref, xn_ref, mod_ref, n2_ref, wv_ref, wg_ref, fv_ref, fg_ref, wd_ref, o_ref,
                h_scr, acc_scr, *, tm, seq_len, tiles_per_seq, latent):
    n = pl.program_id(1)
    r = _mod_row(pl.program_id(0), tiles_per_seq, latent)

    @pl.when(n == 0)
    def _():
        shift = mod_ref[3, pl.ds(r, 1), :]
        scale = mod_ref[4, pl.ds(r, 1), :]
        nm = lambda x: _norm_mod(x, n2_ref[...], shift, scale).astype(BF16)
        h_scr[0:FFN_HALO, :] = nm(xp_ref[...])
        h_scr[FFN_HALO:FFN_HALO + tm, :] = nm(x_ref[...])
        h_scr[FFN_HALO + tm:, :] = nm(xn_ref[...])
        acc_scr[...] = jnp.zeros_like(acc_scr)

    hh = h_scr[...]
    ext = tm + 2 * FFN_HALO
    t_in_seq = (pl.program_id(0) * tm + lax.broadcasted_iota(jnp.int32, (tm, 1), 0)) % seq_len
    first = t_in_seq == 0
    last = t_in_seq == seq_len - 1

    def conv(u, f):
        up = pltpu.roll(u, 1, 0)[FFN_HALO:FFN_HALO + tm]
        un = pltpu.roll(u, ext - 1, 0)[FFN_HALO:FFN_HALO + tm]
        return (jnp.where(first, 0.0, up) * f[0:1, :] + u[FFN_HALO:FFN_HALO + tm] * f[1:2, :]
                + jnp.where(last, 0.0, un) * f[2:3, :])

    u_val = conv(jnp.dot(hh, wv_ref[...], preferred_element_type=F32), fv_ref[...])
    u_gate = conv(jnp.dot(hh, wg_ref[...], preferred_element_type=F32), fg_ref[...])
    acc_scr[...] += _bdot(_silu(u_gate) * u_val, wd_ref[...])

    @pl.when(n == pl.num_programs(1) - 1)
    def _():
        o_ref[...] = x_ref[...] + mod_ref[5, pl.ds(r, 1), :] * acc_scr[...]


def _ffn(x, mod_l, n2, w_up, f_conv, w_dn, seq_len, latent, tm=1024, tn=256):
    rows = x.shape[0]
    nt = D_FF // tn
    hb = tm // FFN_HALO
    last_hb = rows // FFN_HALO - 1
    kern = functools.partial(_ffn_kernel, tm=tm, seq_len=seq_len, tiles_per_seq=max(seq_len // tm, 1), latent=latent)
    return pl.pallas_call(
        kern,
        grid=(rows // tm, nt),
        in_specs=[pl.BlockSpec((tm, D_MODEL), lambda m, n: (m, 0)),
                  pl.BlockSpec((FFN_HALO, D_MODEL), lambda m, n: (jnp.maximum(m * hb - 1, 0), 0)),
                  pl.BlockSpec((FFN_HALO, D_MODEL), lambda m, n: (jnp.minimum((m + 1) * hb, last_hb), 0)),
                  pl.BlockSpec((6, 8, D_MODEL), lambda m, n: (0, 0, 0)),
                  pl.BlockSpec((1, D_MODEL), lambda m, n: (0, 0)),
                  pl.BlockSpec((D_MODEL, tn), lambda m, n: (0, n)),
                  pl.BlockSpec((D_MODEL, tn), lambda m, n: (0, nt + n)),
                  pl.BlockSpec((3, tn), lambda m, n: (0, n)),
                  pl.BlockSpec((3, tn), lambda m, n: (0, nt + n)),
                  pl.BlockSpec((tn, D_MODEL), lambda m, n: (n, 0))],
        out_specs=pl.BlockSpec((tm, D_MODEL), lambda m, n: (m, 0)),
        out_shape=jax.ShapeDtypeStruct((rows, D_MODEL), F32),
        scratch_shapes=[pltpu.VMEM((tm + 2 * FFN_HALO, D_MODEL), BF16), pltpu.VMEM((tm, D_MODEL), F32)],
        compiler_params=_params("arbitrary", "arbitrary"),
        name="ffn",
    )(x, x, x, mod_l, n2, w_up, w_up, f_conv, f_conv, w_dn)


def _permute_w_in(w_in):
    o = np.cumsum([0, 512, 512, 512, 512, 512, 512, 512, 8, 8, 256, 256, 512, 512, 32, 1024, 1024, 1024])
    piece = lambda i: w_in[:, :, o[i]:o[i + 1]]
    gq = piece(9).reshape(DEPTH, D_MODEL, GLA_HEADS, GLA_KEY_DIM)
    gk = piece(10).reshape(DEPTH, D_MODEL, GLA_HEADS, GLA_KEY_DIM)
    gqk = jnp.concatenate([gq, gk], axis=-1).reshape(DEPTH, D_MODEL, 2 * GLA_HEADS * GLA_KEY_DIM)
    small = jnp.concatenate([piece(7), piece(8), piece(13),
                             jnp.zeros((DEPTH, D_MODEL, PROJ_PAD - OFF_SMALL - 48), w_in.dtype)], axis=-1)
    cols = [piece(i) for i in range(7)] + [gqk, piece(11), piece(12), piece(14), piece(15), piece(16), small]
    return jnp.concatenate(cols, axis=-1).astype(BF16)


def _gla_gate_weights(gla_w, gla_b):
    r, dk = GLA_GATE_RANK, GLA_KEY_DIM
    w = gla_w.reshape(DEPTH, 2, r, GLA_HEADS, dk)
    wg = jnp.zeros((DEPTH, GLA_HEADS, 128, 2, dk), F32)
    for d in range(2):
        wg = wg.at[:, :, SMALL_LR + d * r:SMALL_LR + (d + 1) * r, d, :].set(jnp.transpose(w[:, d], (0, 2, 1, 3)))
    bg = jnp.transpose(gla_b.reshape(DEPTH, 2, GLA_HEADS, dk), (0, 2, 1, 3)).reshape(DEPTH, 1, GLA_HEADS * 2 * dk)
    wg = jnp.transpose(wg, (0, 2, 1, 3, 4)).reshape(DEPTH, 128, GLA_HEADS * 2 * dk)
    return wg.astype(BF16), bg.astype(F32)


def _lane_row(values, offset):
    n = values.shape[-1]
    return jnp.zeros((DEPTH, 1, 128), F32).at[:, 0, offset:offset + n].set(values.astype(F32))


def kernel(x_prompt, x_sample, cache_k, cache_v, state_dn, state_gla, c, c_ctx, w_ada, b_ada, norm1, w_in,
           na_q_norm, na_k_norm, na_rpb, dn_conv, dn_a_log, dn_dt_bias, dn_out_norm, gla_w_gate, gla_b_gate,
           gla_out_norm, w_branch, w_out, norm2, w_up, ffn_conv, w_down):
    batch, seq, _ = x_prompt.shape
    dbatch, dseq, _ = x_sample.shape
    past = cache_k.shape[2]

    cvec8 = jnp.concatenate([c_ctx[None, :], c, jnp.zeros((8 - 1 - dbatch, D_MODEL), F32)], axis=0)
    mod = _adaln(cvec8, w_ada, b_ada)

    w_in_p = _permute_w_in(w_in)
    w_br = w_branch.astype(BF16)
    w_o = w_out.astype(BF16)
    w_up_b = w_up.astype(BF16)
    w_dn_b = w_down.astype(BF16)
    bias = _na_bias_table(na_rpb)
    wg, bg = _gla_gate_weights(gla_w_gate, gla_b_gate)
    alog128 = _lane_row(dn_a_log.reshape(DEPTH, 2 * DN_HEADS), SMALL_DECAY)
    dtb128 = _lane_row(dn_dt_bias.reshape(DEPTH, 2 * DN_HEADS), SMALL_DECAY)
    ck = cache_k.reshape(dbatch, DEPTH, past, NA_HEADS * NA_HEAD_DIM)
    cv = cache_v.reshape(dbatch, DEPTH, past, NA_HEADS * NA_HEAD_DIM)
    s0_gla_t = jnp.swapaxes(state_gla, -1, -2)

    y_p = x_prompt.reshape(batch * seq, D_MODEL)
    y_s = x_sample.reshape(dbatch * dseq, D_MODEL)
    new_k, new_v, new_dn, new_gla = [], [], [], []
    for l in range(DEPTH):
        qg, kg = na_q_norm[l][None, :], na_k_norm[l][None, :]
        n1, n2 = norm1[l][None, :], norm2[l][None, :]
        dn_g, gla_g = dn_out_norm[l][None, :], gla_out_norm[l][None, :]
        for latent in (False, True):
            x = y_s if latent else y_p
            b_, t_ = (dbatch, dseq) if latent else (batch, seq)
            p = _inproj(x, mod[l], n1, w_in_p[l], t_, latent)
            if latent:
                o_a = _lat_attn(p, ck, cv, bias, qg, kg, l, b_, t_)
            else:
                o_a, k_l, v_l = _ctx_attn(p, qg, kg, b_, t_)
            o_b, dn_l = _deltanet(p, dn_conv[l], alog128[l], dtb128[l], dn_g,
                                  state_dn if latent else None, l, b_, t_, not latent, 2 if latent else DN_HEADS)
            o_c, gla_l = _gla(p, wg[l], bg[l], gla_g, s0_gla_t if latent else None, l, b_, t_, not latent,
                              2 if latent else GLA_HEADS)
            x = _merge(x, o_a, o_b, o_c, p, mod[l], w_br[l], w_o[l], t_, latent)
            x = _ffn(x, mod[l], n2, w_up_b[l], ffn_conv[l], w_dn_b[l], t_, latent)
            if latent:
                y_s = x
            else:
                y_p = x
                new_k.append(k_l.reshape(batch, seq, NA_HEADS, NA_HEAD_DIM))
                new_v.append(v_l.reshape(batch, seq, NA_HEADS, NA_HEAD_DIM))
                new_dn.append(dn_l)
                new_gla.append(jnp.swapaxes(gla_l, -1, -2))
    return (y_p.reshape(batch, seq, D_MODEL), y_s.reshape(dbatch, dseq, D_MODEL),
            jnp.stack(new_k, axis=1), jnp.stack(new_v, axis=1),
            jnp.stack(new_dn, axis=1), jnp.stack(new_gla, axis=1))
```

```python
import functools

import numpy as np
import jax
import jax.numpy as jnp
from jax import lax
from jax.experimental import pallas as pl
from jax.experimental.pallas import tpu as pltpu

F32 = jnp.float32
BF16 = jnp.bfloat16

D_MODEL = 1024
DEPTH = 2
GRID_W = 64
EPS = 1e-6
NA_HEADS = 8
NA_HEAD_DIM = 64
NA_WIN_H = 8
NA_WIN_W = 16
DN_HEADS = 4
DN_HEAD_DIM = 128
DN_CHUNK = 64
GLA_HEADS = 4
GLA_KEY_DIM = 64
GLA_VAL_DIM = 128
GLA_GATE_RANK = 16
GLA_TAU = 16.0
GLA_CHUNK = 32
D_FF = 2816

OFF_NA_Q, OFF_NA_K, OFF_NA_V = 0, 512, 1024
OFF_DN_Q, OFF_DN_K, OFF_DN_V, OFF_DN_GATE = 1536, 2048, 2560, 3072
OFF_GLA_QK, OFF_GLA_V, OFF_GLA_GATE = 3584, 4096, 4608
OFF_M = 5120
OFF_SMALL = 8192
PROJ_PAD = 8320
SMALL_BETA, SMALL_DECAY, SMALL_LR = 0, 8, 16

GROUP = 256
NEG = -1e30
VMEM_LIMIT = 56 * 1024 * 1024

NA_QROWS = 4
NA_KROWS = 12


def _bdot(a, b):
    return jnp.dot(a.astype(BF16), b.astype(BF16), preferred_element_type=F32)


def _bdot_nt(a, b):
    return lax.dot_general(a.astype(BF16), b.astype(BF16), (((1,), (1,)), ((), ())), preferred_element_type=F32)


def _bdot_tn(a, b):
    return lax.dot_general(a.astype(BF16), b.astype(BF16), (((0,), (0,)), ((), ())), preferred_element_type=F32)


def _split3(x):
    hi = x.astype(BF16)
    r = x - hi.astype(F32)
    mid = r.astype(BF16)
    lo = (r - mid.astype(F32)).astype(BF16)
    return hi, mid, lo


def _dot_mask_f32(t, x):
    hi, mid, lo = _split3(x)
    d = lambda p: jnp.dot(t, p, preferred_element_type=F32)
    return d(hi) + d(mid) + d(lo)


def _dot3(a, b):
    ah = a.astype(BF16)
    al = (a - ah.astype(F32)).astype(BF16)
    bh = b.astype(BF16)
    bl = (b - bh.astype(F32)).astype(BF16)
    d = lambda p, q: jnp.dot(p, q, preferred_element_type=F32)
    return d(ah, bh) + d(ah, bl) + d(al, bh)


def _sigmoid(x):
    return 1.0 / (1.0 + jnp.exp(-x))


def _silu(x):
    return x * _sigmoid(x)


def _softplus(x):
    return jnp.maximum(x, 0.0) + jnp.log(1.0 + jnp.exp(-jnp.abs(x)))


def _params(*sem):
    return pltpu.CompilerParams(dimension_semantics=sem, vmem_limit_bytes=VMEM_LIMIT)


def _adaln_kernel(c_ref, w_ref, b_ref, o_ref):
    o_ref[...] = _bdot(_silu(c_ref[...]), w_ref[...]) + b_ref[...]


def _adaln(cvec8, w_ada, b_ada):
    return pl.pallas_call(
        _adaln_kernel,
        grid=(DEPTH, 6),
        in_specs=[pl.BlockSpec((8, D_MODEL), lambda l, j: (0, 0)),
                  pl.BlockSpec((None, D_MODEL, D_MODEL), lambda l, j: (l, 0, j)),
                  pl.BlockSpec((None, None, 1, D_MODEL), lambda l, j: (l, j, 0, 0))],
        out_specs=pl.BlockSpec((None, None, 8, D_MODEL), lambda l, j: (l, j, 0, 0)),
        out_shape=jax.ShapeDtypeStruct((DEPTH, 6, 8, D_MODEL), F32),
        compiler_params=_params("arbitrary", "arbitrary"),
        name="adaln",
    )(cvec8, w_ada, b_ada.reshape(DEPTH, 6, 1, D_MODEL))


def _mod_row(tile, tiles_per_seq, latent):
    return (1 + tile // tiles_per_seq) if latent else 0


def _norm_mod(x, gain, shift, scale):
    y = x * lax.rsqrt(jnp.mean(x * x, axis=-1, keepdims=True) + EPS) * gain
    return y * (1.0 + scale) + shift


def _inproj_kernel(x_ref, mod_ref, n1_ref, w_ref, o_ref, *, tiles_per_seq, latent, tn):
    r = _mod_row(pl.program_id(0), tiles_per_seq, latent)
    h = _norm_mod(x_ref[...], n1_ref[...], mod_ref[0, pl.ds(r, 1), :], mod_ref[1, pl.ds(r, 1), :]).astype(BF16)
    for n in range(PROJ_PAD // tn):
        o_ref[:, n * tn:(n + 1) * tn] = jnp.dot(h, w_ref[:, n * tn:(n + 1) * tn], preferred_element_type=F32)


def _inproj(x, mod_l, n1, w_in_l, seq_len, latent, tm=256, tn=1664):
    rows = x.shape[0]
    kern = functools.partial(_inproj_kernel, tiles_per_seq=max(seq_len // tm, 1), latent=latent, tn=tn)
    return pl.pallas_call(
        kern,
        grid=(rows // tm,),
        in_specs=[pl.BlockSpec((tm, D_MODEL), lambda m: (m, 0)),
                  pl.BlockSpec((6, 8, D_MODEL), lambda m: (0, 0, 0)),
                  pl.BlockSpec((1, D_MODEL), lambda m: (0, 0)),
                  pl.BlockSpec((D_MODEL, PROJ_PAD), lambda m: (0, 0))],
        out_specs=pl.BlockSpec((tm, PROJ_PAD), lambda m: (m, 0)),
        out_shape=jax.ShapeDtypeStruct((rows, PROJ_PAD), F32),
        compiler_params=_params("arbitrary"),
        name="inproj",
    )(x, mod_l, n1, w_in_l)


def _head_rms(x, gain):
    return x * lax.rsqrt(jnp.mean(x * x, axis=-1, keepdims=True) + EPS) * gain


def _head_slices():
    return [slice(h * NA_HEAD_DIM, (h + 1) * NA_HEAD_DIM) for h in range(NA_HEADS)]


def _ctx_attn_kernel(*refs, has_prev, layer):
    q_ref, k_ref, v_ref, qg_ref, kg_ref = refs[:5]
    o_ref, ka_ref, va_ref = refs[-3:]
    ka_ref = _own_slab(ka_ref, layer, not has_prev)
    va_ref = _own_slab(va_ref, layer, not has_prev)
    hs = _head_slices()
    qn = [_head_rms(q_ref[:, sl], qg_ref[...]) * (NA_HEAD_DIM ** -0.5) for sl in hs]
    kn = [_head_rms(k_ref[:, sl], kg_ref[...]) for sl in hs]
    for sl, k in zip(hs, kn):
        ka_ref[:, sl] = k
    s = [_bdot_nt(q, k) for q, k in zip(qn, kn)]
    p = [jnp.exp(x - jnp.max(x, axis=-1, keepdims=True)) for x in s]
    pv = [_bdot(x, v_ref[:, sl]) for x, sl in zip(p, hs)]
    for sl, x, y in zip(hs, p, pv):
        o_ref[:, sl] = y / jnp.sum(x, axis=-1, keepdims=True)
    va_ref[...] = v_ref[...]


def _layer_slab(shape_tail, index_tail, layer, first):
    if first:
        return pl.BlockSpec((None, DEPTH) + shape_tail, lambda b, *g: (b, 0) + index_tail(*g))
    return pl.BlockSpec((None, None) + shape_tail, lambda b, *g: (b, layer) + index_tail(*g))


def _own_slab(ref, layer, first):
    if not first:
        return ref
    for other in range(DEPTH):
        if other != layer:
            ref[other] = jnp.zeros(ref.shape[1:], ref.dtype)
    return ref.at[layer]


def _ctx_attn(p_ctx, qg, kg, prev_kv, layer, batch, seq):
    rows = batch * seq
    blk = lambda j: pl.BlockSpec((seq, 512), lambda b: (b, j))
    cache = jax.ShapeDtypeStruct((batch, DEPTH, seq, 512), F32)
    cache_spec = _layer_slab((seq, 512), lambda: (0, 0), layer, prev_kv is None)
    in_specs = [blk(OFF_NA_Q // 512), blk(OFF_NA_K // 512), blk(OFF_NA_V // 512),
                pl.BlockSpec((1, NA_HEAD_DIM), lambda b: (0, 0)),
                pl.BlockSpec((1, NA_HEAD_DIM), lambda b: (0, 0))]
    args = [p_ctx, p_ctx, p_ctx, qg, kg]
    aliases = {}
    if prev_kv is not None:
        in_specs += [pl.BlockSpec(memory_space=pl.ANY)] * 2
        args += list(prev_kv)
        aliases = {5: 1, 6: 2}
    return pl.pallas_call(
        functools.partial(_ctx_attn_kernel, has_prev=prev_kv is not None, layer=layer),
        grid=(batch,),
        in_specs=in_specs,
        out_specs=[blk(0), cache_spec, cache_spec],
        out_shape=[jax.ShapeDtypeStruct((rows, 512), F32), cache, cache],
        input_output_aliases=aliases,
        compiler_params=_params("arbitrary"),
        name="ctx_attn",
    )(*args)


NA_NDR = 2 * NA_WIN_H - 1


def _na_bias_tiles(rpb):
    qc = np.arange(GRID_W)
    cs = np.clip(qc - NA_WIN_W // 2, 0, GRID_W - NA_WIN_W)
    col_ok = (qc[None, :] >= cs[:, None]) & (qc[None, :] < cs[:, None] + NA_WIN_W)
    nd = 2 * NA_WIN_W - 1
    lo = GRID_W - NA_WIN_W
    w = jnp.full(rpb.shape[:3] + (2 * GRID_W,), NEG, F32)
    w = w.at[..., lo:lo + nd].set(rpb.astype(F32))
    t = jnp.tile(w, (1, 1, 1, GRID_W))[..., :GRID_W * (2 * GRID_W - 1)]
    t = t.reshape(rpb.shape[:3] + (GRID_W, 2 * GRID_W - 1))[..., GRID_W - 1:]
    t = jnp.where(col_ok, t, NEG)
    return jnp.concatenate([t, jnp.full(t.shape[:2] + (1, GRID_W, GRID_W), NEG, F32)], axis=2)


def _lat_attn_kernel(q_ref, k_ref, v_ref, ck_ref, cv_ref, tile_ref, qg_ref, kg_ref, o_ref, kn_scr, bias_scr, *, rows):
    m = pl.program_id(1)
    nblk = rows // NA_QROWS
    hs = _head_slices()

    @pl.when(m == 0)
    def _():
        for sl in hs:
            kn_scr[:, sl] = _head_rms(k_ref[:, sl], kg_ref[...]).astype(BF16)

    krow0 = jnp.clip(NA_QROWS * m - NA_WIN_H // 2, 0, rows - NA_KROWS)

    @pl.when((m == 0) | (m == 1) | (m == nblk - 1))
    def _():
        for i in range(NA_QROWS):
            r = NA_QROWS * m + i
            rs = jnp.clip(r - NA_WIN_H // 2, 0, rows - NA_WIN_H)
            for j in range(NA_KROWS):
                kr = krow0 + j
                idx = jnp.where((kr >= rs) & (kr < rs + NA_WIN_H), kr - r + NA_WIN_H - 1, NA_NDR)
                for h in range(NA_HEADS):
                    bias_scr[h, i * GRID_W:(i + 1) * GRID_W, j * GRID_W:(j + 1) * GRID_W] = tile_ref[h, idx]

    kstart = pl.multiple_of(krow0 * GRID_W, 256)
    nk = NA_KROWS * GRID_W
    qn = [(_head_rms(q_ref[:, sl], qg_ref[...]) * (NA_HEAD_DIM ** -0.5)).astype(BF16) for sl in hs]
    s_loc = [_bdot_nt(q, kn_scr[pl.ds(kstart, nk), sl]) + bias_scr[h] for h, (q, sl) in enumerate(zip(qn, hs))]
    s_ctx = [_bdot_nt(q, ck_ref[:, sl]) for q, sl in zip(qn, hs)]
    mx = [jnp.maximum(jnp.max(a, axis=-1, keepdims=True), jnp.max(c, axis=-1, keepdims=True))
          for a, c in zip(s_loc, s_ctx)]
    p_loc = [jnp.exp(a - x) for a, x in zip(s_loc, mx)]
    p_ctx = [jnp.exp(c - x) for c, x in zip(s_ctx, mx)]
    o_loc = [_bdot(p, v_ref[pl.ds(kstart, nk), sl]) for p, sl in zip(p_loc, hs)]
    o_ctx = [_bdot(p, cv_ref[:, sl]) for p, sl in zip(p_ctx, hs)]
    for h, sl in enumerate(hs):
        den = jnp.sum(p_loc[h], axis=-1, keepdims=True) + jnp.sum(p_ctx[h], axis=-1, keepdims=True)
        o_ref[:, sl] = (o_loc[h] + o_ctx[h]) / den


def _lat_attn(p_lat, cache_k, cache_v, tiles, qg, kg, layer, batch, seq):
    rows = seq // GRID_W
    nblk = rows // NA_QROWS
    qtok = NA_QROWS * GRID_W
    past = cache_k.shape[2]
    cache_spec = pl.BlockSpec((None, None, past, 512), lambda b, m: (b, layer, 0, 0))
    return pl.pallas_call(
        functools.partial(_lat_attn_kernel, rows=rows),
        grid=(batch, nblk),
        in_specs=[pl.BlockSpec((qtok, 512), lambda b, m: (b * nblk + m, OFF_NA_Q // 512)),
                  pl.BlockSpec((seq, 512), lambda b, m: (b, OFF_NA_K // 512)),
                  pl.BlockSpec((seq, 512), lambda b, m: (b, OFF_NA_V // 512)),
                  cache_spec, cache_spec,
                  pl.BlockSpec((None, NA_HEADS, NA_NDR + 1, GRID_W, GRID_W), lambda b, m: (layer, 0, 0, 0, 0)),
                  pl.BlockSpec((1, NA_HEAD_DIM), lambda b, m: (0, 0)),
                  pl.BlockSpec((1, NA_HEAD_DIM), lambda b, m: (0, 0))],
        out_specs=pl.BlockSpec((qtok, 512), lambda b, m: (b * nblk + m, 0)),
        out_shape=jax.ShapeDtypeStruct((batch * seq, 512), F32),
        scratch_shapes=[pltpu.VMEM((seq, 512), BF16),
                        pltpu.VMEM((NA_HEADS, qtok, NA_KROWS * GRID_W), F32)],
        compiler_params=_params("arbitrary", "arbitrary"),
        name="lat_attn",
    )(p_lat, p_lat, p_lat, cache_k, cache_v, tiles, qg, kg)


def _group_masks(chunk):
    r = lax.broadcasted_iota(jnp.int32, (GROUP, GROUP), 0)
    c = lax.broadcasted_iota(jnp.int32, (GROUP, GROUP), 1)
    same = (r // chunk) == (c // chunk)
    return (same & (c <= r), same & (c >= r), same & (c < r), same & (c > r), same)


def _halving_masks(chunk, lower):
    r = lax.broadcasted_iota(jnp.int32, (GROUP, GROUP), 0)
    c = lax.broadcasted_iota(jnp.int32, (GROUP, GROUP), 1)
    if not lower:
        r, c = c, r
    masks = []
    s = 1
    while s < chunk:
        masks.append(((r // (2 * s)) == (c // (2 * s))) & ((r // s) % 2 == 1) & ((c // s) % 2 == 0))
        s *= 2
    return masks


def _group_eye():
    r = lax.broadcasted_iota(jnp.int32, (GROUP, GROUP), 0)
    c = lax.broadcasted_iota(jnp.int32, (GROUP, GROUP), 1)
    return jnp.where(r == c, 1.0, 0.0)


def _sel_col(a, lane):
    idx = lax.broadcasted_iota(jnp.int32, a.shape, 1)
    return jnp.sum(jnp.where(idx == lane, a, 0.0), axis=1, keepdims=True)


def _sel_row(a, row):
    idx = lax.broadcasted_iota(jnp.int32, a.shape, 0)
    return jnp.sum(jnp.where(idx == row, a, 0.0), axis=0, keepdims=True)


def _conv3_silu(x, w):
    t = x.shape[0]
    row = lax.broadcasted_iota(jnp.int32, x.shape, 0)
    xm = jnp.where(row == 0, 0.0, pltpu.roll(x, 1, 0))
    xp = jnp.where(row == t - 1, 0.0, pltpu.roll(x, t - 1, 0))
    return _silu(xm * w[0:1, :] + x * w[1:2, :] + xp * w[2:3, :])


def _deltanet_kernel(*refs, hp, has_s0, emit_state, has_prev, layer):
    (q_ref, k_ref, v_ref, gate_ref, sm_ref, cq_ref, ck_ref, cv_ref, alog_ref, dtb_ref, og_ref), refs = refs[:11], refs[11:]
    if has_s0:
        s0_ref, refs = refs[0], refs[1:]
    if has_prev:
        refs = refs[1:]
    o_ref, refs = refs[0], refs[1:]
    if emit_state:
        sfin_ref, refs = refs[0], refs[1:]
    qn_scr, kn_scr, vc_scr, of_scr, ob_scr, s_scr = refs

    head0 = pl.program_id(1) * hp
    seq = q_ref.shape[0]
    hd = DN_HEAD_DIM
    ngroups = seq // GROUP
    nchunk = GROUP // DN_CHUNK

    q = _conv3_silu(q_ref[...], cq_ref[...])
    k = _conv3_silu(k_ref[...], ck_ref[...])
    vc_scr[...] = _conv3_silu(v_ref[...], cv_ref[...])
    for j in range(hp):
        sl = slice(j * hd, (j + 1) * hd)
        qj, kj = q[:, sl], k[:, sl]
        qn_scr[:, sl] = qj * lax.rsqrt(jnp.sum(qj * qj, axis=-1, keepdims=True) + EPS) * (hd ** -0.5)
        kn_scr[:, sl] = kj * lax.rsqrt(jnp.sum(kj * kj, axis=-1, keepdims=True) + EPS)

    if has_s0:
        s_scr[...] = s0_ref[...]
    else:
        s_scr[...] = jnp.zeros_like(s_scr)

    m_fi, m_bi, m_fs, m_bs, m_same = _group_masks(DN_CHUNK)
    incl = (m_fi, m_bi)
    strict = (m_fs, m_bs)
    tri = (m_fi.astype(BF16), m_bi.astype(BF16))
    ones_bd = m_same.astype(BF16)
    halving = (_halving_masks(DN_CHUNK, True), _halving_masks(DN_CHUNK, False))

    eye = _group_eye()
    chains = [(d, j) for d in range(2) for j in range(hp)]
    nc = len(chains)

    def group_step(t, carry):
        shared = []
        for d in range(2):
            gi = t if d == 0 else ngroups - 1 - t
            r0 = pl.multiple_of(gi * GROUP, GROUP)
            sm = sm_ref[pl.ds(r0, GROUP), :]
            g128 = -jnp.exp(alog_ref[...]) * _softplus(sm + dtb_ref[...])
            gcum = _dot_mask_f32(tri[d], g128)
            gtot = _dot_mask_f32(ones_bd, g128)
            shared.append((r0, gcum, gtot, gcum.T, _sigmoid(sm)))

        qg, kg, vg, beta, gc_col, gt_col, dec = [], [], [], [], [], [], []
        for d, j in chains:
            r0, gcum, gtot, gcum_t, beta128 = shared[d]
            sl = slice(j * hd, (j + 1) * hd)
            lane_g = SMALL_DECAY + DN_HEADS * d + head0 + j
            gc_col.append(_sel_col(gcum, lane_g))
            gt_col.append(_sel_col(gtot, lane_g))
            beta.append(_sel_col(beta128, SMALL_BETA + DN_HEADS * d + head0 + j))
            dec.append(jnp.exp(jnp.where(incl[d], gc_col[-1] - _sel_row(gcum_t, lane_g), NEG)))
            qg.append(qn_scr[pl.ds(r0, GROUP), sl])
            kg.append(kn_scr[pl.ds(r0, GROUP), sl])
            vg.append(vc_scr[pl.ds(r0, GROUP), sl])
        rng = range(nc)
        kk = [_bdot_nt(kg[i], kg[i]) for i in rng]
        qk = [_bdot_nt(qg[i], kg[i]) for i in rng]
        low = [jnp.where(strict[chains[i][0]], beta[i] * kk[i] * dec[i], 0.0) for i in rng]
        attn = [qk[i] * dec[i] for i in rng]
        e_gc = [jnp.exp(gc_col[i]) for i in rng]
        x = [jnp.concatenate([vg[i] * beta[i], kg[i] * (beta[i] * e_gc[i])], axis=1) for i in rng]
        inv = [eye - jnp.where(halving[chains[i][0]][0], low[i], 0.0) for i in rng]
        for lvl in range(1, len(halving[0])):
            tmp = [_bdot(inv[i], jnp.where(halving[chains[i][0]][lvl], low[i], 0.0)) for i in rng]
            upd = [_bdot(tmp[i], inv[i]) for i in rng]
            inv = [inv[i] - upd[i] for i in rng]
        x = [_bdot(inv[i], x[i]) for i in rng]
        qd = [qg[i] * e_gc[i] for i in rng]
        kd = [kg[i] * jnp.exp(gt_col[i] - gc_col[i]) for i in rng]
        s = [s_scr[d, j] for d, j in chains]
        o_parts = [[None] * nchunk for _ in rng]
        for cc in range(nchunk):
            rs = []
            for d, _ in chains:
                c = cc if d == 0 else nchunk - 1 - cc
                rs.append(slice(c * DN_CHUNK, (c + 1) * DN_CHUNK))
            ws = [_bdot(x[i][rs[i], hd:], s[i]) for i in rng]
            qs = [_bdot(qd[i][rs[i]], s[i]) for i in rng]
            v_new = [x[i][rs[i], :hd] - ws[i] for i in rng]
            av = [_bdot(attn[i][rs[i], rs[i]], v_new[i]) for i in rng]
            kv = [_bdot_tn(kd[i][rs[i]], v_new[i]) for i in rng]
            for i in rng:
                o_parts[i][rs[i].start // DN_CHUNK] = qs[i] + av[i]
                s[i] = s[i] * jnp.exp(gt_col[i][rs[i].start:rs[i].start + 1, :]) + kv[i]
        for i, (d, j) in enumerate(chains):
            s_scr[d, j] = s[i]
            o_scr = of_scr if d == 0 else ob_scr
            o_scr[pl.ds(shared[d][0], GROUP), j * hd:(j + 1) * hd] = jnp.concatenate(o_parts[i], axis=0)
        return carry

    if ngroups == 1:
        group_step(0, 0)
    else:
        lax.fori_loop(0, ngroups, group_step, 0)

    for j in range(hp):
        sl = slice(j * hd, (j + 1) * hd)
        o = of_scr[:, sl] + ob_scr[:, sl]
        o = o * lax.rsqrt(jnp.mean(o * o, axis=-1, keepdims=True) + EPS) * og_ref[...]
        o_ref[:, sl] = o * _silu(gate_ref[:, sl])
    if emit_state:
        _own_slab(sfin_ref, layer, not has_prev)[...] = s_scr[...]


def _deltanet(p, conv_w, alog128, dtb128, out_g, s0, prev_state, layer, batch, seq, emit_state, hp):
    rows = batch * seq
    hd = DN_HEAD_DIM
    w = hp * hd
    col = lambda off: pl.BlockSpec((seq, w), lambda b, h: (b, off // w + h))
    cw = lambda part: pl.BlockSpec((3, w), lambda b, h: (0, part * (DN_HEADS // hp) + h))
    row128 = pl.BlockSpec((1, 128), lambda b, h: (0, 0))
    in_specs = [col(OFF_DN_Q), col(OFF_DN_K), col(OFF_DN_V), col(OFF_DN_GATE),
                pl.BlockSpec((seq, 128), lambda b, h: (b, OFF_SMALL // 128)),
                cw(0), cw(1), cw(2), row128, row128, row128]
    args = [p, p, p, p, p, conv_w, conv_w, conv_w, alog128, dtb128, out_g]
    if s0 is not None:
        in_specs.append(pl.BlockSpec((None, None, 2, hp, hd, hd), lambda b, h: (b, layer, 0, h, 0, 0)))
        args.append(s0)
    out_specs = [pl.BlockSpec((seq, w), lambda b, h: (b, h))]
    out_shape = [jax.ShapeDtypeStruct((rows, DN_HEADS * hd), F32)]
    aliases = {}
    if emit_state:
        out_specs.append(_layer_slab((2, hp, hd, hd), lambda h: (0, h, 0, 0), layer, prev_state is None))
        out_shape.append(jax.ShapeDtypeStruct((batch, DEPTH, 2, DN_HEADS, hd, hd), F32))
        if prev_state is not None:
            in_specs.append(pl.BlockSpec(memory_space=pl.ANY))
            args.append(prev_state)
            aliases = {len(args) - 1: 1}
    res = pl.pallas_call(
        functools.partial(_deltanet_kernel, hp=hp, has_s0=s0 is not None, emit_state=emit_state,
                          has_prev=prev_state is not None, layer=layer),
        grid=(batch, DN_HEADS // hp),
        in_specs=in_specs,
        out_specs=out_specs,
        out_shape=out_shape,
        input_output_aliases=aliases,
        scratch_shapes=[pltpu.VMEM((seq, w), F32)] * 5 + [pltpu.VMEM((2, hp, hd, hd), F32)],
        compiler_params=_params("arbitrary", "arbitrary"),
        name="deltanet",
    )(*args)
    return res if emit_state else (res[0], None)


def _transpose_f32(x):
    n = x.shape[1]
    r = lax.broadcasted_iota(jnp.int32, (n, n), 0)
    c = lax.broadcasted_iota(jnp.int32, (n, n), 1)
    eye = jnp.where(r == c, 1.0, 0.0).astype(BF16)
    d = lambda p: lax.dot_general(eye, p, (((1,), (1,)), ((), ())), preferred_element_type=F32)
    hi, mid, lo = _split3(x)
    return d(hi) + d(mid) + d(lo)


def _gla_kernel(*refs, hp, has_s0, emit_state, has_prev, layer):
    (qk_ref, v_ref, gate_ref, sm_ref, wg_ref, bg_ref, og_ref), refs = refs[:7], refs[7:]
    if has_s0:
        s0_ref, refs = refs[0], refs[1:]
    if has_prev:
        refs = refs[1:]
    o_ref, refs = refs[0], refs[1:]
    if emit_state:
        sfin_ref, refs = refs[0], refs[1:]
    of_scr, ob_scr, s_scr = refs

    seq = qk_ref.shape[0]
    ngroups = seq // GROUP
    nchunk = GROUP // GLA_CHUNK
    dk = GLA_KEY_DIM

    for d in range(2):
        for j in range(hp):
            if has_s0:
                s_scr[d, j] = _transpose_f32(s0_ref[d, j])
            else:
                s_scr[d, j] = jnp.zeros(s_scr.shape[2:], F32)

    m_fi, m_bi, _, _, m_same = _group_masks(GLA_CHUNK)
    incl = (m_fi, m_bi)
    tri = (m_fi.astype(BF16), m_bi.astype(BF16))
    ones_bd = m_same.astype(BF16)

    chains = [(d, j) for d in range(2) for j in range(hp)]
    rng = range(len(chains))

    def group_step(t, carry):
        shared = []
        for d in range(2):
            gi = t if d == 0 else ngroups - 1 - t
            r0 = pl.multiple_of(gi * GROUP, GROUP)
            gl = _bdot(sm_ref[pl.ds(r0, GROUP), :], wg_ref[...]) + bg_ref[...]
            log_a = -_softplus(-gl) / GLA_TAU
            b_all = _dot_mask_f32(tri[d], log_a)
            btot_all = _dot_mask_f32(ones_bd, log_a)
            shared.append((r0, b_all, btot_all))
        b, btot, kg, vg, qe = [], [], [], [], []
        for d, j in chains:
            r0, b_all, btot_all = shared[d]
            sl = slice(j * 128, (j + 1) * 128)
            lanes = slice(j * 128 + d * dk, j * 128 + (d + 1) * dk)
            b.append(b_all[:, lanes])
            btot.append(btot_all[:, lanes])
            qk = qk_ref[pl.ds(r0, GROUP), sl]
            kg.append(qk[:, dk:])
            vg.append(v_ref[pl.ds(r0, GROUP), sl])
            qe.append(qk[:, :dk] * (dk ** -0.5) * jnp.exp(b[-1]))
        attn = [jnp.where(incl[chains[i][0]], _bdot_nt(qe[i], kg[i] * jnp.exp(-b[i])), 0.0) for i in rng]
        o_in = [_bdot(attn[i], vg[i]) for i in rng]
        kd = [kg[i] * jnp.exp(btot[i] - b[i]) for i in rng]
        st = [s_scr[d, j] for d, j in chains]
        o_parts = [[None] * nchunk for _ in rng]
        for cc in range(nchunk):
            rs = []
            for d, _ in chains:
                c = cc if d == 0 else nchunk - 1 - cc
                rs.append(slice(c * GLA_CHUNK, (c + 1) * GLA_CHUNK))
            qs = [_bdot_nt(qe[i][rs[i]], st[i]) for i in rng]
            vk = [_bdot_tn(vg[i][rs[i]], kd[i][rs[i]]) for i in rng]
            for i in rng:
                o_parts[i][rs[i].start // GLA_CHUNK] = o_in[i][rs[i]] + qs[i]
                st[i] = st[i] * jnp.exp(btot[i][rs[i].start:rs[i].start + 1, :]) + vk[i]
        for i, (d, j) in enumerate(chains):
            s_scr[d, j] = st[i]
            o_scr = of_scr if d == 0 else ob_scr
            o_scr[pl.ds(shared[d][0], GROUP), j * 128:(j + 1) * 128] = jnp.concatenate(o_parts[i], axis=0)
        return carry

    if ngroups == 1:
        group_step(0, 0)
    else:
        lax.fori_loop(0, ngroups, group_step, 0)

    for j in range(hp):
        sl = slice(j * 128, (j + 1) * 128)
        o = of_scr[:, sl] + ob_scr[:, sl]
        o = o * lax.rsqrt(jnp.mean(o * o, axis=-1, keepdims=True) + EPS) * og_ref[...]
        o_ref[:, sl] = o * _silu(gate_ref[:, sl])
    if emit_state:
        sfin = _own_slab(sfin_ref, layer, not has_prev)
        for d in range(2):
            for j in range(hp):
                sfin[d, j] = _transpose_f32(s_scr[d, j])


def _gla(p, wg, bg, out_g, s0, prev_state, layer, batch, seq, emit_state, hp):
    rows = batch * seq
    dv, dk = GLA_VAL_DIM, GLA_KEY_DIM
    w = hp * 128
    col = lambda off: pl.BlockSpec((seq, w), lambda b, h: (b, off // w + h))
    in_specs = [col(OFF_GLA_QK), col(OFF_GLA_V), col(OFF_GLA_GATE),
                pl.BlockSpec((seq, 128), lambda b, h: (b, OFF_SMALL // 128)),
                pl.BlockSpec((128, w), lambda b, h: (0, h)),
                pl.BlockSpec((1, w), lambda b, h: (0, h)),
                pl.BlockSpec((1, 128), lambda b, h: (0, 0))]
    args = [p, p, p, p, wg, bg, out_g]
    if s0 is not None:
        in_specs.append(pl.BlockSpec((None, None, 2, hp, dk, dv), lambda b, h: (b, layer, 0, h, 0, 0)))
        args.append(s0)
    out_specs = [pl.BlockSpec((seq, w), lambda b, h: (b, h))]
    out_shape = [jax.ShapeDtypeStruct((rows, GLA_HEADS * dv), F32)]
    aliases = {}
    if emit_state:
        out_specs.append(_layer_slab((2, hp, dk, dv), lambda h: (0, h, 0, 0), layer, prev_state is None))
        out_shape.append(jax.ShapeDtypeStruct((batch, DEPTH, 2, GLA_HEADS, dk, dv), F32))
        if prev_state is not None:
            in_specs.append(pl.BlockSpec(memory_space=pl.ANY))
            args.append(prev_state)
            aliases = {len(args) - 1: 1}
    res = pl.pallas_call(
        functools.partial(_gla_kernel, hp=hp, has_s0=s0 is not None, emit_state=emit_state,
                          has_prev=prev_state is not None, layer=layer),
        grid=(batch, GLA_HEADS // hp),
        in_specs=in_specs,
        out_specs=out_specs,
        out_shape=out_shape,
        input_output_aliases=aliases,
        scratch_shapes=[pltpu.VMEM((seq, w), F32)] * 2 + [pltpu.VMEM((2, hp, dv, dk), F32)],
        compiler_params=_params("arbitrary", "arbitrary"),
        name="gla",
    )(*args)
    return res if emit_state else (res[0], None)


def _merge_kernel(x_ref, oa_ref, ob_ref, oc_ref, ma_ref, mb_ref, mc_ref, mod_ref, wbr_ref, wo_ref, o_ref,
                  *, tiles_per_seq, latent):
    r = _mod_row(pl.program_id(0), tiles_per_seq, latent)
    merged = (_sigmoid(ma_ref[...]) * _bdot(oa_ref[...], wbr_ref[0])
              + _sigmoid(mb_ref[...]) * _bdot(ob_ref[...], wbr_ref[1])
              + _sigmoid(mc_ref[...]) * _bdot(oc_ref[...], wbr_ref[2]))
    o_ref[...] = x_ref[...] + mod_ref[2, pl.ds(r, 1), :] * _bdot(merged, wo_ref[...])


def _merge(x, o_a, o_b, o_c, p, mod_l, w_br, w_o, seq_len, latent, tm=512):
    rows = x.shape[0]
    row_blk = lambda w: pl.BlockSpec((tm, w), lambda m: (m, 0))
    gate_blk = lambda j: pl.BlockSpec((tm, D_MODEL), lambda m: (m, OFF_M // D_MODEL + j))
    return pl.pallas_call(
        functools.partial(_merge_kernel, tiles_per_seq=max(seq_len // tm, 1), latent=latent),
        grid=(rows // tm,),
        in_specs=[row_blk(D_MODEL), row_blk(512), row_blk(512), row_blk(512),
                  gate_blk(0), gate_blk(1), gate_blk(2),
                  pl.BlockSpec((6, 8, D_MODEL), lambda m: (0, 0, 0)),
                  pl.BlockSpec((3, 512, D_MODEL), lambda m: (0, 0, 0)),
                  pl.BlockSpec((D_MODEL, D_MODEL), lambda m: (0, 0))],
        out_specs=row_blk(D_MODEL),
        out_shape=jax.ShapeDtypeStruct((rows, D_MODEL), F32),
        compiler_params=_params("arbitrary"),
        name="merge",
    )(x, o_a, o_b, o_c, p, p, p, mod_l, w_br, w_o)


FFN_HALO = 16


def _ffn_kernel(x_ref, xp_ref, xn_ref, mod_ref, n2_ref, wu_ref, fc_ref, wd_ref, o_ref, h_scr, act_scr,
                *, tm, tn, seq_len, tiles_per_seq, latent):
    r = _mod_row(pl.program_id(0), tiles_per_seq, latent)
    shift = mod_ref[3, pl.ds(r, 1), :]
    scale = mod_ref[4, pl.ds(r, 1), :]
    nm = lambda x: _norm_mod(x, n2_ref[...], shift, scale)
    row0 = pl.program_id(0) * tm
    h_scr[0:FFN_HALO, :] = jnp.where(row0 % seq_len == 0, 0.0, nm(xp_ref[...])).astype(BF16)
    h_scr[FFN_HALO:FFN_HALO + tm, :] = nm(x_ref[...]).astype(BF16)
    h_scr[FFN_HALO + tm:, :] = jnp.where((row0 + tm) % seq_len == 0, 0.0, nm(xn_ref[...])).astype(BF16)

    ext = tm + 2 * FFN_HALO
    inner = slice(FFN_HALO, FFN_HALO + tm)
    bounds = list(range(seq_len, tm, seq_len))
    fix = 16
    rid = lax.broadcasted_iota(jnp.int32, (2 * fix, 1), 0)

    def up(n):
        hh = h_scr[...]
        cv = slice(n * tn, (n + 1) * tn)
        cg = slice(D_FF + n * tn, D_FF + (n + 1) * tn)
        return (jnp.dot(hh, wu_ref[:, cv], preferred_element_type=F32),
                jnp.dot(hh, wu_ref[:, cg], preferred_element_type=F32))

    def taps(u):
        return pltpu.roll(u, 1, 0)[inner], u[inner], pltpu.roll(u, ext - 1, 0)[inner]

    def conv(t, f, rows=slice(None), keep_prev=None, keep_next=None):
        prev, mid, nxt = (a[rows] for a in t)
        if keep_prev is not None:
            prev = jnp.where(keep_prev, prev, 0.0)
            nxt = jnp.where(keep_next, nxt, 0.0)
        return prev * f[0:1, :] + mid * f[1:2, :] + nxt * f[2:3, :]

    nt = D_FF // tn
    u_next = up(0)
    for n in range(nt):
        u_val, u_gate = u_next
        if n + 1 < nt:
            u_next = up(n + 1)
        cols = slice(n * tn, (n + 1) * tn)
        f_val = fc_ref[:, cols]
        f_gate = fc_ref[:, D_FF + n * tn:D_FF + (n + 1) * tn]
        t_val, t_gate = taps(u_val), taps(u_gate)
        act_scr[:, cols] = (_silu(conv(t_gate, f_gate)) * conv(t_val, f_val)).astype(BF16)
        for b in bounds:
            rows = slice(b - fix, b + fix)
            kp, kn = rid != fix, rid != fix - 1
            act_scr[rows, cols] = (_silu(conv(t_gate, f_gate, rows, kp, kn))
                                   * conv(t_val, f_val, rows, kp, kn)).astype(BF16)
    down = jnp.dot(act_scr[...], wd_ref[...], preferred_element_type=F32)
    o_ref[...] = x_ref[...] + mod_ref[5, pl.ds(r, 1), :] * down


def _ffn(x, mod_l, n2, w_up, f_conv, w_dn, seq_len, latent, tm=512, tn=256):
    rows = x.shape[0]
    hb = tm // FFN_HALO
    last_hb = rows // FFN_HALO - 1
    kern = functools.partial(_ffn_kernel, tm=tm, tn=tn, seq_len=seq_len, tiles_per_seq=max(seq_len // tm, 1),
                             latent=latent)
    return pl.pallas_call(
        kern,
        grid=(rows // tm,),
        in_specs=[pl.BlockSpec((tm, D_MODEL), lambda m: (m, 0)),
                  pl.BlockSpec((FFN_HALO, D_MODEL), lambda m: (jnp.maximum(m * hb - 1, 0), 0)),
                  pl.BlockSpec((FFN_HALO, D_MODEL), lambda m: (jnp.minimum((m + 1) * hb, last_hb), 0)),
                  pl.BlockSpec((6, 8, D_MODEL), lambda m: (0, 0, 0)),
                  pl.BlockSpec((1, D_MODEL), lambda m: (0, 0)),
                  pl.BlockSpec((D_MODEL, 2 * D_FF), lambda m: (0, 0)),
                  pl.BlockSpec((3, 2 * D_FF), lambda m: (0, 0)),
                  pl.BlockSpec((D_FF, D_MODEL), lambda m: (0, 0))],
        out_specs=pl.BlockSpec((tm, D_MODEL), lambda m: (m, 0)),
        out_shape=jax.ShapeDtypeStruct((rows, D_MODEL), F32),
        scratch_shapes=[pltpu.VMEM((tm + 2 * FFN_HALO, D_MODEL), BF16), pltpu.VMEM((tm, D_FF), BF16)],
        compiler_params=_params("arbitrary"),
        name="ffn",
    )(x, x, x, mod_l, n2, w_up, f_conv, w_dn)


def _permute_w_in(w):
    o = np.cumsum([0, 512, 512, 512, 512, 512, 512, 512, 8, 8, 256, 256, 512, 512, 32, 1024, 1024, 1024])
    w = w.astype(BF16)
    piece = lambda i: w[:, o[i]:o[i + 1]]
    gq = piece(9).reshape(D_MODEL, GLA_HEADS, GLA_KEY_DIM)
    gk = piece(10).reshape(D_MODEL, GLA_HEADS, GLA_KEY_DIM)
    gqk = jnp.concatenate([gq, gk], axis=-1).reshape(D_MODEL, 2 * GLA_HEADS * GLA_KEY_DIM)
    small = jnp.concatenate([piece(7), piece(8), piece(13),
                             jnp.zeros((D_MODEL, PROJ_PAD - OFF_SMALL - 48), BF16)], axis=-1)
    cols = [piece(i) for i in range(7)] + [gqk, piece(11), piece(12), piece(14), piece(15), piece(16), small]
    return jnp.concatenate(cols, axis=-1)


def _gla_gate_weights(gla_w, gla_b):
    r, dk = GLA_GATE_RANK, GLA_KEY_DIM
    w = gla_w.reshape(DEPTH, 2, r, GLA_HEADS, dk)
    wg = jnp.zeros((DEPTH, GLA_HEADS, 128, 2, dk), F32)
    for d in range(2):
        wg = wg.at[:, :, SMALL_LR + d * r:SMALL_LR + (d + 1) * r, d, :].set(jnp.transpose(w[:, d], (0, 2, 1, 3)))
    bg = jnp.transpose(gla_b.reshape(DEPTH, 2, GLA_HEADS, dk), (0, 2, 1, 3)).reshape(DEPTH, 1, GLA_HEADS * 2 * dk)
    wg = jnp.transpose(wg, (0, 2, 1, 3, 4)).reshape(DEPTH, 128, GLA_HEADS * 2 * dk)
    return wg.astype(BF16), bg.astype(F32)


def _lane_row(values, offset):
    n = values.shape[-1]
    return jnp.zeros((DEPTH, 1, 128), F32).at[:, 0, offset:offset + n].set(values.astype(F32))


def kernel(x_prompt, x_sample, cache_k, cache_v, state_dn, state_gla, c, c_ctx, w_ada, b_ada, norm1, w_in,
           na_q_norm, na_k_norm, na_rpb, dn_conv, dn_a_log, dn_dt_bias, dn_out_norm, gla_w_gate, gla_b_gate,
           gla_out_norm, w_branch, w_out, norm2, w_up, ffn_conv, w_down):
    batch, seq, _ = x_prompt.shape
    dbatch, dseq, _ = x_sample.shape
    past = cache_k.shape[2]

    cvec8 = jnp.concatenate([c_ctx[None, :], c, jnp.zeros((8 - 1 - dbatch, D_MODEL), F32)], axis=0)
    mod = _adaln(cvec8, w_ada, b_ada)

    w_in_p = [_permute_w_in(w_in[l]) for l in range(DEPTH)]
    w_br = [w_branch[l].astype(BF16) for l in range(DEPTH)]
    w_o = [w_out[l].astype(BF16) for l in range(DEPTH)]
    w_up_b = [w_up[l].astype(BF16) for l in range(DEPTH)]
    w_dn_b = [w_down[l].astype(BF16) for l in range(DEPTH)]
    tiles = _na_bias_tiles(na_rpb)
    wg, bg = _gla_gate_weights(gla_w_gate, gla_b_gate)
    alog128 = _lane_row(dn_a_log.reshape(DEPTH, 2 * DN_HEADS), SMALL_DECAY)
    dtb128 = _lane_row(dn_dt_bias.reshape(DEPTH, 2 * DN_HEADS), SMALL_DECAY)
    ck = cache_k.reshape(dbatch, DEPTH, past, NA_HEADS * NA_HEAD_DIM)
    cv = cache_v.reshape(dbatch, DEPTH, past, NA_HEADS * NA_HEAD_DIM)

    y_p = x_prompt.reshape(batch * seq, D_MODEL)
    y_s = x_sample.reshape(dbatch * dseq, D_MODEL)
    new_kv, new_dn, new_gla = None, None, None
    for l in range(DEPTH):
        qg, kg = na_q_norm[l][None, :], na_k_norm[l][None, :]
        n1, n2 = norm1[l][None, :], norm2[l][None, :]
        dn_g, gla_g = dn_out_norm[l][None, :], gla_out_norm[l][None, :]
        for latent in (False, True):
            x = y_s if latent else y_p
            b_, t_ = (dbatch, dseq) if latent else (batch, seq)
            p = _inproj(x, mod[l], n1, w_in_p[l], t_, latent)
            if latent:
                o_a = _lat_attn(p, ck, cv, tiles, qg, kg, l, b_, t_)
                o_b, _ = _deltanet(p, dn_conv[l], alog128[l], dtb128[l], dn_g, state_dn, None, l, b_, t_, False, 2)
                o_c, _ = _gla(p, wg[l], bg[l], gla_g, state_gla, None, l, b_, t_, False, 2)
            else:
                o_a, k_all, v_all = _ctx_attn(p, qg, kg, new_kv, l, b_, t_)
                new_kv = (k_all, v_all)
                o_b, new_dn = _deltanet(p, dn_conv[l], alog128[l], dtb128[l], dn_g, None, new_dn, l, b_, t_, True,
                                        DN_HEADS)
                o_c, new_gla = _gla(p, wg[l], bg[l], gla_g, None, new_gla, l, b_, t_, True, GLA_HEADS)
            x = _merge(x, o_a, o_b, o_c, p, mod[l], w_br[l], w_o[l], t_, latent)
            x = _ffn(x, mod[l], n2, w_up_b[l], ffn_conv[l], w_dn_b[l], t_, latent)
            if latent:
                y_s = x
            else:
                y_p = x
    cache_shape = (batch, DEPTH, seq, NA_HEADS, NA_HEAD_DIM)
    return (y_p.reshape(batch, seq, D_MODEL), y_s.reshape(dbatch, dseq, D_MODEL),
            new_kv[0].reshape(cache_shape), new_kv[1].reshape(cache_shape), new_dn, new_gla)
```

```python
import functools

import numpy as np
import jax
import jax.numpy as jnp
from jax import lax
from jax.experimental import pallas as pl
from jax.experimental.pallas import tpu as pltpu

F32 = jnp.float32
BF16 = jnp.bfloat16

D_MODEL = 1024
DEPTH = 2
GRID_W = 64
EPS = 1e-6
NA_HEADS = 8
NA_HEAD_DIM = 64
NA_WIN_H = 8
NA_WIN_W = 16
DN_HEADS = 4
DN_HEAD_DIM = 128
DN_CHUNK = 64
GLA_HEADS = 4
GLA_KEY_DIM = 64
GLA_VAL_DIM = 128
GLA_GATE_RANK = 16
GLA_TAU = 16.0
GLA_CHUNK = 32
D_FF = 2816

OFF_NA_Q, OFF_NA_K, OFF_NA_V = 0, 512, 1024
OFF_DN_Q, OFF_DN_K, OFF_DN_V, OFF_DN_GATE = 1536, 2048, 2560, 3072
OFF_GLA_QK, OFF_GLA_V, OFF_GLA_GATE = 3584, 4096, 4608
OFF_M = 5120
OFF_SMALL = 8192
PROJ_PAD = 8320
SMALL_BETA, SMALL_DECAY, SMALL_LR = 0, 8, 16

GROUP = 256
NEG = -1e30
VMEM_LIMIT = 56 * 1024 * 1024

NA_QROWS = 4
NA_KROWS = 12


def _bdot(a, b):
    return jnp.dot(a.astype(BF16), b.astype(BF16), preferred_element_type=F32)


def _bdot_nt(a, b):
    return lax.dot_general(a.astype(BF16), b.astype(BF16), (((1,), (1,)), ((), ())), preferred_element_type=F32)


def _bdot_tn(a, b):
    return lax.dot_general(a.astype(BF16), b.astype(BF16), (((0,), (0,)), ((), ())), preferred_element_type=F32)


def _split3(x):
    hi = x.astype(BF16)
    r = x - hi.astype(F32)
    mid = r.astype(BF16)
    lo = (r - mid.astype(F32)).astype(BF16)
    return hi, mid, lo


def _dot_mask_f32(t, x):
    hi, mid, lo = _split3(x)
    d = lambda p: jnp.dot(t, p, preferred_element_type=F32)
    return d(hi) + d(mid) + d(lo)


def _dot3(a, b):
    ah = a.astype(BF16)
    al = (a - ah.astype(F32)).astype(BF16)
    bh = b.astype(BF16)
    bl = (b - bh.astype(F32)).astype(BF16)
    d = lambda p, q: jnp.dot(p, q, preferred_element_type=F32)
    return d(ah, bh) + d(ah, bl) + d(al, bh)


def _sigmoid(x):
    return 1.0 / (1.0 + jnp.exp(-x))


def _silu(x):
    return x * _sigmoid(x)


def _softplus(x):
    return jnp.maximum(x, 0.0) + jnp.log(1.0 + jnp.exp(-jnp.abs(x)))


def _params(*sem):
    return pltpu.CompilerParams(dimension_semantics=sem, vmem_limit_bytes=VMEM_LIMIT)


def _adaln_kernel(c_ref, w_ref, b_ref, o_ref):
    o_ref[...] = _bdot(_silu(c_ref[...]), w_ref[...]) + b_ref[...]


def _adaln(cvec8, w_ada, b_ada):
    return pl.pallas_call(
        _adaln_kernel,
        grid=(DEPTH, 6),
        in_specs=[pl.BlockSpec((8, D_MODEL), lambda l, j: (0, 0)),
                  pl.BlockSpec((None, D_MODEL, D_MODEL), lambda l, j: (l, 0, j)),
                  pl.BlockSpec((None, None, 1, D_MODEL), lambda l, j: (l, j, 0, 0))],
        out_specs=pl.BlockSpec((None, None, 8, D_MODEL), lambda l, j: (l, j, 0, 0)),
        out_shape=jax.ShapeDtypeStruct((DEPTH, 6, 8, D_MODEL), F32),
        compiler_params=_params("arbitrary", "arbitrary"),
        name="adaln",
    )(cvec8, w_ada, b_ada.reshape(DEPTH, 6, 1, D_MODEL))


def _mod_row(tile, tiles_per_seq, latent):
    return (1 + tile // tiles_per_seq) if latent else 0


def _norm_mod(x, gain, shift, scale):
    y = x * lax.rsqrt(jnp.mean(x * x, axis=-1, keepdims=True) + EPS) * gain
    return y * (1.0 + scale) + shift


def _inproj_kernel(x_ref, mod_ref, n1_ref, w_ref, o_ref, *, tiles_per_seq, latent, tn):
    r = _mod_row(pl.program_id(0), tiles_per_seq, latent)
    h = _norm_mod(x_ref[...], n1_ref[...], mod_ref[0, pl.ds(r, 1), :], mod_ref[1, pl.ds(r, 1), :]).astype(BF16)
    for n in range(PROJ_PAD // tn):
        o_ref[:, n * tn:(n + 1) * tn] = jnp.dot(h, w_ref[:, n * tn:(n + 1) * tn], preferred_element_type=F32)


def _resident(stacked, layer):
    shape = stacked.shape[1:]
    zeros = (0,) * len(shape)
    return pl.BlockSpec((None,) + shape, lambda *g: (layer,) + zeros, pipeline_mode=pl.Buffered(1))


def _inproj(x, mod_l, n1, w_in_p, layer, seq_len, latent, tm=256, tn=1664):
    rows = x.shape[0]
    kern = functools.partial(_inproj_kernel, tiles_per_seq=max(seq_len // tm, 1), latent=latent, tn=tn)
    return pl.pallas_call(
        kern,
        grid=(rows // tm,),
        in_specs=[pl.BlockSpec((tm, D_MODEL), lambda m: (m, 0)),
                  pl.BlockSpec((6, 8, D_MODEL), lambda m: (0, 0, 0)),
                  pl.BlockSpec((1, D_MODEL), lambda m: (0, 0)),
                  _resident(w_in_p, layer)],
        out_specs=pl.BlockSpec((tm, PROJ_PAD), lambda m: (m, 0)),
        out_shape=jax.ShapeDtypeStruct((rows, PROJ_PAD), F32),
        compiler_params=_params("arbitrary"),
        name="inproj",
    )(x, mod_l, n1, w_in_p)


def _head_rms(x, gain):
    return x * lax.rsqrt(jnp.mean(x * x, axis=-1, keepdims=True) + EPS) * gain


def _head_slices():
    return [slice(h * NA_HEAD_DIM, (h + 1) * NA_HEAD_DIM) for h in range(NA_HEADS)]


def _ctx_attn_kernel(*refs, has_prev, layer):
    q_ref, k_ref, v_ref, qg_ref, kg_ref = refs[:5]
    o_ref, ka_ref, va_ref = refs[-3:]
    ka_ref = _own_slab(ka_ref, layer, not has_prev)
    va_ref = _own_slab(va_ref, layer, not has_prev)
    hs = _head_slices()
    qn = [_head_rms(q_ref[:, sl], qg_ref[...]) * (NA_HEAD_DIM ** -0.5) for sl in hs]
    kn = [_head_rms(k_ref[:, sl], kg_ref[...]) for sl in hs]
    for sl, k in zip(hs, kn):
        ka_ref[:, sl] = k
    s = [_bdot_nt(q, k) for q, k in zip(qn, kn)]
    p = [jnp.exp(x - jnp.max(x, axis=-1, keepdims=True)) for x in s]
    pv = [_bdot(x, v_ref[:, sl]) for x, sl in zip(p, hs)]
    for sl, x, y in zip(hs, p, pv):
        o_ref[:, sl] = y / jnp.sum(x, axis=-1, keepdims=True)
    va_ref[...] = v_ref[...]


def _layer_slab(shape_tail, index_tail, layer, first):
    if first:
        return pl.BlockSpec((None, DEPTH) + shape_tail, lambda b, *g: (b, 0) + index_tail(*g))
    return pl.BlockSpec((None, None) + shape_tail, lambda b, *g: (b, layer) + index_tail(*g))


def _own_slab(ref, layer, first):
    if not first:
        return ref
    for other in range(DEPTH):
        if other != layer:
            ref[other] = jnp.zeros(ref.shape[1:], ref.dtype)
    return ref.at[layer]


def _ctx_attn(p_ctx, qg, kg, prev_kv, layer, batch, seq):
    rows = batch * seq
    blk = lambda j: pl.BlockSpec((seq, 512), lambda b: (b, j))
    cache = jax.ShapeDtypeStruct((batch, DEPTH, seq, 512), F32)
    cache_spec = _layer_slab((seq, 512), lambda: (0, 0), layer, prev_kv is None)
    in_specs = [blk(OFF_NA_Q // 512), blk(OFF_NA_K // 512), blk(OFF_NA_V // 512),
                pl.BlockSpec((1, NA_HEAD_DIM), lambda b: (0, 0)),
                pl.BlockSpec((1, NA_HEAD_DIM), lambda b: (0, 0))]
    args = [p_ctx, p_ctx, p_ctx, qg, kg]
    aliases = {}
    if prev_kv is not None:
        in_specs += [pl.BlockSpec(memory_space=pl.ANY)] * 2
        args += list(prev_kv)
        aliases = {5: 1, 6: 2}
    return pl.pallas_call(
        functools.partial(_ctx_attn_kernel, has_prev=prev_kv is not None, layer=layer),
        grid=(batch,),
        in_specs=in_specs,
        out_specs=[blk(0), cache_spec, cache_spec],
        out_shape=[jax.ShapeDtypeStruct((rows, 512), F32), cache, cache],
        input_output_aliases=aliases,
        compiler_params=_params("arbitrary"),
        name="ctx_attn",
    )(*args)


NA_NDR = 2 * NA_WIN_H - 1


def _na_bias_tiles(rpb):
    qc = np.arange(GRID_W)
    cs = np.clip(qc - NA_WIN_W // 2, 0, GRID_W - NA_WIN_W)
    col_ok = (qc[None, :] >= cs[:, None]) & (qc[None, :] < cs[:, None] + NA_WIN_W)
    nd = 2 * NA_WIN_W - 1
    dc = np.clip(qc[None, :] - qc[:, None] + NA_WIN_W - 1, 0, nd - 1)
    sel = (np.arange(nd)[:, None, None] == dc[None]).astype(np.float32).reshape(nd, GRID_W * GRID_W)
    t = jnp.einsum('lhrd,dn->lhrn', rpb.astype(F32), jnp.asarray(sel), precision=lax.Precision.HIGHEST)
    t = jnp.where(col_ok, t.reshape(rpb.shape[:3] + (GRID_W, GRID_W)), NEG)
    return jnp.concatenate([t, jnp.full(t.shape[:2] + (1, GRID_W, GRID_W), NEG, F32)], axis=2)


def _lat_attn_kernel(q_ref, k_ref, v_ref, ck_ref, cv_ref, tile_ref, qg_ref, kg_ref, o_ref, kn_scr, bias_scr, *, rows):
    m = pl.program_id(1)
    nblk = rows // NA_QROWS
    hs = _head_slices()

    @pl.when(m == 0)
    def _():
        for sl in hs:
            kn_scr[:, sl] = _head_rms(k_ref[:, sl], kg_ref[...]).astype(BF16)

    krow0 = jnp.clip(NA_QROWS * m - NA_WIN_H // 2, 0, rows - NA_KROWS)

    @pl.when((m == 0) | (m == 1) | (m == nblk - 1))
    def _():
        for i in range(NA_QROWS):
            r = NA_QROWS * m + i
            rs = jnp.clip(r - NA_WIN_H // 2, 0, rows - NA_WIN_H)
            for j in range(NA_KROWS):
                kr = krow0 + j
                idx = jnp.where((kr >= rs) & (kr < rs + NA_WIN_H), kr - r + NA_WIN_H - 1, NA_NDR)
                for h in range(NA_HEADS):
                    bias_scr[h, i * GRID_W:(i + 1) * GRID_W, j * GRID_W:(j + 1) * GRID_W] = tile_ref[h, idx]

    kstart = pl.multiple_of(krow0 * GRID_W, 256)
    nk = NA_KROWS * GRID_W
    qn = [(_head_rms(q_ref[:, sl], qg_ref[...]) * (NA_HEAD_DIM ** -0.5)).astype(BF16) for sl in hs]
    s_loc = [_bdot_nt(q, kn_scr[pl.ds(kstart, nk), sl]) + bias_scr[h] for h, (q, sl) in enumerate(zip(qn, hs))]
    s_ctx = [_bdot_nt(q, ck_ref[:, sl]) for q, sl in zip(qn, hs)]
    mx = [jnp.maximum(jnp.max(a, axis=-1, keepdims=True), jnp.max(c, axis=-1, keepdims=True))
          for a, c in zip(s_loc, s_ctx)]
    p_loc = [jnp.exp(a - x) for a, x in zip(s_loc, mx)]
    p_ctx = [jnp.exp(c - x) for c, x in zip(s_ctx, mx)]
    o_loc = [_bdot(p, v_ref[pl.ds(kstart, nk), sl]) for p, sl in zip(p_loc, hs)]
    o_ctx = [_bdot(p, cv_ref[:, sl]) for p, sl in zip(p_ctx, hs)]
    for h, sl in enumerate(hs):
        den = jnp.sum(p_loc[h], axis=-1, keepdims=True) + jnp.sum(p_ctx[h], axis=-1, keepdims=True)
        o_ref[:, sl] = (o_loc[h] + o_ctx[h]) / den


def _lat_attn(p_lat, cache_k, cache_v, tiles, qg, kg, layer, batch, seq):
    rows = seq // GRID_W
    nblk = rows // NA_QROWS
    qtok = NA_QROWS * GRID_W
    past = cache_k.shape[2]
    cache_spec = pl.BlockSpec((None, None, past, 512), lambda b, m: (b, layer, 0, 0))
    return pl.pallas_call(
        functools.partial(_lat_attn_kernel, rows=rows),
        grid=(batch, nblk),
        in_specs=[pl.BlockSpec((qtok, 512), lambda b, m: (b * nblk + m, OFF_NA_Q // 512)),
                  pl.BlockSpec((seq, 512), lambda b, m: (b, OFF_NA_K // 512)),
                  pl.BlockSpec((seq, 512), lambda b, m: (b, OFF_NA_V // 512)),
                  cache_spec, cache_spec,
                  pl.BlockSpec((None, NA_HEADS, NA_NDR + 1, GRID_W, GRID_W), lambda b, m: (layer, 0, 0, 0, 0)),
                  pl.BlockSpec((1, NA_HEAD_DIM), lambda b, m: (0, 0)),
                  pl.BlockSpec((1, NA_HEAD_DIM), lambda b, m: (0, 0))],
        out_specs=pl.BlockSpec((qtok, 512), lambda b, m: (b * nblk + m, 0)),
        out_shape=jax.ShapeDtypeStruct((batch * seq, 512), F32),
        scratch_shapes=[pltpu.VMEM((seq, 512), BF16),
                        pltpu.VMEM((NA_HEADS, qtok, NA_KROWS * GRID_W), F32)],
        compiler_params=_params("arbitrary", "arbitrary"),
        name="lat_attn",
    )(p_lat, p_lat, p_lat, cache_k, cache_v, tiles, qg, kg)


def _group_masks(chunk):
    r = lax.broadcasted_iota(jnp.int32, (GROUP, GROUP), 0)
    c = lax.broadcasted_iota(jnp.int32, (GROUP, GROUP), 1)
    same = (r // chunk) == (c // chunk)
    return (same & (c <= r), same & (c >= r), same & (c < r), same & (c > r), same)


def _halving_masks(chunk, lower):
    r = lax.broadcasted_iota(jnp.int32, (GROUP, GROUP), 0)
    c = lax.broadcasted_iota(jnp.int32, (GROUP, GROUP), 1)
    if not lower:
        r, c = c, r
    masks = []
    s = 1
    while s < chunk:
        m = ((r // (2 * s)) == (c // (2 * s))) & ((r // s) % 2 == 1) & ((c // s) % 2 == 0)
        masks.append(jnp.where(m, 1.0, 0.0).astype(BF16))
        s *= 2
    return masks


def _group_eye():
    r = lax.broadcasted_iota(jnp.int32, (GROUP, GROUP), 0)
    c = lax.broadcasted_iota(jnp.int32, (GROUP, GROUP), 1)
    return jnp.where(r == c, 1.0, 0.0).astype(BF16)


def _sel_col(a, lane):
    idx = lax.broadcasted_iota(jnp.int32, a.shape, 1)
    return jnp.sum(jnp.where(idx == lane, a, 0.0), axis=1, keepdims=True)


def _sel_row(a, row):
    idx = lax.broadcasted_iota(jnp.int32, a.shape, 0)
    return jnp.sum(jnp.where(idx == row, a, 0.0), axis=0, keepdims=True)


def _conv3_silu(x, w):
    t = x.shape[0]
    row = lax.broadcasted_iota(jnp.int32, x.shape, 0)
    xm = jnp.where(row == 0, 0.0, pltpu.roll(x, 1, 0))
    xp = jnp.where(row == t - 1, 0.0, pltpu.roll(x, t - 1, 0))
    return _silu(xm * w[0:1, :] + x * w[1:2, :] + xp * w[2:3, :])


def _deltanet_kernel(*refs, hp, has_s0, emit_state, has_prev, layer):
    (q_ref, k_ref, v_ref, gate_ref, sm_ref, cq_ref, ck_ref, cv_ref, alog_ref, dtb_ref, og_ref), refs = refs[:11], refs[11:]
    if has_s0:
        s0_ref, refs = refs[0], refs[1:]
    if has_prev:
        refs = refs[1:]
    o_ref, refs = refs[0], refs[1:]
    if emit_state:
        sfin_ref, refs = refs[0], refs[1:]
    qn_scr, kn_scr, vc_scr, of_scr, ob_scr, s_scr = refs

    head0 = pl.program_id(1) * hp
    seq = q_ref.shape[0]
    hd = DN_HEAD_DIM
    ngroups = seq // GROUP
    nchunk = GROUP // DN_CHUNK

    q = _conv3_silu(q_ref[...], cq_ref[...])
    k = _conv3_silu(k_ref[...], ck_ref[...])
    vc_scr[...] = _conv3_silu(v_ref[...], cv_ref[...])
    for j in range(hp):
        sl = slice(j * hd, (j + 1) * hd)
        qj, kj = q[:, sl], k[:, sl]
        qn_scr[:, sl] = qj * lax.rsqrt(jnp.sum(qj * qj, axis=-1, keepdims=True) + EPS) * (hd ** -0.5)
        kn_scr[:, sl] = kj * lax.rsqrt(jnp.sum(kj * kj, axis=-1, keepdims=True) + EPS)

    if has_s0:
        s_scr[...] = s0_ref[...]
    else:
        s_scr[...] = jnp.zeros_like(s_scr)

    m_fi, m_bi, m_fs, m_bs, m_same = _group_masks(DN_CHUNK)
    incl = (m_fi, m_bi)
    strict = (m_fs, m_bs)
    tri = (m_fi.astype(BF16), m_bi.astype(BF16))
    ones_bd = m_same.astype(BF16)
    halving = (_halving_masks(DN_CHUNK, True), _halving_masks(DN_CHUNK, False))

    eye = _group_eye()
    chains = [(d, j) for d in range(2) for j in range(hp)]
    nc = len(chains)

    def group_step(t, carry):
        shared = []
        for d in range(2):
            gi = t if d == 0 else ngroups - 1 - t
            r0 = pl.multiple_of(gi * GROUP, GROUP)
            sm = sm_ref[pl.ds(r0, GROUP), :]
            g128 = -jnp.exp(alog_ref[...]) * _softplus(sm + dtb_ref[...])
            gcum = _dot_mask_f32(tri[d], g128)
            gtot = _dot_mask_f32(ones_bd, g128)
            shared.append((r0, gcum, gtot, gcum.T, _sigmoid(sm)))

        qg, kg, vg, beta, gc_col, gt_col, dec = [], [], [], [], [], [], []
        for d, j in chains:
            r0, gcum, gtot, gcum_t, beta128 = shared[d]
            sl = slice(j * hd, (j + 1) * hd)
            lane_g = SMALL_DECAY + DN_HEADS * d + head0 + j
            gc_col.append(_sel_col(gcum, lane_g))
            gt_col.append(_sel_col(gtot, lane_g))
            beta.append(_sel_col(beta128, SMALL_BETA + DN_HEADS * d + head0 + j))
            dec.append(jnp.exp(jnp.where(incl[d], gc_col[-1] - _sel_row(gcum_t, lane_g), NEG)))
            qg.append(qn_scr[pl.ds(r0, GROUP), sl])
            kg.append(kn_scr[pl.ds(r0, GROUP), sl])
            vg.append(vc_scr[pl.ds(r0, GROUP), sl])
        rng = range(nc)
        kk = [_bdot_nt(kg[i], kg[i]) for i in rng]
        qk = [_bdot_nt(qg[i], kg[i]) for i in rng]
        low = [jnp.where(strict[chains[i][0]], beta[i] * kk[i] * dec[i], 0.0).astype(BF16) for i in rng]
        attn = [(qk[i] * dec[i]).astype(BF16) for i in rng]
        e_gc = [jnp.exp(gc_col[i]) for i in rng]
        x = [jnp.concatenate([vg[i] * beta[i], kg[i] * (beta[i] * e_gc[i])], axis=1).astype(BF16) for i in rng]
        inv = [eye - low[i] * halving[chains[i][0]][0] for i in rng]
        for lvl in range(1, len(halving[0])):
            tmp = [_bdot(inv[i], low[i] * halving[chains[i][0]][lvl]).astype(BF16) for i in rng]
            upd = [_bdot(tmp[i], inv[i]).astype(BF16) for i in rng]
            inv = [inv[i] - upd[i] for i in rng]
        x = [_bdot(inv[i], x[i]) for i in rng]
        w = [x[i][:, hd:].astype(BF16) for i in rng]
        qd = [(qg[i] * e_gc[i]).astype(BF16) for i in rng]
        kd = [(kg[i] * jnp.exp(gt_col[i] - gc_col[i])).astype(BF16) for i in rng]
        s = [s_scr[d, j] for d, j in chains]
        o_parts = [[None] * nchunk for _ in rng]
        for cc in range(nchunk):
            rs = []
            for d, _ in chains:
                c = cc if d == 0 else nchunk - 1 - cc
                rs.append(slice(c * DN_CHUNK, (c + 1) * DN_CHUNK))
            ws = [_bdot(w[i][rs[i]], s[i]) for i in rng]
            qs = [_bdot(qd[i][rs[i]], s[i]) for i in rng]
            v_new = [x[i][rs[i], :hd] - ws[i] for i in rng]
            av = [_bdot(attn[i][rs[i], rs[i]], v_new[i]) for i in rng]
            kv = [_bdot_tn(kd[i][rs[i]], v_new[i]) for i in rng]
            for i in rng:
                o_parts[i][rs[i].start // DN_CHUNK] = qs[i] + av[i]
                s[i] = s[i] * jnp.exp(gt_col[i][rs[i].start:rs[i].start + 1, :]) + kv[i]
        for i, (d, j) in enumerate(chains):
            s_scr[d, j] = s[i]
            o_scr = of_scr if d == 0 else ob_scr
            o_scr[pl.ds(shared[d][0], GROUP), j * hd:(j + 1) * hd] = jnp.concatenate(o_parts[i], axis=0)
        return carry

    if ngroups == 1:
        group_step(0, 0)
    else:
        lax.fori_loop(0, ngroups, group_step, 0)

    for j in range(hp):
        sl = slice(j * hd, (j + 1) * hd)
        o = of_scr[:, sl] + ob_scr[:, sl]
        o = o * lax.rsqrt(jnp.mean(o * o, axis=-1, keepdims=True) + EPS) * og_ref[...]
        o_ref[:, sl] = o * _silu(gate_ref[:, sl])
    if emit_state:
        _own_slab(sfin_ref, layer, not has_prev)[...] = s_scr[...]


def _deltanet(p, conv_w, alog128, dtb128, out_g, s0, prev_state, layer, batch, seq, emit_state, hp):
    rows = batch * seq
    hd = DN_HEAD_DIM
    w = hp * hd
    col = lambda off: pl.BlockSpec((seq, w), lambda b, h: (b, off // w + h))
    cw = lambda part: pl.BlockSpec((3, w), lambda b, h: (0, part * (DN_HEADS // hp) + h))
    row128 = pl.BlockSpec((1, 128), lambda b, h: (0, 0))
    in_specs = [col(OFF_DN_Q), col(OFF_DN_K), col(OFF_DN_V), col(OFF_DN_GATE),
                pl.BlockSpec((seq, 128), lambda b, h: (b, OFF_SMALL // 128)),
                cw(0), cw(1), cw(2), row128, row128, row128]
    args = [p, p, p, p, p, conv_w, conv_w, conv_w, alog128, dtb128, out_g]
    if s0 is not None:
        in_specs.append(pl.BlockSpec((None, None, 2, hp, hd, hd), lambda b, h: (b, layer, 0, h, 0, 0)))
        args.append(s0)
    out_specs = [pl.BlockSpec((seq, w), lambda b, h: (b, h))]
    out_shape = [jax.ShapeDtypeStruct((rows, DN_HEADS * hd), F32)]
    aliases = {}
    if emit_state:
        out_specs.append(_layer_slab((2, hp, hd, hd), lambda h: (0, h, 0, 0), layer, prev_state is None))
        out_shape.append(jax.ShapeDtypeStruct((batch, DEPTH, 2, DN_HEADS, hd, hd), F32))
        if prev_state is not None:
            in_specs.append(pl.BlockSpec(memory_space=pl.ANY))
            args.append(prev_state)
            aliases = {len(args) - 1: 1}
    res = pl.pallas_call(
        functools.partial(_deltanet_kernel, hp=hp, has_s0=s0 is not None, emit_state=emit_state,
                          has_prev=prev_state is not None, layer=layer),
        grid=(batch, DN_HEADS // hp),
        in_specs=in_specs,
        out_specs=out_specs,
        out_shape=out_shape,
        input_output_aliases=aliases,
        scratch_shapes=[pltpu.VMEM((seq, w), F32)] * 5 + [pltpu.VMEM((2, hp, hd, hd), F32)],
        compiler_params=_params("arbitrary", "arbitrary"),
        name="deltanet",
    )(*args)
    return res if emit_state else (res[0], None)


def _transpose_f32(x):
    n = x.shape[1]
    r = lax.broadcasted_iota(jnp.int32, (n, n), 0)
    c = lax.broadcasted_iota(jnp.int32, (n, n), 1)
    eye = jnp.where(r == c, 1.0, 0.0).astype(BF16)
    d = lambda p: lax.dot_general(eye, p, (((1,), (1,)), ((), ())), preferred_element_type=F32)
    hi, mid, lo = _split3(x)
    return d(hi) + d(mid) + d(lo)


def _gla_kernel(*refs, hp, has_s0, emit_state, has_prev, layer):
    (qk_ref, v_ref, gate_ref, sm_ref, wg_ref, bg_ref, og_ref), refs = refs[:7], refs[7:]
    if has_s0:
        s0_ref, refs = refs[0], refs[1:]
    if has_prev:
        refs = refs[1:]
    o_ref, refs = refs[0], refs[1:]
    if emit_state:
        sfin_ref, refs = refs[0], refs[1:]
    of_scr, ob_scr, s_scr = refs

    seq = qk_ref.shape[0]
    ngroups = seq // GROUP
    nchunk = GROUP // GLA_CHUNK
    dk = GLA_KEY_DIM

    for d in range(2):
        for j in range(hp):
            if has_s0:
                s_scr[d, j] = _transpose_f32(s0_ref[d, j])
            else:
                s_scr[d, j] = jnp.zeros(s_scr.shape[2:], F32)

    m_fi, m_bi, _, _, m_same = _group_masks(GLA_CHUNK)
    incl = (m_fi, m_bi)
    tri = (m_fi.astype(BF16), m_bi.astype(BF16))
    ones_bd = m_same.astype(BF16)

    chains = [(d, j) for d in range(2) for j in range(hp)]
    rng = range(len(chains))

    def group_step(t, carry):
        shared = []
        for d in range(2):
            gi = t if d == 0 else ngroups - 1 - t
            r0 = pl.multiple_of(gi * GROUP, GROUP)
            gl = _bdot(sm_ref[pl.ds(r0, GROUP), :], wg_ref[...]) + bg_ref[...]
            log_a = -_softplus(-gl) / GLA_TAU
            b_all = _dot_mask_f32(tri[d], log_a)
            btot_all = _dot_mask_f32(ones_bd, log_a)
            shared.append((r0, b_all, btot_all))
        b, btot, kg, vg, qe = [], [], [], [], []
        for d, j in chains:
            r0, b_all, btot_all = shared[d]
            sl = slice(j * 128, (j + 1) * 128)
            lanes = slice(j * 128 + d * dk, j * 128 + (d + 1) * dk)
            b.append(b_all[:, lanes])
            btot.append(btot_all[:, lanes])
            qk = qk_ref[pl.ds(r0, GROUP), sl]
            kg.append(qk[:, dk:])
            vg.append(v_ref[pl.ds(r0, GROUP), sl])
            qe.append((qk[:, :dk] * (dk ** -0.5) * jnp.exp(b[-1])).astype(BF16))
        vg = [v.astype(BF16) for v in vg]
        attn = [jnp.where(incl[chains[i][0]], _bdot_nt(qe[i], kg[i] * jnp.exp(-b[i])), 0.0).astype(BF16)
                for i in rng]
        o_in = [_bdot(attn[i], vg[i]) for i in rng]
        kd = [(kg[i] * jnp.exp(btot[i] - b[i])).astype(BF16) for i in rng]
        st = [s_scr[d, j] for d, j in chains]
        o_parts = [[None] * nchunk for _ in rng]
        for cc in range(nchunk):
            rs = []
            for d, _ in chains:
                c = cc if d == 0 else nchunk - 1 - cc
                rs.append(slice(c * GLA_CHUNK, (c + 1) * GLA_CHUNK))
            qs = [_bdot_nt(qe[i][rs[i]], st[i]) for i in rng]
            vk = [_bdot_tn(vg[i][rs[i]], kd[i][rs[i]]) for i in rng]
            for i in rng:
                o_parts[i][rs[i].start // GLA_CHUNK] = o_in[i][rs[i]] + qs[i]
                st[i] = st[i] * jnp.exp(btot[i][rs[i].start:rs[i].start + 1, :]) + vk[i]
        for i, (d, j) in enumerate(chains):
            s_scr[d, j] = st[i]
            o_scr = of_scr if d == 0 else ob_scr
            o_scr[pl.ds(shared[d][0], GROUP), j * 128:(j + 1) * 128] = jnp.concatenate(o_parts[i], axis=0)
        return carry

    if ngroups == 1:
        group_step(0, 0)
    else:
        lax.fori_loop(0, ngroups, group_step, 0)

    for j in range(hp):
        sl = slice(j * 128, (j + 1) * 128)
        o = of_scr[:, sl] + ob_scr[:, sl]
        o = o * lax.rsqrt(jnp.mean(o * o, axis=-1, keepdims=True) + EPS) * og_ref[...]
        o_ref[:, sl] = o * _silu(gate_ref[:, sl])
    if emit_state:
        sfin = _own_slab(sfin_ref, layer, not has_prev)
        for d in range(2):
            for j in range(hp):
                sfin[d, j] = _transpose_f32(s_scr[d, j])


def _gla(p, wg, bg, out_g, s0, prev_state, layer, batch, seq, emit_state, hp):
    rows = batch * seq
    dv, dk = GLA_VAL_DIM, GLA_KEY_DIM
    w = hp * 128
    col = lambda off: pl.BlockSpec((seq, w), lambda b, h: (b, off // w + h))
    in_specs = [col(OFF_GLA_QK), col(OFF_GLA_V), col(OFF_GLA_GATE),
                pl.BlockSpec((seq, 128), lambda b, h: (b, OFF_SMALL // 128)),
                pl.BlockSpec((128, w), lambda b, h: (0, h)),
                pl.BlockSpec((1, w), lambda b, h: (0, h)),
                pl.BlockSpec((1, 128), lambda b, h: (0, 0))]
    args = [p, p, p, p, wg, bg, out_g]
    if s0 is not None:
        in_specs.append(pl.BlockSpec((None, None, 2, hp, dk, dv), lambda b, h: (b, layer, 0, h, 0, 0)))
        args.append(s0)
    out_specs = [pl.BlockSpec((seq, w), lambda b, h: (b, h))]
    out_shape = [jax.ShapeDtypeStruct((rows, GLA_HEADS * dv), F32)]
    aliases = {}
    if emit_state:
        out_specs.append(_layer_slab((2, hp, dk, dv), lambda h: (0, h, 0, 0), layer, prev_state is None))
        out_shape.append(jax.ShapeDtypeStruct((batch, DEPTH, 2, GLA_HEADS, dk, dv), F32))
        if prev_state is not None:
            in_specs.append(pl.BlockSpec(memory_space=pl.ANY))
            args.append(prev_state)
            aliases = {len(args) - 1: 1}
    res = pl.pallas_call(
        functools.partial(_gla_kernel, hp=hp, has_s0=s0 is not None, emit_state=emit_state,
                          has_prev=prev_state is not None, layer=layer),
        grid=(batch, GLA_HEADS // hp),
        in_specs=in_specs,
        out_specs=out_specs,
        out_shape=out_shape,
        input_output_aliases=aliases,
        scratch_shapes=[pltpu.VMEM((seq, w), F32)] * 2 + [pltpu.VMEM((2, hp, dv, dk), F32)],
        compiler_params=_params("arbitrary", "arbitrary"),
        name="gla",
    )(*args)
    return res if emit_state else (res[0], None)


def _merge_kernel(x_ref, oa_ref, ob_ref, oc_ref, ma_ref, mb_ref, mc_ref, mod_ref, wbr_ref, wo_ref, o_ref,
                  *, tiles_per_seq, latent):
    r = _mod_row(pl.program_id(0), tiles_per_seq, latent)
    merged = (_sigmoid(ma_ref[...]) * _bdot(oa_ref[...], wbr_ref[0])
              + _sigmoid(mb_ref[...]) * _bdot(ob_ref[...], wbr_ref[1])
              + _sigmoid(mc_ref[...]) * _bdot(oc_ref[...], wbr_ref[2]))
    o_ref[...] = x_ref[...] + mod_ref[2, pl.ds(r, 1), :] * _bdot(merged, wo_ref[...])


def _merge(x, o_a, o_b, o_c, p, mod_l, w_br, w_o, layer, seq_len, latent, tm=512):
    rows = x.shape[0]
    row_blk = lambda w: pl.BlockSpec((tm, w), lambda m: (m, 0))
    gate_blk = lambda j: pl.BlockSpec((tm, D_MODEL), lambda m: (m, OFF_M // D_MODEL + j))
    return pl.pallas_call(
        functools.partial(_merge_kernel, tiles_per_seq=max(seq_len // tm, 1), latent=latent),
        grid=(rows // tm,),
        in_specs=[row_blk(D_MODEL), row_blk(512), row_blk(512), row_blk(512),
                  gate_blk(0), gate_blk(1), gate_blk(2),
                  pl.BlockSpec((6, 8, D_MODEL), lambda m: (0, 0, 0)),
                  _resident(w_br, layer), _resident(w_o, layer)],
        out_specs=row_blk(D_MODEL),
        out_shape=jax.ShapeDtypeStruct((rows, D_MODEL), F32),
        compiler_params=_params("arbitrary"),
        name="merge",
    )(x, o_a, o_b, o_c, p, p, p, mod_l, w_br, w_o)


FFN_HALO = 16


def _ffn_kernel(x_ref, xp_ref, xn_ref, mod_ref, n2_ref, wu_ref, fc_ref, wd_ref, o_ref, h_scr, act_scr,
                *, tm, tn, seq_len, tiles_per_seq, latent):
    r = _mod_row(pl.program_id(0), tiles_per_seq, latent)
    shift = mod_ref[3, pl.ds(r, 1), :]
    scale = mod_ref[4, pl.ds(r, 1), :]
    nm = lambda x: _norm_mod(x, n2_ref[...], shift, scale)
    row0 = pl.program_id(0) * tm
    h_scr[0:FFN_HALO, :] = jnp.where(row0 % seq_len == 0, 0.0, nm(xp_ref[...])).astype(BF16)
    h_scr[FFN_HALO:FFN_HALO + tm, :] = nm(x_ref[...]).astype(BF16)
    h_scr[FFN_HALO + tm:, :] = jnp.where((row0 + tm) % seq_len == 0, 0.0, nm(xn_ref[...])).astype(BF16)

    ext = tm + 2 * FFN_HALO
    inner = slice(FFN_HALO, FFN_HALO + tm)
    bounds = list(range(seq_len, tm, seq_len))
    fix = 16
    rid = lax.broadcasted_iota(jnp.int32, (2 * fix, 1), 0)

    def up(n):
        hh = h_scr[...]
        cv = slice(n * tn, (n + 1) * tn)
        cg = slice(D_FF + n * tn, D_FF + (n + 1) * tn)
        return (jnp.dot(hh, wu_ref[:, cv], preferred_element_type=F32),
                jnp.dot(hh, wu_ref[:, cg], preferred_element_type=F32))

    def taps(u):
        return pltpu.roll(u, 1, 0)[inner], u[inner], pltpu.roll(u, ext - 1, 0)[inner]

    def conv(t, f, rows=slice(None), keep_prev=None, keep_next=None):
        prev, mid, nxt = (a[rows] for a in t)
        if keep_prev is not None:
            prev = jnp.where(keep_prev, prev, 0.0)
            nxt = jnp.where(keep_next, nxt, 0.0)
        return prev * f[0:1, :] + mid * f[1:2, :] + nxt * f[2:3, :]

    nt = D_FF // tn
    u_next = up(0)
    for n in range(nt):
        u_val, u_gate = u_next
        if n + 1 < nt:
            u_next = up(n + 1)
        cols = slice(n * tn, (n + 1) * tn)
        f_val = fc_ref[:, cols]
        f_gate = fc_ref[:, D_FF + n * tn:D_FF + (n + 1) * tn]
        t_val, t_gate = taps(u_val), taps(u_gate)
        act_scr[:, cols] = (_silu(conv(t_gate, f_gate)) * conv(t_val, f_val)).astype(BF16)
        for b in bounds:
            rows = slice(b - fix, b + fix)
            kp, kn = rid != fix, rid != fix - 1
            act_scr[rows, cols] = (_silu(conv(t_gate, f_gate, rows, kp, kn))
                                   * conv(t_val, f_val, rows, kp, kn)).astype(BF16)
    down = jnp.dot(act_scr[...], wd_ref[...], preferred_element_type=F32)
    o_ref[...] = x_ref[...] + mod_ref[5, pl.ds(r, 1), :] * down


def _ffn(x, mod_l, n2, w_up, f_conv, w_dn, layer, seq_len, latent, tm=1024, tn=256):
    rows = x.shape[0]
    hb = tm // FFN_HALO
    last_hb = rows // FFN_HALO - 1
    kern = functools.partial(_ffn_kernel, tm=tm, tn=tn, seq_len=seq_len, tiles_per_seq=max(seq_len // tm, 1),
                             latent=latent)
    return pl.pallas_call(
        kern,
        grid=(rows // tm,),
        in_specs=[pl.BlockSpec((tm, D_MODEL), lambda m: (m, 0)),
                  pl.BlockSpec((FFN_HALO, D_MODEL), lambda m: (jnp.maximum(m * hb - 1, 0), 0)),
                  pl.BlockSpec((FFN_HALO, D_MODEL), lambda m: (jnp.minimum((m + 1) * hb, last_hb), 0)),
                  pl.BlockSpec((6, 8, D_MODEL), lambda m: (0, 0, 0)),
                  pl.BlockSpec((1, D_MODEL), lambda m: (0, 0)),
                  _resident(w_up, layer), _resident(f_conv, layer), _resident(w_dn, layer)],
        out_specs=pl.BlockSpec((tm, D_MODEL), lambda m: (m, 0)),
        out_shape=jax.ShapeDtypeStruct((rows, D_MODEL), F32),
        scratch_shapes=[pltpu.VMEM((tm + 2 * FFN_HALO, D_MODEL), BF16), pltpu.VMEM((tm, D_FF), BF16)],
        compiler_params=_params("arbitrary"),
        name="ffn",
    )(x, x, x, mod_l, n2, w_up, f_conv, w_dn)


def _permute_w_in(w_in):
    o = np.cumsum([0, 512, 512, 512, 512, 512, 512, 512, 8, 8, 256, 256, 512, 512, 32, 1024, 1024, 1024])
    piece = lambda i: w_in[:, :, o[i]:o[i + 1]]
    gq = piece(9).reshape(DEPTH, D_MODEL, GLA_HEADS, GLA_KEY_DIM)
    gk = piece(10).reshape(DEPTH, D_MODEL, GLA_HEADS, GLA_KEY_DIM)
    gqk = jnp.concatenate([gq, gk], axis=-1).reshape(DEPTH, D_MODEL, 2 * GLA_HEADS * GLA_KEY_DIM)
    small = jnp.concatenate([piece(7), piece(8), piece(13),
                             jnp.zeros((DEPTH, D_MODEL, PROJ_PAD - OFF_SMALL - 48), w_in.dtype)], axis=-1)
    cols = [piece(i) for i in range(7)] + [gqk, piece(11), piece(12), piece(14), piece(15), piece(16), small]
    return jnp.concatenate(cols, axis=-1).astype(BF16)


def _gla_gate_weights(gla_w, gla_b):
    r, dk = GLA_GATE_RANK, GLA_KEY_DIM
    w = gla_w.reshape(DEPTH, 2, r, GLA_HEADS, dk)
    wg = jnp.zeros((DEPTH, GLA_HEADS, 128, 2, dk), F32)
    for d in range(2):
        wg = wg.at[:, :, SMALL_LR + d * r:SMALL_LR + (d + 1) * r, d, :].set(jnp.transpose(w[:, d], (0, 2, 1, 3)))
    bg = jnp.transpose(gla_b.reshape(DEPTH, 2, GLA_HEADS, dk), (0, 2, 1, 3)).reshape(DEPTH, 1, GLA_HEADS * 2 * dk)
    wg = jnp.transpose(wg, (0, 2, 1, 3, 4)).reshape(DEPTH, 128, GLA_HEADS * 2 * dk)
    return wg.astype(BF16), bg.astype(F32)


def _lane_row(values, offset):
    n = values.shape[-1]
    return jnp.zeros((DEPTH, 1, 128), F32).at[:, 0, offset:offset + n].set(values.astype(F32))


def kernel(x_prompt, x_sample, cache_k, cache_v, state_dn, state_gla, c, c_ctx, w_ada, b_ada, norm1, w_in,
           na_q_norm, na_k_norm, na_rpb, dn_conv, dn_a_log, dn_dt_bias, dn_out_norm, gla_w_gate, gla_b_gate,
           gla_out_norm, w_branch, w_out, norm2, w_up, ffn_conv, w_down):
    batch, seq, _ = x_prompt.shape
    dbatch, dseq, _ = x_sample.shape
    past = cache_k.shape[2]

    cvec8 = jnp.concatenate([c_ctx[None, :], c, jnp.zeros((8 - 1 - dbatch, D_MODEL), F32)], axis=0)
    mod = _adaln(cvec8, w_ada, b_ada)

    w_in_p = _permute_w_in(w_in)
    w_br = w_branch.astype(BF16)
    w_o = w_out.astype(BF16)
    w_up_b = w_up.astype(BF16)
    w_dn_b = w_down.astype(BF16)
    tiles = _na_bias_tiles(na_rpb)
    wg, bg = _gla_gate_weights(gla_w_gate, gla_b_gate)
    alog128 = _lane_row(dn_a_log.reshape(DEPTH, 2 * DN_HEADS), SMALL_DECAY)
    dtb128 = _lane_row(dn_dt_bias.reshape(DEPTH, 2 * DN_HEADS), SMALL_DECAY)
    ck = cache_k.reshape(dbatch, DEPTH, past, NA_HEADS * NA_HEAD_DIM)
    cv = cache_v.reshape(dbatch, DEPTH, past, NA_HEADS * NA_HEAD_DIM)

    y_p = x_prompt.reshape(batch * seq, D_MODEL)
    y_s = x_sample.reshape(dbatch * dseq, D_MODEL)
    new_kv, new_dn, new_gla = None, None, None
    for l in range(DEPTH):
        qg, kg = na_q_norm[l][None, :], na_k_norm[l][None, :]
        n1, n2 = norm1[l][None, :], norm2[l][None, :]
        dn_g, gla_g = dn_out_norm[l][None, :], gla_out_norm[l][None, :]
        for latent in (False, True):
            x = y_s if latent else y_p
            b_, t_ = (dbatch, dseq) if latent else (batch, seq)
            p = _inproj(x, mod[l], n1, w_in_p, l, t_, latent)
            if latent:
                o_a = _lat_attn(p, ck, cv, tiles, qg, kg, l, b_, t_)
                o_b, _ = _deltanet(p, dn_conv[l], alog128[l], dtb128[l], dn_g, state_dn, None, l, b_, t_, False, 2)
                o_c, _ = _gla(p, wg[l], bg[l], gla_g, state_gla, None, l, b_, t_, False, 2)
            else:
                o_a, k_all, v_all = _ctx_attn(p, qg, kg, new_kv, l, b_, t_)
                new_kv = (k_all, v_all)
                o_b, new_dn = _deltanet(p, dn_conv[l], alog128[l], dtb128[l], dn_g, None, new_dn, l, b_, t_, True,
                                        DN_HEADS)
                o_c, new_gla = _gla(p, wg[l], bg[l], gla_g, None, new_gla, l, b_, t_, True, GLA_HEADS)
            x = _merge(x, o_a, o_b, o_c, p, mod[l], w_br, w_o, l, t_, latent)
            x = _ffn(x, mod[l], n2, w_up_b, ffn_conv, w_dn_b, l, t_, latent)
            if latent:
                y_s = x
            else:
                y_p = x
    cache_shape = (batch, DEPTH, seq, NA_HEADS, NA_HEAD_DIM)
    return (y_p.reshape(batch, seq, D_MODEL), y_s.reshape(dbatch, dseq, D_MODEL),
            new_kv[0].reshape(cache_shape), new_kv[1].reshape(cache_shape), new_dn, new_gla)
```

```python
import functools

import numpy as np
import jax
import jax.numpy as jnp
from jax import lax
from jax.experimental import pallas as pl
from jax.experimental.pallas import tpu as pltpu

F32 = jnp.float32
BF16 = jnp.bfloat16

D_MODEL = 1024
DEPTH = 2
GRID_W = 64
EPS = 1e-6
NA_HEADS = 8
NA_HEAD_DIM = 64
NA_WIN_H = 8
NA_WIN_W = 16
DN_HEADS = 4
DN_HEAD_DIM = 128
DN_CHUNK = 64
GLA_HEADS = 4
GLA_KEY_DIM = 64
GLA_VAL_DIM = 128
GLA_GATE_RANK = 16
GLA_TAU = 16.0
GLA_CHUNK = 32
D_FF = 2816

OFF_NA_Q, OFF_NA_K, OFF_NA_V = 0, 512, 1024
OFF_DN_Q, OFF_DN_K, OFF_DN_V, OFF_DN_GATE = 1536, 2048, 2560, 3072
OFF_GLA_QK, OFF_GLA_V, OFF_GLA_GATE = 3584, 4096, 4608
OFF_M = 5120
OFF_SMALL = 8192
PROJ_PAD = 8320
SMALL_BETA, SMALL_DECAY, SMALL_LR = 0, 8, 16

GROUP = 256
NEG = -1e30
VMEM_LIMIT = 56 * 1024 * 1024

NA_QROWS = 4
NA_KROWS = 12


def _bdot(a, b):
    return jnp.dot(a.astype(BF16), b.astype(BF16), preferred_element_type=F32)


def _bdot_nt(a, b):
    return lax.dot_general(a.astype(BF16), b.astype(BF16), (((1,), (1,)), ((), ())), preferred_element_type=F32)


def _bdot_tn(a, b):
    return lax.dot_general(a.astype(BF16), b.astype(BF16), (((0,), (0,)), ((), ())), preferred_element_type=F32)


def _split3(x):
    hi = x.astype(BF16)
    r = x - hi.astype(F32)
    mid = r.astype(BF16)
    lo = (r - mid.astype(F32)).astype(BF16)
    return hi, mid, lo


def _dot_mask_f32(t, x):
    hi, mid, lo = _split3(x)
    d = lambda p: jnp.dot(t, p, preferred_element_type=F32)
    return d(hi) + d(mid) + d(lo)


def _dot3(a, b):
    ah = a.astype(BF16)
    al = (a - ah.astype(F32)).astype(BF16)
    bh = b.astype(BF16)
    bl = (b - bh.astype(F32)).astype(BF16)
    d = lambda p, q: jnp.dot(p, q, preferred_element_type=F32)
    return d(ah, bh) + d(ah, bl) + d(al, bh)


def _sigmoid(x):
    return 1.0 / (1.0 + jnp.exp(-x))


def _silu(x):
    return x * _sigmoid(x)


def _softplus(x):
    return jnp.maximum(x, 0.0) + jnp.log(1.0 + jnp.exp(-jnp.abs(x)))


def _params(*sem):
    return pltpu.CompilerParams(dimension_semantics=sem, vmem_limit_bytes=VMEM_LIMIT)


def _adaln_kernel(c_ref, w_ref, b_ref, o_ref):
    o_ref[...] = _bdot(_silu(c_ref[...]), w_ref[...]) + b_ref[...]


def _adaln(cvec8, w_ada, b_ada):
    return pl.pallas_call(
        _adaln_kernel,
        grid=(DEPTH, 6),
        in_specs=[pl.BlockSpec((8, D_MODEL), lambda l, j: (0, 0)),
                  pl.BlockSpec((None, D_MODEL, D_MODEL), lambda l, j: (l, 0, j)),
                  pl.BlockSpec((None, None, 1, D_MODEL), lambda l, j: (l, j, 0, 0))],
        out_specs=pl.BlockSpec((None, None, 8, D_MODEL), lambda l, j: (l, j, 0, 0)),
        out_shape=jax.ShapeDtypeStruct((DEPTH, 6, 8, D_MODEL), F32),
        compiler_params=_params("arbitrary", "arbitrary"),
        name="adaln",
    )(cvec8, w_ada, b_ada.reshape(DEPTH, 6, 1, D_MODEL))


def _mod_row(tile, tiles_per_seq, latent):
    return (1 + tile // tiles_per_seq) if latent else 0


def _norm_mod(x, gain, shift, scale):
    y = x * lax.rsqrt(jnp.mean(x * x, axis=-1, keepdims=True) + EPS) * gain
    return y * (1.0 + scale) + shift


def _inproj_kernel(x_ref, mod_ref, n1_ref, w_ref, o_ref, osm_ref, *, tiles_per_seq, latent, tn):
    r = _mod_row(pl.program_id(0), tiles_per_seq, latent)
    h = _norm_mod(x_ref[...], n1_ref[...], mod_ref[0, pl.ds(r, 1), :], mod_ref[1, pl.ds(r, 1), :]).astype(BF16)
    for n in range(OFF_SMALL // tn):
        cols = slice(n * tn, (n + 1) * tn)
        o_ref[:, cols] = jnp.dot(h, w_ref[:, cols], preferred_element_type=F32).astype(BF16)
    osm_ref[...] = jnp.dot(h, w_ref[:, OFF_SMALL:], preferred_element_type=F32)


def _resident(stacked, layer):
    shape = stacked.shape[1:]
    zeros = (0,) * len(shape)
    return pl.BlockSpec((None,) + shape, lambda *g: (layer,) + zeros, pipeline_mode=pl.Buffered(1))


def _inproj(x, mod_l, n1, w_in_p, layer, seq_len, latent, tm=512, tn=2048):
    rows = x.shape[0]
    kern = functools.partial(_inproj_kernel, tiles_per_seq=max(seq_len // tm, 1), latent=latent, tn=tn)
    return pl.pallas_call(
        kern,
        grid=(rows // tm,),
        in_specs=[pl.BlockSpec((tm, D_MODEL), lambda m: (m, 0)),
                  pl.BlockSpec((6, 8, D_MODEL), lambda m: (0, 0, 0)),
                  pl.BlockSpec((1, D_MODEL), lambda m: (0, 0)),
                  _resident(w_in_p, layer)],
        out_specs=[pl.BlockSpec((tm, OFF_SMALL), lambda m: (m, 0)),
                   pl.BlockSpec((tm, PROJ_PAD - OFF_SMALL), lambda m: (m, 0))],
        out_shape=[jax.ShapeDtypeStruct((rows, OFF_SMALL), BF16),
                   jax.ShapeDtypeStruct((rows, PROJ_PAD - OFF_SMALL), F32)],
        compiler_params=_params("arbitrary"),
        name="inproj",
    )(x, mod_l, n1, w_in_p)


def _head_rms(x, gain):
    return x * lax.rsqrt(jnp.mean(x * x, axis=-1, keepdims=True) + EPS) * gain


def _head_slices():
    return [slice(h * NA_HEAD_DIM, (h + 1) * NA_HEAD_DIM) for h in range(NA_HEADS)]


def _ctx_attn_kernel(*refs, has_prev, layer):
    q_ref, k_ref, v_ref, qg_ref, kg_ref = refs[:5]
    o_ref, ka_ref, va_ref = refs[-3:]
    ka_ref = _own_slab(ka_ref, layer, not has_prev)
    va_ref = _own_slab(va_ref, layer, not has_prev)
    hs = _head_slices()
    qn = [_head_rms(q_ref[:, sl].astype(F32), qg_ref[...]) * (NA_HEAD_DIM ** -0.5) for sl in hs]
    kn = [_head_rms(k_ref[:, sl].astype(F32), kg_ref[...]) for sl in hs]
    for sl, k in zip(hs, kn):
        ka_ref[:, sl] = k
    s = [_bdot_nt(q, k) for q, k in zip(qn, kn)]
    p = [jnp.exp(x - jnp.max(x, axis=-1, keepdims=True)) for x in s]
    pv = [_bdot(x, v_ref[:, sl]) for x, sl in zip(p, hs)]
    for sl, x, y in zip(hs, p, pv):
        o_ref[:, sl] = (y / jnp.sum(x, axis=-1, keepdims=True)).astype(BF16)
    va_ref[...] = v_ref[...].astype(F32)


def _layer_slab(shape_tail, index_tail, layer, first):
    if first:
        return pl.BlockSpec((None, DEPTH) + shape_tail, lambda b, *g: (b, 0) + index_tail(*g))
    return pl.BlockSpec((None, None) + shape_tail, lambda b, *g: (b, layer) + index_tail(*g))


def _own_slab(ref, layer, first):
    if not first:
        return ref
    for other in range(DEPTH):
        if other != layer:
            ref[other] = jnp.zeros(ref.shape[1:], ref.dtype)
    return ref.at[layer]


def _ctx_attn(p_ctx, qg, kg, prev_kv, layer, batch, seq):
    rows = batch * seq
    blk = lambda j: pl.BlockSpec((seq, 512), lambda b: (b, j))
    cache = jax.ShapeDtypeStruct((batch, DEPTH, seq, 512), F32)
    cache_spec = _layer_slab((seq, 512), lambda: (0, 0), layer, prev_kv is None)
    in_specs = [blk(OFF_NA_Q // 512), blk(OFF_NA_K // 512), blk(OFF_NA_V // 512),
                pl.BlockSpec((1, NA_HEAD_DIM), lambda b: (0, 0)),
                pl.BlockSpec((1, NA_HEAD_DIM), lambda b: (0, 0))]
    args = [p_ctx, p_ctx, p_ctx, qg, kg]
    aliases = {}
    if prev_kv is not None:
        in_specs += [pl.BlockSpec(memory_space=pl.ANY)] * 2
        args += list(prev_kv)
        aliases = {5: 1, 6: 2}
    return pl.pallas_call(
        functools.partial(_ctx_attn_kernel, has_prev=prev_kv is not None, layer=layer),
        grid=(batch,),
        in_specs=in_specs,
        out_specs=[blk(0), cache_spec, cache_spec],
        out_shape=[jax.ShapeDtypeStruct((rows, 512), BF16), cache, cache],
        input_output_aliases=aliases,
        compiler_params=_params("arbitrary"),
        name="ctx_attn",
    )(*args)


NA_NDR = 2 * NA_WIN_H - 1


def _na_bias_tiles(rpb):
    qc = np.arange(GRID_W)
    cs = np.clip(qc - NA_WIN_W // 2, 0, GRID_W - NA_WIN_W)
    col_ok = (qc[None, :] >= cs[:, None]) & (qc[None, :] < cs[:, None] + NA_WIN_W)
    nd = 2 * NA_WIN_W - 1
    dc = np.clip(qc[None, :] - qc[:, None] + NA_WIN_W - 1, 0, nd - 1)
    sel = (np.arange(nd)[:, None, None] == dc[None]).astype(np.float32).reshape(nd, GRID_W * GRID_W)
    t = jnp.einsum('lhrd,dn->lhrn', rpb.astype(F32), jnp.asarray(sel), precision=lax.Precision.HIGHEST)
    t = jnp.where(col_ok, t.reshape(rpb.shape[:3] + (GRID_W, GRID_W)), NEG)
    return jnp.concatenate([t, jnp.full(t.shape[:2] + (1, GRID_W, GRID_W), NEG, F32)], axis=2)


def _lat_attn_kernel(q_ref, k_ref, v_ref, ck_ref, cv_ref, tile_ref, qg_ref, kg_ref, o_ref, kn_scr, bias_scr, *, rows):
    m = pl.program_id(1)
    nblk = rows // NA_QROWS
    hs = _head_slices()

    @pl.when(m == 0)
    def _():
        for sl in hs:
            kn_scr[:, sl] = _head_rms(k_ref[:, sl].astype(F32), kg_ref[...]).astype(BF16)

    krow0 = jnp.clip(NA_QROWS * m - NA_WIN_H // 2, 0, rows - NA_KROWS)

    @pl.when((m == 0) | (m == 1) | (m == nblk - 1))
    def _():
        for i in range(NA_QROWS):
            r = NA_QROWS * m + i
            rs = jnp.clip(r - NA_WIN_H // 2, 0, rows - NA_WIN_H)
            for j in range(NA_KROWS):
                kr = krow0 + j
                idx = jnp.where((kr >= rs) & (kr < rs + NA_WIN_H), kr - r + NA_WIN_H - 1, NA_NDR)
                for h in range(NA_HEADS):
                    bias_scr[h, i * GRID_W:(i + 1) * GRID_W, j * GRID_W:(j + 1) * GRID_W] = tile_ref[h, idx]

    kstart = pl.multiple_of(krow0 * GRID_W, 256)
    nk = NA_KROWS * GRID_W
    qn = [(_head_rms(q_ref[:, sl].astype(F32), qg_ref[...]) * (NA_HEAD_DIM ** -0.5)).astype(BF16) for sl in hs]
    s_loc = [_bdot_nt(q, kn_scr[pl.ds(kstart, nk), sl]) + bias_scr[h] for h, (q, sl) in enumerate(zip(qn, hs))]
    s_ctx = [_bdot_nt(q, ck_ref[:, sl]) for q, sl in zip(qn, hs)]
    mx = [jnp.maximum(jnp.max(a, axis=-1, keepdims=True), jnp.max(c, axis=-1, keepdims=True))
          for a, c in zip(s_loc, s_ctx)]
    p_loc = [jnp.exp(a - x) for a, x in zip(s_loc, mx)]
    p_ctx = [jnp.exp(c - x) for c, x in zip(s_ctx, mx)]
    o_loc = [_bdot(p, v_ref[pl.ds(kstart, nk), sl]) for p, sl in zip(p_loc, hs)]
    o_ctx = [_bdot(p, cv_ref[:, sl]) for p, sl in zip(p_ctx, hs)]
    for h, sl in enumerate(hs):
        den = jnp.sum(p_loc[h], axis=-1, keepdims=True) + jnp.sum(p_ctx[h], axis=-1, keepdims=True)
        o_ref[:, sl] = ((o_loc[h] + o_ctx[h]) / den).astype(BF16)


def _lat_attn(p_lat, cache_k, cache_v, tiles, qg, kg, layer, batch, seq):
    rows = seq // GRID_W
    nblk = rows // NA_QROWS
    qtok = NA_QROWS * GRID_W
    past = cache_k.shape[2]
    cache_spec = pl.BlockSpec((None, None, past, 512), lambda b, m: (b, layer, 0, 0))
    return pl.pallas_call(
        functools.partial(_lat_attn_kernel, rows=rows),
        grid=(batch, nblk),
        in_specs=[pl.BlockSpec((qtok, 512), lambda b, m: (b * nblk + m, OFF_NA_Q // 512)),
                  pl.BlockSpec((seq, 512), lambda b, m: (b, OFF_NA_K // 512)),
                  pl.BlockSpec((seq, 512), lambda b, m: (b, OFF_NA_V // 512)),
                  cache_spec, cache_spec,
                  pl.BlockSpec((None, NA_HEADS, NA_NDR + 1, GRID_W, GRID_W), lambda b, m: (layer, 0, 0, 0, 0)),
                  pl.BlockSpec((1, NA_HEAD_DIM), lambda b, m: (0, 0)),
                  pl.BlockSpec((1, NA_HEAD_DIM), lambda b, m: (0, 0))],
        out_specs=pl.BlockSpec((qtok, 512), lambda b, m: (b * nblk + m, 0)),
        out_shape=jax.ShapeDtypeStruct((batch * seq, 512), BF16),
        scratch_shapes=[pltpu.VMEM((seq, 512), BF16),
                        pltpu.VMEM((NA_HEADS, qtok, NA_KROWS * GRID_W), F32)],
        compiler_params=_params("arbitrary", "arbitrary"),
        name="lat_attn",
    )(p_lat, p_lat, p_lat, cache_k, cache_v, tiles, qg, kg)


def _group_masks(chunk):
    r = lax.broadcasted_iota(jnp.int32, (GROUP, GROUP), 0)
    c = lax.broadcasted_iota(jnp.int32, (GROUP, GROUP), 1)
    same = (r // chunk) == (c // chunk)
    return (same & (c <= r), same & (c >= r), same & (c < r), same & (c > r), same)


def _halving_masks(chunk, lower):
    r = lax.broadcasted_iota(jnp.int32, (GROUP, GROUP), 0)
    c = lax.broadcasted_iota(jnp.int32, (GROUP, GROUP), 1)
    if not lower:
        r, c = c, r
    masks = []
    s = 1
    while s < chunk:
        m = ((r // (2 * s)) == (c // (2 * s))) & ((r // s) % 2 == 1) & ((c // s) % 2 == 0)
        masks.append(jnp.where(m, 1.0, 0.0).astype(BF16))
        s *= 2
    return masks


def _group_eye():
    r = lax.broadcasted_iota(jnp.int32, (GROUP, GROUP), 0)
    c = lax.broadcasted_iota(jnp.int32, (GROUP, GROUP), 1)
    return jnp.where(r == c, 1.0, 0.0).astype(BF16)


def _sel_col(a, lane):
    idx = lax.broadcasted_iota(jnp.int32, a.shape, 1)
    return jnp.sum(jnp.where(idx == lane, a, 0.0), axis=1, keepdims=True)


def _sel_row(a, row):
    idx = lax.broadcasted_iota(jnp.int32, a.shape, 0)
    return jnp.sum(jnp.where(idx == row, a, 0.0), axis=0, keepdims=True)


def _conv3_silu(x, w):
    t = x.shape[0]
    row = lax.broadcasted_iota(jnp.int32, x.shape, 0)
    xm = jnp.where(row == 0, 0.0, pltpu.roll(x, 1, 0))
    xp = jnp.where(row == t - 1, 0.0, pltpu.roll(x, t - 1, 0))
    return _silu(xm * w[0:1, :] + x * w[1:2, :] + xp * w[2:3, :])


def _deltanet_kernel(*refs, hp, has_s0, emit_state, has_prev, layer):
    (q_ref, k_ref, v_ref, gate_ref, sm_ref, cq_ref, ck_ref, cv_ref, alog_ref, dtb_ref, og_ref), refs = refs[:11], refs[11:]
    if has_s0:
        s0_ref, refs = refs[0], refs[1:]
    if has_prev:
        refs = refs[1:]
    o_ref, refs = refs[0], refs[1:]
    if emit_state:
        sfin_ref, refs = refs[0], refs[1:]
    qn_scr, kn_scr, vc_scr, of_scr, ob_scr, s_scr = refs

    head0 = pl.program_id(1) * hp
    seq = q_ref.shape[0]
    hd = DN_HEAD_DIM
    ngroups = seq // GROUP
    nchunk = GROUP // DN_CHUNK

    q = _conv3_silu(q_ref[...].astype(F32), cq_ref[...])
    k = _conv3_silu(k_ref[...].astype(F32), ck_ref[...])
    vc_scr[...] = _conv3_silu(v_ref[...].astype(F32), cv_ref[...])
    for j in range(hp):
        sl = slice(j * hd, (j + 1) * hd)
        qj, kj = q[:, sl], k[:, sl]
        qn_scr[:, sl] = qj * lax.rsqrt(jnp.sum(qj * qj, axis=-1, keepdims=True) + EPS) * (hd ** -0.5)
        kn_scr[:, sl] = kj * lax.rsqrt(jnp.sum(kj * kj, axis=-1, keepdims=True) + EPS)

    if has_s0:
        s_scr[...] = s0_ref[...]
    else:
        s_scr[...] = jnp.zeros_like(s_scr)

    m_fi, m_bi, m_fs, m_bs, m_same = _group_masks(DN_CHUNK)
    incl = (m_fi, m_bi)
    strict = (m_fs, m_bs)
    tri = (m_fi.astype(BF16), m_bi.astype(BF16))
    ones_bd = m_same.astype(BF16)
    halving = (_halving_masks(DN_CHUNK, True), _halving_masks(DN_CHUNK, False))

    eye = _group_eye()
    chains = [(d, j) for d in range(2) for j in range(hp)]
    nc = len(chains)

    def group_step(t, carry):
        shared = []
        for d in range(2):
            gi = t if d == 0 else ngroups - 1 - t
            r0 = pl.multiple_of(gi * GROUP, GROUP)
            sm = sm_ref[pl.ds(r0, GROUP), :]
            g128 = -jnp.exp(alog_ref[...]) * _softplus(sm + dtb_ref[...])
            gcum = _dot_mask_f32(tri[d], g128)
            gtot = _dot_mask_f32(ones_bd, g128)
            shared.append((r0, gcum, gtot, gcum.T, _sigmoid(sm)))

        qg, kg, vg, beta, gc_col, gt_col, dec = [], [], [], [], [], [], []
        for d, j in chains:
            r0, gcum, gtot, gcum_t, beta128 = shared[d]
            sl = slice(j * hd, (j + 1) * hd)
            lane_g = SMALL_DECAY + DN_HEADS * d + head0 + j
            gc_col.append(_sel_col(gcum, lane_g))
            gt_col.append(_sel_col(gtot, lane_g))
            beta.append(_sel_col(beta128, SMALL_BETA + DN_HEADS * d + head0 + j))
            dec.append(jnp.exp(jnp.where(incl[d], gc_col[-1] - _sel_row(gcum_t, lane_g), NEG)))
            qg.append(qn_scr[pl.ds(r0, GROUP), sl])
            kg.append(kn_scr[pl.ds(r0, GROUP), sl])
            vg.append(vc_scr[pl.ds(r0, GROUP), sl])
        rng = range(nc)
        kk = [_bdot_nt(kg[i], kg[i]) for i in rng]
        qk = [_bdot_nt(qg[i], kg[i]) for i in rng]
        low = [jnp.where(strict[chains[i][0]], beta[i] * kk[i] * dec[i], 0.0).astype(BF16) for i in rng]
        attn = [(qk[i] * dec[i]).astype(BF16) for i in rng]
        e_gc = [jnp.exp(gc_col[i]) for i in rng]
        x = [jnp.concatenate([vg[i] * beta[i], kg[i] * (beta[i] * e_gc[i])], axis=1).astype(BF16) for i in rng]
        inv = [eye - low[i] * halving[chains[i][0]][0] for i in rng]
        for lvl in range(1, len(halving[0])):
            tmp = [_bdot(inv[i], low[i] * halving[chains[i][0]][lvl]).astype(BF16) for i in rng]
            upd = [_bdot(tmp[i], inv[i]).astype(BF16) for i in rng]
            inv = [inv[i] - upd[i] for i in rng]
        x = [_bdot(inv[i], x[i]) for i in rng]
        w = [x[i][:, hd:].astype(BF16) for i in rng]
        qd = [(qg[i] * e_gc[i]).astype(BF16) for i in rng]
        kd = [(kg[i] * jnp.exp(gt_col[i] - gc_col[i])).astype(BF16) for i in rng]
        s = [s_scr[d, j] for d, j in chains]
        o_parts = [[None] * nchunk for _ in rng]
        for cc in range(nchunk):
            rs = []
            for d, _ in chains:
                c = cc if d == 0 else nchunk - 1 - cc
                rs.append(slice(c * DN_CHUNK, (c + 1) * DN_CHUNK))
            ws = [_bdot(w[i][rs[i]], s[i]) for i in rng]
            qs = [_bdot(qd[i][rs[i]], s[i]) for i in rng]
            v_new = [x[i][rs[i], :hd] - ws[i] for i in rng]
            av = [_bdot(attn[i][rs[i], rs[i]], v_new[i]) for i in rng]
            kv = [_bdot_tn(kd[i][rs[i]], v_new[i]) for i in rng]
            for i in rng:
                o_parts[i][rs[i].start // DN_CHUNK] = qs[i] + av[i]
                s[i] = s[i] * jnp.exp(gt_col[i][rs[i].start:rs[i].start + 1, :]) + kv[i]
        for i, (d, j) in enumerate(chains):
            s_scr[d, j] = s[i]
            o_scr = of_scr if d == 0 else ob_scr
            o_scr[pl.ds(shared[d][0], GROUP), j * hd:(j + 1) * hd] = jnp.concatenate(o_parts[i], axis=0)
        return carry

    if ngroups == 1:
        group_step(0, 0)
    else:
        lax.fori_loop(0, ngroups, group_step, 0)

    for j in range(hp):
        sl = slice(j * hd, (j + 1) * hd)
        o = of_scr[:, sl] + ob_scr[:, sl]
        o = o * lax.rsqrt(jnp.mean(o * o, axis=-1, keepdims=True) + EPS) * og_ref[...]
        o_ref[:, sl] = (o * _silu(gate_ref[:, sl].astype(F32))).astype(BF16)
    if emit_state:
        _own_slab(sfin_ref, layer, not has_prev)[...] = s_scr[...]


def _deltanet(p, psm, conv_w, alog128, dtb128, out_g, s0, prev_state, layer, batch, seq, emit_state, hp):
    rows = batch * seq
    hd = DN_HEAD_DIM
    w = hp * hd
    col = lambda off: pl.BlockSpec((seq, w), lambda b, h: (b, off // w + h))
    cw = lambda part: pl.BlockSpec((3, w), lambda b, h: (0, part * (DN_HEADS // hp) + h))
    row128 = pl.BlockSpec((1, 128), lambda b, h: (0, 0))
    in_specs = [col(OFF_DN_Q), col(OFF_DN_K), col(OFF_DN_V), col(OFF_DN_GATE),
                pl.BlockSpec((seq, 128), lambda b, h: (b, 0)),
                cw(0), cw(1), cw(2), row128, row128, row128]
    args = [p, p, p, p, psm, conv_w, conv_w, conv_w, alog128, dtb128, out_g]
    if s0 is not None:
        in_specs.append(pl.BlockSpec((None, None, 2, hp, hd, hd), lambda b, h: (b, layer, 0, h, 0, 0)))
        args.append(s0)
    out_specs = [pl.BlockSpec((seq, w), lambda b, h: (b, h))]
    out_shape = [jax.ShapeDtypeStruct((rows, DN_HEADS * hd), BF16)]
    aliases = {}
    if emit_state:
        out_specs.append(_layer_slab((2, hp, hd, hd), lambda h: (0, h, 0, 0), layer, prev_state is None))
        out_shape.append(jax.ShapeDtypeStruct((batch, DEPTH, 2, DN_HEADS, hd, hd), F32))
        if prev_state is not None:
            in_specs.append(pl.BlockSpec(memory_space=pl.ANY))
            args.append(prev_state)
            aliases = {len(args) - 1: 1}
    res = pl.pallas_call(
        functools.partial(_deltanet_kernel, hp=hp, has_s0=s0 is not None, emit_state=emit_state,
                          has_prev=prev_state is not None, layer=layer),
        grid=(batch, DN_HEADS // hp),
        in_specs=in_specs,
        out_specs=out_specs,
        out_shape=out_shape,
        input_output_aliases=aliases,
        scratch_shapes=[pltpu.VMEM((seq, w), F32)] * 5 + [pltpu.VMEM((2, hp, hd, hd), F32)],
        compiler_params=_params("arbitrary", "arbitrary"),
        name="deltanet",
    )(*args)
    return res if emit_state else (res[0], None)


def _transpose_f32(x):
    n = x.shape[1]
    r = lax.broadcasted_iota(jnp.int32, (n, n), 0)
    c = lax.broadcasted_iota(jnp.int32, (n, n), 1)
    eye = jnp.where(r == c, 1.0, 0.0).astype(BF16)
    d = lambda p: lax.dot_general(eye, p, (((1,), (1,)), ((), ())), preferred_element_type=F32)
    hi, mid, lo = _split3(x)
    return d(hi) + d(mid) + d(lo)


def _gla_kernel(*refs, hp, has_s0, emit_state, has_prev, layer):
    (qk_ref, v_ref, gate_ref, sm_ref, wg_ref, bg_ref, og_ref), refs = refs[:7], refs[7:]
    if has_s0:
        s0_ref, refs = refs[0], refs[1:]
    if has_prev:
        refs = refs[1:]
    o_ref, refs = refs[0], refs[1:]
    if emit_state:
        sfin_ref, refs = refs[0], refs[1:]
    of_scr, ob_scr, s_scr = refs

    seq = qk_ref.shape[0]
    ngroups = seq // GROUP
    nchunk = GROUP // GLA_CHUNK
    dk = GLA_KEY_DIM

    for d in range(2):
        for j in range(hp):
            if has_s0:
                s_scr[d, j] = _transpose_f32(s0_ref[d, j])
            else:
                s_scr[d, j] = jnp.zeros(s_scr.shape[2:], F32)

    m_fi, m_bi, _, _, m_same = _group_masks(GLA_CHUNK)
    incl = (m_fi, m_bi)
    tri = (m_fi.astype(BF16), m_bi.astype(BF16))
    ones_bd = m_same.astype(BF16)

    chains = [(d, j) for d in range(2) for j in range(hp)]
    rng = range(len(chains))

    def group_step(t, carry):
        shared = []
        for d in range(2):
            gi = t if d == 0 else ngroups - 1 - t
            r0 = pl.multiple_of(gi * GROUP, GROUP)
            gl = _bdot(sm_ref[pl.ds(r0, GROUP), :], wg_ref[...]) + bg_ref[...]
            log_a = -_softplus(-gl) / GLA_TAU
            b_all = _dot_mask_f32(tri[d], log_a)
            btot_all = _dot_mask_f32(ones_bd, log_a)
            shared.append((r0, b_all, btot_all))
        b, btot, kg, vg, qe = [], [], [], [], []
        for d, j in chains:
            r0, b_all, btot_all = shared[d]
            sl = slice(j * 128, (j + 1) * 128)
            lanes = slice(j * 128 + d * dk, j * 128 + (d + 1) * dk)
            b.append(b_all[:, lanes])
            btot.append(btot_all[:, lanes])
            qk = qk_ref[pl.ds(r0, GROUP), sl].astype(F32)
            kg.append(qk[:, dk:])
            vg.append(v_ref[pl.ds(r0, GROUP), sl])
            qe.append((qk[:, :dk] * (dk ** -0.5) * jnp.exp(b[-1])).astype(BF16))
        vg = [v.astype(BF16) for v in vg]
        attn = [jnp.where(incl[chains[i][0]], _bdot_nt(qe[i], kg[i] * jnp.exp(-b[i])), 0.0).astype(BF16)
                for i in rng]
        o_in = [_bdot(attn[i], vg[i]) for i in rng]
        kd = [(kg[i] * jnp.exp(btot[i] - b[i])).astype(BF16) for i in rng]
        st = [s_scr[d, j] for d, j in chains]
        o_parts = [[None] * nchunk for _ in rng]
        for cc in range(nchunk):
            rs = []
            for d, _ in chains:
                c = cc if d == 0 else nchunk - 1 - cc
                rs.append(slice(c * GLA_CHUNK, (c + 1) * GLA_CHUNK))
            qs = [_bdot_nt(qe[i][rs[i]], st[i]) for i in rng]
            vk = [_bdot_tn(vg[i][rs[i]], kd[i][rs[i]]) for i in rng]
            for i in rng:
                o_parts[i][rs[i].start // GLA_CHUNK] = o_in[i][rs[i]] + qs[i]
                st[i] = st[i] * jnp.exp(btot[i][rs[i].start:rs[i].start + 1, :]) + vk[i]
        for i, (d, j) in enumerate(chains):
            s_scr[d, j] = st[i]
            o_scr = of_scr if d == 0 else ob_scr
            o_scr[pl.ds(shared[d][0], GROUP), j * 128:(j + 1) * 128] = jnp.concatenate(o_parts[i], axis=0)
        return carry

    if ngroups == 1:
        group_step(0, 0)
    else:
        lax.fori_loop(0, ngroups, group_step, 0)

    for j in range(hp):
        sl = slice(j * 128, (j + 1) * 128)
        o = of_scr[:, sl] + ob_scr[:, sl]
        o = o * lax.rsqrt(jnp.mean(o * o, axis=-1, keepdims=True) + EPS) * og_ref[...]
        o_ref[:, sl] = (o * _silu(gate_ref[:, sl].astype(F32))).astype(BF16)
    if emit_state:
        sfin = _own_slab(sfin_ref, layer, not has_prev)
        for d in range(2):
            for j in range(hp):
                sfin[d, j] = _transpose_f32(s_scr[d, j])


def _gla(p, psm, wg, bg, out_g, s0, prev_state, layer, batch, seq, emit_state, hp):
    rows = batch * seq
    dv, dk = GLA_VAL_DIM, GLA_KEY_DIM
    w = hp * 128
    col = lambda off: pl.BlockSpec((seq, w), lambda b, h: (b, off // w + h))
    in_specs = [col(OFF_GLA_QK), col(OFF_GLA_V), col(OFF_GLA_GATE),
                pl.BlockSpec((seq, 128), lambda b, h: (b, 0)),
                pl.BlockSpec((128, w), lambda b, h: (0, h)),
                pl.BlockSpec((1, w), lambda b, h: (0, h)),
                pl.BlockSpec((1, 128), lambda b, h: (0, 0))]
    args = [p, p, p, psm, wg, bg, out_g]
    if s0 is not None:
        in_specs.append(pl.BlockSpec((None, None, 2, hp, dk, dv), lambda b, h: (b, layer, 0, h, 0, 0)))
        args.append(s0)
    out_specs = [pl.BlockSpec((seq, w), lambda b, h: (b, h))]
    out_shape = [jax.ShapeDtypeStruct((rows, GLA_HEADS * dv), BF16)]
    aliases = {}
    if emit_state:
        out_specs.append(_layer_slab((2, hp, dk, dv), lambda h: (0, h, 0, 0), layer, prev_state is None))
        out_shape.append(jax.ShapeDtypeStruct((batch, DEPTH, 2, GLA_HEADS, dk, dv), F32))
        if prev_state is not None:
            in_specs.append(pl.BlockSpec(memory_space=pl.ANY))
            args.append(prev_state)
            aliases = {len(args) - 1: 1}
    res = pl.pallas_call(
        functools.partial(_gla_kernel, hp=hp, has_s0=s0 is not None, emit_state=emit_state,
                          has_prev=prev_state is not None, layer=layer),
        grid=(batch, GLA_HEADS // hp),
        in_specs=in_specs,
        out_specs=out_specs,
        out_shape=out_shape,
        input_output_aliases=aliases,
        scratch_shapes=[pltpu.VMEM((seq, w), F32)] * 2 + [pltpu.VMEM((2, hp, dv, dk), F32)],
        compiler_params=_params("arbitrary", "arbitrary"),
        name="gla",
    )(*args)
    return res if emit_state else (res[0], None)


def _merge_kernel(x_ref, oa_ref, ob_ref, oc_ref, ma_ref, mb_ref, mc_ref, mod_ref, wbr_ref, wo_ref, o_ref,
                  *, tiles_per_seq, latent):
    r = _mod_row(pl.program_id(0), tiles_per_seq, latent)
    merged = (_sigmoid(ma_ref[...].astype(F32)) * _bdot(oa_ref[...], wbr_ref[0])
              + _sigmoid(mb_ref[...].astype(F32)) * _bdot(ob_ref[...], wbr_ref[1])
              + _sigmoid(mc_ref[...].astype(F32)) * _bdot(oc_ref[...], wbr_ref[2]))
    o_ref[...] = x_ref[...] + mod_ref[2, pl.ds(r, 1), :] * _bdot(merged, wo_ref[...])


def _merge(x, o_a, o_b, o_c, p, mod_l, w_br, w_o, layer, seq_len, latent, tm=512):
    rows = x.shape[0]
    row_blk = lambda w: pl.BlockSpec((tm, w), lambda m: (m, 0))
    gate_blk = lambda j: pl.BlockSpec((tm, D_MODEL), lambda m: (m, OFF_M // D_MODEL + j))
    return pl.pallas_call(
        functools.partial(_merge_kernel, tiles_per_seq=max(seq_len // tm, 1), latent=latent),
        grid=(rows // tm,),
        in_specs=[row_blk(D_MODEL), row_blk(512), row_blk(512), row_blk(512),
                  gate_blk(0), gate_blk(1), gate_blk(2),
                  pl.BlockSpec((6, 8, D_MODEL), lambda m: (0, 0, 0)),
                  _resident(w_br, layer), _resident(w_o, layer)],
        out_specs=row_blk(D_MODEL),
        out_shape=jax.ShapeDtypeStruct((rows, D_MODEL), F32),
        compiler_params=_params("arbitrary"),
        name="merge",
    )(x, o_a, o_b, o_c, p, p, p, mod_l, w_br, w_o)


FFN_HALO = 16


def _ffn_kernel(x_ref, xp_ref, xn_ref, mod_ref, n2_ref, wu_ref, fc_ref, wd_ref, o_ref, h_scr, act_scr,
                *, tm, tn, seq_len, tiles_per_seq, latent):
    r = _mod_row(pl.program_id(0), tiles_per_seq, latent)
    shift = mod_ref[3, pl.ds(r, 1), :]
    scale = mod_ref[4, pl.ds(r, 1), :]
    nm = lambda x: _norm_mod(x, n2_ref[...], shift, scale)
    row0 = pl.program_id(0) * tm
    h_scr[0:FFN_HALO, :] = jnp.where(row0 % seq_len == 0, 0.0, nm(xp_ref[...])).astype(BF16)
    h_scr[FFN_HALO:FFN_HALO + tm, :] = nm(x_ref[...]).astype(BF16)
    h_scr[FFN_HALO + tm:, :] = jnp.where((row0 + tm) % seq_len == 0, 0.0, nm(xn_ref[...])).astype(BF16)

    ext = tm + 2 * FFN_HALO
    inner = slice(FFN_HALO, FFN_HALO + tm)
    bounds = list(range(seq_len, tm, seq_len))
    fix = 16
    rid = lax.broadcasted_iota(jnp.int32, (2 * fix, 1), 0)

    def up(n):
        hh = h_scr[...]
        cv = slice(n * tn, (n + 1) * tn)
        cg = slice(D_FF + n * tn, D_FF + (n + 1) * tn)
        return (jnp.dot(hh, wu_ref[:, cv], preferred_element_type=F32),
                jnp.dot(hh, wu_ref[:, cg], preferred_element_type=F32))

    def taps(u):
        return pltpu.roll(u, 1, 0)[inner], u[inner], pltpu.roll(u, ext - 1, 0)[inner]

    def conv(t, f, rows=slice(None), keep_prev=None, keep_next=None):
        prev, mid, nxt = (a[rows] for a in t)
        if keep_prev is not None:
            prev = jnp.where(keep_prev, prev, 0.0)
            nxt = jnp.where(keep_next, nxt, 0.0)
        return prev * f[0:1, :] + mid * f[1:2, :] + nxt * f[2:3, :]

    nt = D_FF // tn
    u_next = up(0)
    for n in range(nt):
        u_val, u_gate = u_next
        if n + 1 < nt:
            u_next = up(n + 1)
        cols = slice(n * tn, (n + 1) * tn)
        f_val = fc_ref[:, cols]
        f_gate = fc_ref[:, D_FF + n * tn:D_FF + (n + 1) * tn]
        t_val, t_gate = taps(u_val), taps(u_gate)
        act_scr[:, cols] = (_silu(conv(t_gate, f_gate)) * conv(t_val, f_val)).astype(BF16)
        for b in bounds:
            rows = slice(b - fix, b + fix)
            kp, kn = rid != fix, rid != fix - 1
            act_scr[rows, cols] = (_silu(conv(t_gate, f_gate, rows, kp, kn))
                                   * conv(t_val, f_val, rows, kp, kn)).astype(BF16)
    down = jnp.dot(act_scr[...], wd_ref[...], preferred_element_type=F32)
    o_ref[...] = x_ref[...] + mod_ref[5, pl.ds(r, 1), :] * down


def _ffn(x, mod_l, n2, w_up, f_conv, w_dn, layer, seq_len, latent, tm=1024, tn=256):
    rows = x.shape[0]
    hb = tm // FFN_HALO
    last_hb = rows // FFN_HALO - 1
    kern = functools.partial(_ffn_kernel, tm=tm, tn=tn, seq_len=seq_len, tiles_per_seq=max(seq_len // tm, 1),
                             latent=latent)
    return pl.pallas_call(
        kern,
        grid=(rows // tm,),
        in_specs=[pl.BlockSpec((tm, D_MODEL), lambda m: (m, 0)),
                  pl.BlockSpec((FFN_HALO, D_MODEL), lambda m: (jnp.maximum(m * hb - 1, 0), 0)),
                  pl.BlockSpec((FFN_HALO, D_MODEL), lambda m: (jnp.minimum((m + 1) * hb, last_hb), 0)),
                  pl.BlockSpec((6, 8, D_MODEL), lambda m: (0, 0, 0)),
                  pl.BlockSpec((1, D_MODEL), lambda m: (0, 0)),
                  _resident(w_up, layer), _resident(f_conv, layer), _resident(w_dn, layer)],
        out_specs=pl.BlockSpec((tm, D_MODEL), lambda m: (m, 0)),
        out_shape=jax.ShapeDtypeStruct((rows, D_MODEL), F32),
        scratch_shapes=[pltpu.VMEM((tm + 2 * FFN_HALO, D_MODEL), BF16), pltpu.VMEM((tm, D_FF), BF16)],
        compiler_params=_params("arbitrary"),
        name="ffn",
    )(x, x, x, mod_l, n2, w_up, f_conv, w_dn)


def _w_in_moves():
    src = np.cumsum([0, 512, 512, 512, 512, 512, 512, 512, 8, 8, 256, 256, 512, 512, 32, 1024, 1024, 1024])
    dk = GLA_KEY_DIM
    moves = [(0, 0, int(src[7]))]
    for h in range(GLA_HEADS):
        moves.append((OFF_GLA_QK + 2 * dk * h, int(src[9]) + dk * h, dk))
        moves.append((OFF_GLA_QK + 2 * dk * h + dk, int(src[10]) + dk * h, dk))
    moves += [(OFF_GLA_V, int(src[11]), 512), (OFF_GLA_GATE, int(src[12]), 512), (OFF_M, int(src[14]), 3072),
              (OFF_SMALL + SMALL_BETA, int(src[7]), 8), (OFF_SMALL + SMALL_DECAY, int(src[8]), 8),
              (OFF_SMALL + SMALL_LR, int(src[13]), 32)]
    return moves


def _permute_kernel(w_ref, o_ref):
    used = SMALL_LR + 2 * GLA_GATE_RANK
    o_ref[:, OFF_SMALL + used:] = jnp.zeros((o_ref.shape[0], PROJ_PAD - OFF_SMALL - used), BF16)
    for dst, src, width in _w_in_moves():
        o_ref[:, dst:dst + width] = w_ref[:, src:src + width].astype(BF16)


def _permute_w_in(w_in, tr=128):
    proj = w_in.shape[-1]
    return pl.pallas_call(
        _permute_kernel,
        grid=(DEPTH, D_MODEL // tr),
        in_specs=[pl.BlockSpec((None, tr, proj), lambda l, r: (l, r, 0))],
        out_specs=pl.BlockSpec((None, tr, PROJ_PAD), lambda l, r: (l, r, 0)),
        out_shape=jax.ShapeDtypeStruct((DEPTH, D_MODEL, PROJ_PAD), BF16),
        compiler_params=_params("arbitrary", "arbitrary"),
        name="permute_w_in",
    )(w_in)


def _gla_gate_weights(gla_w, gla_b):
    r, dk = GLA_GATE_RANK, GLA_KEY_DIM
    w = gla_w.reshape(DEPTH, 2, r, GLA_HEADS, dk)
    wg = jnp.zeros((DEPTH, GLA_HEADS, 128, 2, dk), F32)
    for d in range(2):
        wg = wg.at[:, :, SMALL_LR + d * r:SMALL_LR + (d + 1) * r, d, :].set(jnp.transpose(w[:, d], (0, 2, 1, 3)))
    bg = jnp.transpose(gla_b.reshape(DEPTH, 2, GLA_HEADS, dk), (0, 2, 1, 3)).reshape(DEPTH, 1, GLA_HEADS * 2 * dk)
    wg = jnp.transpose(wg, (0, 2, 1, 3, 4)).reshape(DEPTH, 128, GLA_HEADS * 2 * dk)
    return wg.astype(BF16), bg.astype(F32)


def _lane_row(values, offset):
    n = values.shape[-1]
    return jnp.zeros((DEPTH, 1, 128), F32).at[:, 0, offset:offset + n].set(values.astype(F32))


def kernel(x_prompt, x_sample, cache_k, cache_v, state_dn, state_gla, c, c_ctx, w_ada, b_ada, norm1, w_in,
           na_q_norm, na_k_norm, na_rpb, dn_conv, dn_a_log, dn_dt_bias, dn_out_norm, gla_w_gate, gla_b_gate,
           gla_out_norm, w_branch, w_out, norm2, w_up, ffn_conv, w_down):
    batch, seq, _ = x_prompt.shape
    dbatch, dseq, _ = x_sample.shape
    past = cache_k.shape[2]

    cvec8 = jnp.concatenate([c_ctx[None, :], c, jnp.zeros((8 - 1 - dbatch, D_MODEL), F32)], axis=0)
    mod = _adaln(cvec8, w_ada, b_ada)

    w_in_p = _permute_w_in(w_in)
    w_br = w_branch.astype(BF16)
    w_o = w_out.astype(BF16)
    w_up_b = w_up.astype(BF16)
    w_dn_b = w_down.astype(BF16)
    tiles = _na_bias_tiles(na_rpb)
    wg, bg = _gla_gate_weights(gla_w_gate, gla_b_gate)
    alog128 = _lane_row(dn_a_log.reshape(DEPTH, 2 * DN_HEADS), SMALL_DECAY)
    dtb128 = _lane_row(dn_dt_bias.reshape(DEPTH, 2 * DN_HEADS), SMALL_DECAY)
    ck = cache_k.reshape(dbatch, DEPTH, past, NA_HEADS * NA_HEAD_DIM)
    cv = cache_v.reshape(dbatch, DEPTH, past, NA_HEADS * NA_HEAD_DIM)

    y_p = x_prompt.reshape(batch * seq, D_MODEL)
    y_s = x_sample.reshape(dbatch * dseq, D_MODEL)
    new_kv, new_dn, new_gla = None, None, None
    for l in range(DEPTH):
        qg, kg = na_q_norm[l][None, :], na_k_norm[l][None, :]
        n1, n2 = norm1[l][None, :], norm2[l][None, :]
        dn_g, gla_g = dn_out_norm[l][None, :], gla_out_norm[l][None, :]
        for latent in (False, True):
            x = y_s if latent else y_p
            b_, t_ = (dbatch, dseq) if latent else (batch, seq)
            p, psm = _inproj(x, mod[l], n1, w_in_p, l, t_, latent)
            if latent:
                o_a = _lat_attn(p, ck, cv, tiles, qg, kg, l, b_, t_)
                o_b, _ = _deltanet(p, psm, dn_conv[l], alog128[l], dtb128[l], dn_g, state_dn, None, l, b_, t_,
                                   False, 2)
                o_c, _ = _gla(p, psm, wg[l], bg[l], gla_g, state_gla, None, l, b_, t_, False, 2)
            else:
                o_a, k_all, v_all = _ctx_attn(p, qg, kg, new_kv, l, b_, t_)
                new_kv = (k_all, v_all)
                o_b, new_dn = _deltanet(p, psm, dn_conv[l], alog128[l], dtb128[l], dn_g, None, new_dn, l, b_, t_,
                                        True, DN_HEADS)
                o_c, new_gla = _gla(p, psm, wg[l], bg[l], gla_g, None, new_gla, l, b_, t_, True, GLA_HEADS)
            x = _merge(x, o_a, o_b, o_c, p, mod[l], w_br, w_o, l, t_, latent)
            x = _ffn(x, mod[l], n2, w_up_b, ffn_conv, w_dn_b, l, t_, latent)
            if latent:
                y_s = x
            else:
                y_p = x
    cache_shape = (batch, DEPTH, seq, NA_HEADS, NA_HEAD_DIM)
    return (y_p.reshape(batch, seq, D_MODEL), y_s.reshape(dbatch, dseq, D_MODEL),
            new_kv[0].reshape(cache_shape), new_kv[1].reshape(cache_shape), new_dn, new_gla)
```

```python
import functools

import numpy as np
import jax
import jax.numpy as jnp
from jax import lax
from jax.experimental import pallas as pl
from jax.experimental.pallas import tpu as pltpu

F32 = jnp.float32
BF16 = jnp.bfloat16

D_MODEL = 1024
DEPTH = 2
GRID_W = 64
EPS = 1e-6
NA_HEADS = 8
NA_HEAD_DIM = 64
NA_WIN_H = 8
NA_WIN_W = 16
DN_HEADS = 4
DN_HEAD_DIM = 128
DN_CHUNK = 64
GLA_HEADS = 4
GLA_KEY_DIM = 64
GLA_VAL_DIM = 128
GLA_GATE_RANK = 16
GLA_TAU = 16.0
GLA_CHUNK = 32
D_FF = 2816

OFF_NA_Q, OFF_NA_K, OFF_NA_V = 0, 512, 1024
OFF_DN_Q, OFF_DN_K, OFF_DN_V, OFF_DN_GATE = 1536, 2048, 2560, 3072
OFF_GLA_QK, OFF_GLA_V, OFF_GLA_GATE = 3584, 4096, 4608
OFF_M = 5120
OFF_SMALL = 8192
PROJ_PAD = 8320
SMALL_BETA, SMALL_DECAY, SMALL_LR = 0, 8, 16

GROUP = 256
NEG = -1e30
VMEM_LIMIT = 56 * 1024 * 1024

NA_QROWS = 4
NA_KROWS = 12


def _bdot(a, b):
    return jnp.dot(a.astype(BF16), b.astype(BF16), preferred_element_type=F32)


def _bdot_nt(a, b):
    return lax.dot_general(a.astype(BF16), b.astype(BF16), (((1,), (1,)), ((), ())), preferred_element_type=F32)


def _bdot_tn(a, b):
    return lax.dot_general(a.astype(BF16), b.astype(BF16), (((0,), (0,)), ((), ())), preferred_element_type=F32)


def _split3(x):
    hi = x.astype(BF16)
    r = x - hi.astype(F32)
    mid = r.astype(BF16)
    lo = (r - mid.astype(F32)).astype(BF16)
    return hi, mid, lo


def _dot_mask_f32(t, x):
    hi, mid, lo = _split3(x)
    d = lambda p: jnp.dot(t, p, preferred_element_type=F32)
    return d(hi) + d(mid) + d(lo)


def _dot3(a, b):
    ah = a.astype(BF16)
    al = (a - ah.astype(F32)).astype(BF16)
    bh = b.astype(BF16)
    bl = (b - bh.astype(F32)).astype(BF16)
    d = lambda p, q: jnp.dot(p, q, preferred_element_type=F32)
    return d(ah, bh) + d(ah, bl) + d(al, bh)


def _sigmoid(x):
    return 1.0 / (1.0 + jnp.exp(-x))


def _silu(x):
    return x * _sigmoid(x)


def _softplus(x):
    return jnp.maximum(x, 0.0) + jnp.log(1.0 + jnp.exp(-jnp.abs(x)))


def _params(*sem):
    return pltpu.CompilerParams(dimension_semantics=sem, vmem_limit_bytes=VMEM_LIMIT)


def _adaln_kernel(c_ref, w_ref, b_ref, o_ref):
    o_ref[...] = _bdot(_silu(c_ref[...]), w_ref[...]) + b_ref[...]


def _adaln(cvec8, w_ada, b_ada):
    return pl.pallas_call(
        _adaln_kernel,
        grid=(DEPTH, 6),
        in_specs=[pl.BlockSpec((8, D_MODEL), lambda l, j: (0, 0)),
                  pl.BlockSpec((None, D_MODEL, D_MODEL), lambda l, j: (l, 0, j)),
                  pl.BlockSpec((None, None, 1, D_MODEL), lambda l, j: (l, j, 0, 0))],
        out_specs=pl.BlockSpec((None, None, 8, D_MODEL), lambda l, j: (l, j, 0, 0)),
        out_shape=jax.ShapeDtypeStruct((DEPTH, 6, 8, D_MODEL), F32),
        compiler_params=_params("arbitrary", "arbitrary"),
        name="adaln",
    )(cvec8, w_ada, b_ada.reshape(DEPTH, 6, 1, D_MODEL))


def _mod_row(tile, tiles_per_seq, latent):
    return (1 + tile // tiles_per_seq) if latent else 0


def _norm_mod(x, gain, shift, scale):
    y = x * lax.rsqrt(jnp.mean(x * x, axis=-1, keepdims=True) + EPS) * gain
    return y * (1.0 + scale) + shift


def _inproj_kernel(x_ref, mod_ref, n1_ref, w_ref, o_ref, osm_ref, *, tiles_per_seq, latent, tn):
    r = _mod_row(pl.program_id(0), tiles_per_seq, latent)
    h = _norm_mod(x_ref[...], n1_ref[...], mod_ref[0, pl.ds(r, 1), :], mod_ref[1, pl.ds(r, 1), :]).astype(BF16)
    for n in range(OFF_SMALL // tn):
        cols = slice(n * tn, (n + 1) * tn)
        o_ref[:, cols] = _bdot_nt(h, w_ref[cols, :]).astype(BF16)
    osm_ref[...] = _bdot_nt(h, w_ref[OFF_SMALL:, :])


def _resident(stacked, layer):
    shape = stacked.shape[1:]
    zeros = (0,) * len(shape)
    return pl.BlockSpec((None,) + shape, lambda *g: (layer,) + zeros, pipeline_mode=pl.Buffered(1))


def _inproj(x, mod_l, n1, w_in_p, layer, seq_len, latent, tm=512, tn=2048):
    rows = x.shape[0]
    kern = functools.partial(_inproj_kernel, tiles_per_seq=max(seq_len // tm, 1), latent=latent, tn=tn)
    return pl.pallas_call(
        kern,
        grid=(rows // tm,),
        in_specs=[pl.BlockSpec((tm, D_MODEL), lambda m: (m, 0)),
                  pl.BlockSpec((6, 8, D_MODEL), lambda m: (0, 0, 0)),
                  pl.BlockSpec((1, D_MODEL), lambda m: (0, 0)),
                  _resident(w_in_p, layer)],
        out_specs=[pl.BlockSpec((tm, OFF_SMALL), lambda m: (m, 0)),
                   pl.BlockSpec((tm, PROJ_PAD - OFF_SMALL), lambda m: (m, 0))],
        out_shape=[jax.ShapeDtypeStruct((rows, OFF_SMALL), BF16),
                   jax.ShapeDtypeStruct((rows, PROJ_PAD - OFF_SMALL), F32)],
        compiler_params=_params("arbitrary"),
        name="inproj",
    )(x, mod_l, n1, w_in_p)


def _head_rms(x, gain):
    return x * lax.rsqrt(jnp.mean(x * x, axis=-1, keepdims=True) + EPS) * gain


def _head_slices():
    return [slice(h * NA_HEAD_DIM, (h + 1) * NA_HEAD_DIM) for h in range(NA_HEADS)]


def _ctx_attn_kernel(*refs, has_prev, layer):
    q_ref, k_ref, v_ref, qg_ref, kg_ref = refs[:5]
    o_ref, ka_ref, va_ref = refs[-3:]
    ka_ref = _own_slab(ka_ref, layer, not has_prev)
    va_ref = _own_slab(va_ref, layer, not has_prev)
    hs = _head_slices()
    qn = [_head_rms(q_ref[:, sl].astype(F32), qg_ref[...]) * (NA_HEAD_DIM ** -0.5) for sl in hs]
    kn = [_head_rms(k_ref[:, sl].astype(F32), kg_ref[...]) for sl in hs]
    for sl, k in zip(hs, kn):
        ka_ref[:, sl] = k
    s = [_bdot_nt(q, k) for q, k in zip(qn, kn)]
    p = [jnp.exp(x - jnp.max(x, axis=-1, keepdims=True)) for x in s]
    pv = [_bdot(x, v_ref[:, sl]) for x, sl in zip(p, hs)]
    for sl, x, y in zip(hs, p, pv):
        o_ref[:, sl] = (y / jnp.sum(x, axis=-1, keepdims=True)).astype(BF16)
    va_ref[...] = v_ref[...].astype(F32)


def _layer_slab(shape_tail, index_tail, layer, first):
    if first:
        return pl.BlockSpec((None, DEPTH) + shape_tail, lambda b, *g: (b, 0) + index_tail(*g))
    return pl.BlockSpec((None, None) + shape_tail, lambda b, *g: (b, layer) + index_tail(*g))


def _own_slab(ref, layer, first):
    if not first:
        return ref
    for other in range(DEPTH):
        if other != layer:
            ref[other] = jnp.zeros(ref.shape[1:], ref.dtype)
    return ref.at[layer]


def _ctx_attn(p_ctx, qg, kg, prev_kv, layer, batch, seq):
    rows = batch * seq
    blk = lambda j: pl.BlockSpec((seq, 512), lambda b: (b, j))
    cache = jax.ShapeDtypeStruct((batch, DEPTH, seq, 512), F32)
    cache_spec = _layer_slab((seq, 512), lambda: (0, 0), layer, prev_kv is None)
    in_specs = [blk(OFF_NA_Q // 512), blk(OFF_NA_K // 512), blk(OFF_NA_V // 512),
                pl.BlockSpec((1, NA_HEAD_DIM), lambda b: (0, 0)),
                pl.BlockSpec((1, NA_HEAD_DIM), lambda b: (0, 0))]
    args = [p_ctx, p_ctx, p_ctx, qg, kg]
    aliases = {}
    if prev_kv is not None:
        in_specs += [pl.BlockSpec(memory_space=pl.ANY)] * 2
        args += list(prev_kv)
        aliases = {5: 1, 6: 2}
    return pl.pallas_call(
        functools.partial(_ctx_attn_kernel, has_prev=prev_kv is not None, layer=layer),
        grid=(batch,),
        in_specs=in_specs,
        out_specs=[blk(0), cache_spec, cache_spec],
        out_shape=[jax.ShapeDtypeStruct((rows, 512), BF16), cache, cache],
        input_output_aliases=aliases,
        compiler_params=_params("arbitrary"),
        name="ctx_attn",
    )(*args)


NA_NDR = 2 * NA_WIN_H - 1


def _na_bias_tiles(rpb):
    qc = np.arange(GRID_W)
    cs = np.clip(qc - NA_WIN_W // 2, 0, GRID_W - NA_WIN_W)
    col_ok = (qc[None, :] >= cs[:, None]) & (qc[None, :] < cs[:, None] + NA_WIN_W)
    nd = 2 * NA_WIN_W - 1
    dc = np.clip(qc[None, :] - qc[:, None] + NA_WIN_W - 1, 0, nd - 1)
    sel = (np.arange(nd)[:, None, None] == dc[None]).astype(np.float32).reshape(nd, GRID_W * GRID_W)
    t = jnp.einsum('lhrd,dn->lhrn', rpb.astype(F32), jnp.asarray(sel), precision=lax.Precision.HIGHEST)
    t = jnp.where(col_ok, t.reshape(rpb.shape[:3] + (GRID_W, GRID_W)), NEG)
    return jnp.concatenate([t, jnp.full(t.shape[:2] + (1, GRID_W, GRID_W), NEG, F32)], axis=2)


def _lat_attn_kernel(q_ref, k_ref, v_ref, ck_ref, cv_ref, tile_ref, qg_ref, kg_ref, o_ref, kn_scr, bias_scr, *, rows):
    m = pl.program_id(1)
    nblk = rows // NA_QROWS
    hs = _head_slices()

    @pl.when(m == 0)
    def _():
        for sl in hs:
            kn_scr[:, sl] = _head_rms(k_ref[:, sl].astype(F32), kg_ref[...]).astype(BF16)

    krow0 = jnp.clip(NA_QROWS * m - NA_WIN_H // 2, 0, rows - NA_KROWS)

    @pl.when((m == 0) | (m == 1) | (m == nblk - 1))
    def _():
        for i in range(NA_QROWS):
            r = NA_QROWS * m + i
            rs = jnp.clip(r - NA_WIN_H // 2, 0, rows - NA_WIN_H)
            for j in range(NA_KROWS):
                kr = krow0 + j
                idx = jnp.where((kr >= rs) & (kr < rs + NA_WIN_H), kr - r + NA_WIN_H - 1, NA_NDR)
                for h in range(NA_HEADS):
                    bias_scr[h, i * GRID_W:(i + 1) * GRID_W, j * GRID_W:(j + 1) * GRID_W] = tile_ref[h, idx]

    kstart = pl.multiple_of(krow0 * GRID_W, 256)
    nk = NA_KROWS * GRID_W
    qn = [(_head_rms(q_ref[:, sl].astype(F32), qg_ref[...]) * (NA_HEAD_DIM ** -0.5)).astype(BF16) for sl in hs]
    s_loc = [_bdot_nt(q, kn_scr[pl.ds(kstart, nk), sl]) + bias_scr[h] for h, (q, sl) in enumerate(zip(qn, hs))]
    s_ctx = [_bdot(q, ck_ref[h]) for h, q in enumerate(qn)]
    mx = [jnp.maximum(jnp.max(a, axis=-1, keepdims=True), jnp.max(c, axis=-1, keepdims=True))
          for a, c in zip(s_loc, s_ctx)]
    p_loc = [jnp.exp(a - x) for a, x in zip(s_loc, mx)]
    p_ctx = [jnp.exp(c - x) for c, x in zip(s_ctx, mx)]
    o_loc = [_bdot(p, v_ref[pl.ds(kstart, nk), sl]) for p, sl in zip(p_loc, hs)]
    o_ctx = [_bdot_nt(p, cv_ref[h]) for h, p in enumerate(p_ctx)]
    for h, sl in enumerate(hs):
        den = jnp.sum(p_loc[h], axis=-1, keepdims=True) + jnp.sum(p_ctx[h], axis=-1, keepdims=True)
        o_ref[:, sl] = ((o_loc[h] + o_ctx[h]) / den).astype(BF16)


def _lat_attn(p_lat, cache_k, cache_v, tiles, qg, kg, layer, batch, seq):
    rows = seq // GRID_W
    nblk = rows // NA_QROWS
    qtok = NA_QROWS * GRID_W
    past = cache_k.shape[-1]
    cache_spec = pl.BlockSpec((None, None, NA_HEADS, NA_HEAD_DIM, past), lambda b, m: (b, layer, 0, 0, 0))
    return pl.pallas_call(
        functools.partial(_lat_attn_kernel, rows=rows),
        grid=(batch, nblk),
        in_specs=[pl.BlockSpec((qtok, 512), lambda b, m: (b * nblk + m, OFF_NA_Q // 512)),
                  pl.BlockSpec((seq, 512), lambda b, m: (b, OFF_NA_K // 512)),
                  pl.BlockSpec((seq, 512), lambda b, m: (b, OFF_NA_V // 512)),
                  cache_spec, cache_spec,
                  pl.BlockSpec((None, NA_HEADS, NA_NDR + 1, GRID_W, GRID_W), lambda b, m: (layer, 0, 0, 0, 0)),
                  pl.BlockSpec((1, NA_HEAD_DIM), lambda b, m: (0, 0)),
                  pl.BlockSpec((1, NA_HEAD_DIM), lambda b, m: (0, 0))],
        out_specs=pl.BlockSpec((qtok, 512), lambda b, m: (b * nblk + m, 0)),
        out_shape=jax.ShapeDtypeStruct((batch * seq, 512), BF16),
        scratch_shapes=[pltpu.VMEM((seq, 512), BF16),
                        pltpu.VMEM((NA_HEADS, qtok, NA_KROWS * GRID_W), F32)],
        compiler_params=_params("arbitrary", "arbitrary"),
        name="lat_attn",
    )(p_lat, p_lat, p_lat, cache_k, cache_v, tiles, qg, kg)


def _group_masks(chunk):
    r = lax.broadcasted_iota(jnp.int32, (GROUP, GROUP), 0)
    c = lax.broadcasted_iota(jnp.int32, (GROUP, GROUP), 1)
    same = (r // chunk) == (c // chunk)
    return (same & (c <= r), same & (c >= r), same & (c < r), same & (c > r), same)


def _halving_masks(chunk, lower):
    r = lax.broadcasted_iota(jnp.int32, (GROUP, GROUP), 0)
    c = lax.broadcasted_iota(jnp.int32, (GROUP, GROUP), 1)
    if not lower:
        r, c = c, r
    masks = []
    s = 1
    while s < chunk:
        m = ((r // (2 * s)) == (c // (2 * s))) & ((r // s) % 2 == 1) & ((c // s) % 2 == 0)
        masks.append(jnp.where(m, 1.0, 0.0).astype(BF16))
        s *= 2
    return masks


def _group_eye():
    r = lax.broadcasted_iota(jnp.int32, (GROUP, GROUP), 0)
    c = lax.broadcasted_iota(jnp.int32, (GROUP, GROUP), 1)
    return jnp.where(r == c, 1.0, 0.0).astype(BF16)


def _sel_col(a, lane):
    idx = lax.broadcasted_iota(jnp.int32, a.shape, 1)
    return jnp.sum(jnp.where(idx == lane, a, 0.0), axis=1, keepdims=True)


def _sel_row(a, row):
    idx = lax.broadcasted_iota(jnp.int32, a.shape, 0)
    return jnp.sum(jnp.where(idx == row, a, 0.0), axis=0, keepdims=True)


def _conv3_silu(x, w):
    t = x.shape[0]
    row = lax.broadcasted_iota(jnp.int32, x.shape, 0)
    xm = jnp.where(row == 0, 0.0, pltpu.roll(x, 1, 0))
    xp = jnp.where(row == t - 1, 0.0, pltpu.roll(x, t - 1, 0))
    return _silu(xm * w[0:1, :] + x * w[1:2, :] + xp * w[2:3, :])


def _deltanet_kernel(*refs, hp, has_s0, emit_state, has_prev, layer):
    (q_ref, k_ref, v_ref, gate_ref, sm_ref, cq_ref, ck_ref, cv_ref, alog_ref, dtb_ref, og_ref), refs = refs[:11], refs[11:]
    if has_s0:
        s0_ref, refs = refs[0], refs[1:]
    if has_prev:
        refs = refs[1:]
    o_ref, refs = refs[0], refs[1:]
    if emit_state:
        sfin_ref, refs = refs[0], refs[1:]
    qn_scr, kn_scr, vc_scr, of_scr, ob_scr, s_scr = refs

    head0 = pl.program_id(1) * hp
    seq = q_ref.shape[0]
    hd = DN_HEAD_DIM
    ngroups = seq // GROUP
    nchunk = GROUP // DN_CHUNK

    q = _conv3_silu(q_ref[...].astype(F32), cq_ref[...])
    k = _conv3_silu(k_ref[...].astype(F32), ck_ref[...])
    vc_scr[...] = _conv3_silu(v_ref[...].astype(F32), cv_ref[...])
    for j in range(hp):
        sl = slice(j * hd, (j + 1) * hd)
        qj, kj = q[:, sl], k[:, sl]
        qn_scr[:, sl] = qj * lax.rsqrt(jnp.sum(qj * qj, axis=-1, keepdims=True) + EPS) * (hd ** -0.5)
        kn_scr[:, sl] = kj * lax.rsqrt(jnp.sum(kj * kj, axis=-1, keepdims=True) + EPS)

    if has_s0:
        s_scr[...] = s0_ref[...]
    else:
        s_scr[...] = jnp.zeros_like(s_scr)

    m_fi, m_bi, m_fs, m_bs, m_same = _group_masks(DN_CHUNK)
    incl = (m_fi, m_bi)
    strict = (m_fs, m_bs)
    tri = (m_fi.astype(BF16), m_bi.astype(BF16))
    ones_bd = m_same.astype(BF16)
    halving = (_halving_masks(DN_CHUNK, True), _halving_masks(DN_CHUNK, False))

    eye = _group_eye()
    chains = [(d, j) for d in range(2) for j in range(hp)]
    nc = len(chains)

    def group_step(t, carry):
        shared = []
        for d in range(2):
            gi = t if d == 0 else ngroups - 1 - t
            r0 = pl.multiple_of(gi * GROUP, GROUP)
            sm = sm_ref[pl.ds(r0, GROUP), :]
            g128 = -jnp.exp(alog_ref[...]) * _softplus(sm + dtb_ref[...])
            gcum = _dot_mask_f32(tri[d], g128)
            gtot = _dot_mask_f32(ones_bd, g128)
            shared.append((r0, gcum, gtot, gcum.T, _sigmoid(sm)))

        qg, kg, vg, beta, gc_col, gt_col, dec = [], [], [], [], [], [], []
        for d, j in chains:
            r0, gcum, gtot, gcum_t, beta128 = shared[d]
            sl = slice(j * hd, (j + 1) * hd)
            lane_g = SMALL_DECAY + DN_HEADS * d + head0 + j
            gc_col.append(_sel_col(gcum, lane_g))
            gt_col.append(_sel_col(gtot, lane_g))
            beta.append(_sel_col(beta128, SMALL_BETA + DN_HEADS * d + head0 + j))
            dec.append(jnp.exp(jnp.where(incl[d], gc_col[-1] - _sel_row(gcum_t, lane_g), NEG)))
            qg.append(qn_scr[pl.ds(r0, GROUP), sl])
            kg.append(kn_scr[pl.ds(r0, GROUP), sl])
            vg.append(vc_scr[pl.ds(r0, GROUP), sl])
        rng = range(nc)
        kk = [_bdot_nt(kg[i], kg[i]) for i in rng]
        qk = [_bdot_nt(qg[i], kg[i]) for i in rng]
        low = [jnp.where(strict[chains[i][0]], beta[i] * kk[i] * dec[i], 0.0).astype(BF16) for i in rng]
        attn = [(qk[i] * dec[i]).astype(BF16) for i in rng]
        e_gc = [jnp.exp(gc_col[i]) for i in rng]
        x = [jnp.concatenate([vg[i] * beta[i], kg[i] * (beta[i] * e_gc[i])], axis=1).astype(BF16) for i in rng]
        inv = [eye - low[i] * halving[chains[i][0]][0] for i in rng]
        for lvl in range(1, len(halving[0])):
            tmp = [_bdot(inv[i], low[i] * halving[chains[i][0]][lvl]).astype(BF16) for i in rng]
            upd = [_bdot(tmp[i], inv[i]).astype(BF16) for i in rng]
            inv = [inv[i] - upd[i] for i in rng]
        x = [_bdot(inv[i], x[i]) for i in rng]
        w = [x[i][:, hd:].astype(BF16) for i in rng]
        qd = [(qg[i] * e_gc[i]).astype(BF16) for i in rng]
        kd = [(kg[i] * jnp.exp(gt_col[i] - gc_col[i])).astype(BF16) for i in rng]
        s = [s_scr[d, j] for d, j in chains]
        o_parts = [[None] * nchunk for _ in rng]
        for cc in range(nchunk):
            rs = []
            for d, _ in chains:
                c = cc if d == 0 else nchunk - 1 - cc
                rs.append(slice(c * DN_CHUNK, (c + 1) * DN_CHUNK))
            ws = [_bdot(w[i][rs[i]], s[i]) for i in rng]
            qs = [_bdot(qd[i][rs[i]], s[i]) for i in rng]
            v_new = [x[i][rs[i], :hd] - ws[i] for i in rng]
            av = [_bdot(attn[i][rs[i], rs[i]], v_new[i]) for i in rng]
            kv = [_bdot_tn(kd[i][rs[i]], v_new[i]) for i in rng]
            for i in rng:
                o_parts[i][rs[i].start // DN_CHUNK] = qs[i] + av[i]
                s[i] = s[i] * jnp.exp(gt_col[i][rs[i].start:rs[i].start + 1, :]) + kv[i]
        for i, (d, j) in enumerate(chains):
            s_scr[d, j] = s[i]
            o_scr = of_scr if d == 0 else ob_scr
            o_scr[pl.ds(shared[d][0], GROUP), j * hd:(j + 1) * hd] = jnp.concatenate(o_parts[i], axis=0)
        return carry

    if ngroups == 1:
        group_step(0, 0)
    else:
        lax.fori_loop(0, ngroups, group_step, 0)

    for j in range(hp):
        sl = slice(j * hd, (j + 1) * hd)
        o = of_scr[:, sl] + ob_scr[:, sl]
        o = o * lax.rsqrt(jnp.mean(o * o, axis=-1, keepdims=True) + EPS) * og_ref[...]
        o_ref[:, sl] = (o * _silu(gate_ref[:, sl].astype(F32))).astype(BF16)
    if emit_state:
        _own_slab(sfin_ref, layer, not has_prev)[...] = s_scr[...]


def _deltanet(p, psm, conv_w, alog128, dtb128, out_g, s0, prev_state, layer, batch, seq, emit_state, hp):
    rows = batch * seq
    hd = DN_HEAD_DIM
    w = hp * hd
    col = lambda off: pl.BlockSpec((seq, w), lambda b, h: (b, off // w + h))
    cw = lambda part: pl.BlockSpec((3, w), lambda b, h: (0, part * (DN_HEADS // hp) + h))
    row128 = pl.BlockSpec((1, 128), lambda b, h: (0, 0))
    in_specs = [col(OFF_DN_Q), col(OFF_DN_K), col(OFF_DN_V), col(OFF_DN_GATE),
                pl.BlockSpec((seq, 128), lambda b, h: (b, 0)),
                cw(0), cw(1), cw(2), row128, row128, row128]
    args = [p, p, p, p, psm, conv_w, conv_w, conv_w, alog128, dtb128, out_g]
    if s0 is not None:
        in_specs.append(pl.BlockSpec((None, None, 2, hp, hd, hd), lambda b, h: (b, layer, 0, h, 0, 0)))
        args.append(s0)
    out_specs = [pl.BlockSpec((seq, w), lambda b, h: (b, h))]
    out_shape = [jax.ShapeDtypeStruct((rows, DN_HEADS * hd), BF16)]
    aliases = {}
    if emit_state:
        out_specs.append(_layer_slab((2, hp, hd, hd), lambda h: (0, h, 0, 0), layer, prev_state is None))
        out_shape.append(jax.ShapeDtypeStruct((batch, DEPTH, 2, DN_HEADS, hd, hd), F32))
        if prev_state is not None:
            in_specs.append(pl.BlockSpec(memory_space=pl.ANY))
            args.append(prev_state)
            aliases = {len(args) - 1: 1}
    res = pl.pallas_call(
        functools.partial(_deltanet_kernel, hp=hp, has_s0=s0 is not None, emit_state=emit_state,
                          has_prev=prev_state is not None, layer=layer),
        grid=(batch, DN_HEADS // hp),
        in_specs=in_specs,
        out_specs=out_specs,
        out_shape=out_shape,
        input_output_aliases=aliases,
        scratch_shapes=[pltpu.VMEM((seq, w), F32)] * 5 + [pltpu.VMEM((2, hp, hd, hd), F32)],
        compiler_params=_params("arbitrary", "arbitrary"),
        name="deltanet",
    )(*args)
    return res if emit_state else (res[0], None)


def _transpose_f32(x):
    n = x.shape[1]
    r = lax.broadcasted_iota(jnp.int32, (n, n), 0)
    c = lax.broadcasted_iota(jnp.int32, (n, n), 1)
    eye = jnp.where(r == c, 1.0, 0.0).astype(BF16)
    d = lambda p: lax.dot_general(eye, p, (((1,), (1,)), ((), ())), preferred_element_type=F32)
    hi, mid, lo = _split3(x)
    return d(hi) + d(mid) + d(lo)


def _gla_kernel(*refs, hp, has_s0, emit_state, has_prev, layer):
    (qk_ref, v_ref, gate_ref, sm_ref, wg_ref, bg_ref, og_ref), refs = refs[:7], refs[7:]
    if has_s0:
        s0_ref, refs = refs[0], refs[1:]
    if has_prev:
        refs = refs[1:]
    o_ref, refs = refs[0], refs[1:]
    if emit_state:
        sfin_ref, refs = refs[0], refs[1:]
    of_scr, ob_scr, s_scr = refs

    seq = qk_ref.shape[0]
    ngroups = seq // GROUP
    nchunk = GROUP // GLA_CHUNK
    dk = GLA_KEY_DIM

    for d in range(2):
        for j in range(hp):
            if has_s0:
                s_scr[d, j] = _transpose_f32(s0_ref[d, j])
            else:
                s_scr[d, j] = jnp.zeros(s_scr.shape[2:], F32)

    m_fi, m_bi, _, _, m_same = _group_masks(GLA_CHUNK)
    incl = (m_fi, m_bi)
    tri = (m_fi.astype(BF16), m_bi.astype(BF16))
    ones_bd = m_same.astype(BF16)

    chains = [(d, j) for d in range(2) for j in range(hp)]
    rng = range(len(chains))

    def group_step(t, carry):
        shared = []
        for d in range(2):
            gi = t if d == 0 else ngroups - 1 - t
            r0 = pl.multiple_of(gi * GROUP, GROUP)
            gl = _bdot(sm_ref[pl.ds(r0, GROUP), :], wg_ref[...]) + bg_ref[...]
            log_a = -_softplus(-gl) / GLA_TAU
            b_all = _dot_mask_f32(tri[d], log_a)
            btot_all = _dot_mask_f32(ones_bd, log_a)
            shared.append((r0, b_all, btot_all))
        b, btot, kg, vg, qe = [], [], [], [], []
        for d, j in chains:
            r0, b_all, btot_all = shared[d]
            sl = slice(j * 128, (j + 1) * 128)
            lanes = slice(j * 128 + d * dk, j * 128 + (d + 1) * dk)
            b.append(b_all[:, lanes])
            btot.append(btot_all[:, lanes])
            qk = qk_ref[pl.ds(r0, GROUP), sl].astype(F32)
            kg.append(qk[:, dk:])
            vg.append(v_ref[pl.ds(r0, GROUP), sl])
            qe.append((qk[:, :dk] * (dk ** -0.5) * jnp.exp(b[-1])).astype(BF16))
        vg = [v.astype(BF16) for v in vg]
        attn = [jnp.where(incl[chains[i][0]], _bdot_nt(qe[i], kg[i] * jnp.exp(-b[i])), 0.0).astype(BF16)
                for i in rng]
        o_in = [_bdot(attn[i], vg[i]) for i in rng]
        kd = [(kg[i] * jnp.exp(btot[i] - b[i])).astype(BF16) for i in rng]
        st = [s_scr[d, j] for d, j in chains]
        o_parts = [[None] * nchunk for _ in rng]
        for cc in range(nchunk):
            rs = []
            for d, _ in chains:
                c = cc if d == 0 else nchunk - 1 - cc
                rs.append(slice(c * GLA_CHUNK, (c + 1) * GLA_CHUNK))
            qs = [_bdot_nt(qe[i][rs[i]], st[i]) for i in rng]
            vk = [_bdot_tn(vg[i][rs[i]], kd[i][rs[i]]) for i in rng]
            for i in rng:
                o_parts[i][rs[i].start // GLA_CHUNK] = o_in[i][rs[i]] + qs[i]
                st[i] = st[i] * jnp.exp(btot[i][rs[i].start:rs[i].start + 1, :]) + vk[i]
        for i, (d, j) in enumerate(chains):
            s_scr[d, j] = st[i]
            o_scr = of_scr if d == 0 else ob_scr
            o_scr[pl.ds(shared[d][0], GROUP), j * 128:(j + 1) * 128] = jnp.concatenate(o_parts[i], axis=0)
        return carry

    if ngroups == 1:
        group_step(0, 0)
    else:
        lax.fori_loop(0, ngroups, group_step, 0)

    for j in range(hp):
        sl = slice(j * 128, (j + 1) * 128)
        o = of_scr[:, sl] + ob_scr[:, sl]
        o = o * lax.rsqrt(jnp.mean(o * o, axis=-1, keepdims=True) + EPS) * og_ref[...]
        o_ref[:, sl] = (o * _silu(gate_ref[:, sl].astype(F32))).astype(BF16)
    if emit_state:
        sfin = _own_slab(sfin_ref, layer, not has_prev)
        for d in range(2):
            for j in range(hp):
                sfin[d, j] = _transpose_f32(s_scr[d, j])


def _gla(p, psm, wg, bg, out_g, s0, prev_state, layer, batch, seq, emit_state, hp):
    rows = batch * seq
    dv, dk = GLA_VAL_DIM, GLA_KEY_DIM
    w = hp * 128
    col = lambda off: pl.BlockSpec((seq, w), lambda b, h: (b, off // w + h))
    in_specs = [col(OFF_GLA_QK), col(OFF_GLA_V), col(OFF_GLA_GATE),
                pl.BlockSpec((seq, 128), lambda b, h: (b, 0)),
                pl.BlockSpec((128, w), lambda b, h: (0, h)),
                pl.BlockSpec((1, w), lambda b, h: (0, h)),
                pl.BlockSpec((1, 128), lambda b, h: (0, 0))]
    args = [p, p, p, psm, wg, bg, out_g]
    if s0 is not None:
        in_specs.append(pl.BlockSpec((None, None, 2, hp, dk, dv), lambda b, h: (b, layer, 0, h, 0, 0)))
        args.append(s0)
    out_specs = [pl.BlockSpec((seq, w), lambda b, h: (b, h))]
    out_shape = [jax.ShapeDtypeStruct((rows, GLA_HEADS * dv), BF16)]
    aliases = {}
    if emit_state:
        out_specs.append(_layer_slab((2, hp, dk, dv), lambda h: (0, h, 0, 0), layer, prev_state is None))
        out_shape.append(jax.ShapeDtypeStruct((batch, DEPTH, 2, GLA_HEADS, dk, dv), F32))
        if prev_state is not None:
            in_specs.append(pl.BlockSpec(memory_space=pl.ANY))
            args.append(prev_state)
            aliases = {len(args) - 1: 1}
    res = pl.pallas_call(
        functools.partial(_gla_kernel, hp=hp, has_s0=s0 is not None, emit_state=emit_state,
                          has_prev=prev_state is not None, layer=layer),
        grid=(batch, GLA_HEADS // hp),
        in_specs=in_specs,
        out_specs=out_specs,
        out_shape=out_shape,
        input_output_aliases=aliases,
        scratch_shapes=[pltpu.VMEM((seq, w), F32)] * 2 + [pltpu.VMEM((2, hp, dv, dk), F32)],
        compiler_params=_params("arbitrary", "arbitrary"),
        name="gla",
    )(*args)
    return res if emit_state else (res[0], None)


def _merge_kernel(x_ref, oa_ref, ob_ref, oc_ref, ma_ref, mb_ref, mc_ref, mod_ref, wbr_ref, wo_ref, o_ref,
                  *, tiles_per_seq, latent):
    r = _mod_row(pl.program_id(0), tiles_per_seq, latent)
    merged = (_sigmoid(ma_ref[...].astype(F32)) * _bdot(oa_ref[...], wbr_ref[0])
              + _sigmoid(mb_ref[...].astype(F32)) * _bdot(ob_ref[...], wbr_ref[1])
              + _sigmoid(mc_ref[...].astype(F32)) * _bdot(oc_ref[...], wbr_ref[2]))
    o_ref[...] = x_ref[...] + mod_ref[2, pl.ds(r, 1), :] * _bdot(merged, wo_ref[...])


def _merge(x, o_a, o_b, o_c, p, mod_l, w_br, w_o, layer, seq_len, latent, tm=512):
    rows = x.shape[0]
    row_blk = lambda w: pl.BlockSpec((tm, w), lambda m: (m, 0))
    gate_blk = lambda j: pl.BlockSpec((tm, D_MODEL), lambda m: (m, OFF_M // D_MODEL + j))
    return pl.pallas_call(
        functools.partial(_merge_kernel, tiles_per_seq=max(seq_len // tm, 1), latent=latent),
        grid=(rows // tm,),
        in_specs=[row_blk(D_MODEL), row_blk(512), row_blk(512), row_blk(512),
                  gate_blk(0), gate_blk(1), gate_blk(2),
                  pl.BlockSpec((6, 8, D_MODEL), lambda m: (0, 0, 0)),
                  _resident(w_br, layer), _resident(w_o, layer)],
        out_specs=row_blk(D_MODEL),
        out_shape=jax.ShapeDtypeStruct((rows, D_MODEL), F32),
        compiler_params=_params("arbitrary"),
        name="merge",
    )(x, o_a, o_b, o_c, p, p, p, mod_l, w_br, w_o)


FFN_HALO = 16


def _ffn_kernel(x_ref, xp_ref, xn_ref, mod_ref, n2_ref, wu_ref, fc_ref, wd_ref, o_ref, h_scr, act_scr,
                *, tm, tn, seq_len, tiles_per_seq, latent):
    r = _mod_row(pl.program_id(0), tiles_per_seq, latent)
    shift = mod_ref[3, pl.ds(r, 1), :]
    scale = mod_ref[4, pl.ds(r, 1), :]
    nm = lambda x: _norm_mod(x, n2_ref[...], shift, scale)
    row0 = pl.program_id(0) * tm
    h_scr[0:FFN_HALO, :] = jnp.where(row0 % seq_len == 0, 0.0, nm(xp_ref[...])).astype(BF16)
    h_scr[FFN_HALO:FFN_HALO + tm, :] = nm(x_ref[...]).astype(BF16)
    h_scr[FFN_HALO + tm:, :] = jnp.where((row0 + tm) % seq_len == 0, 0.0, nm(xn_ref[...])).astype(BF16)

    ext = tm + 2 * FFN_HALO
    inner = slice(FFN_HALO, FFN_HALO + tm)
    bounds = list(range(seq_len, tm, seq_len))
    fix = 16
    rid = lax.broadcasted_iota(jnp.int32, (2 * fix, 1), 0)

    def up(n):
        hh = h_scr[...]
        cv = slice(n * tn, (n + 1) * tn)
        cg = slice(D_FF + n * tn, D_FF + (n + 1) * tn)
        return (jnp.dot(hh, wu_ref[:, cv], preferred_element_type=F32),
                jnp.dot(hh, wu_ref[:, cg], preferred_element_type=F32))

    def taps(u):
        return pltpu.roll(u, 1, 0)[inner], u[inner], pltpu.roll(u, ext - 1, 0)[inner]

    def conv(t, f, rows=slice(None), keep_prev=None, keep_next=None):
        prev, mid, nxt = (a[rows] for a in t)
        if keep_prev is not None:
            prev = jnp.where(keep_prev, prev, 0.0)
            nxt = jnp.where(keep_next, nxt, 0.0)
        return prev * f[0:1, :] + mid * f[1:2, :] + nxt * f[2:3, :]

    nt = D_FF // tn
    u_next = up(0)
    for n in range(nt):
        u_val, u_gate = u_next
        if n + 1 < nt:
            u_next = up(n + 1)
        cols = slice(n * tn, (n + 1) * tn)
        f_val = fc_ref[:, cols]
        f_gate = fc_ref[:, D_FF + n * tn:D_FF + (n + 1) * tn]
        t_val, t_gate = taps(u_val), taps(u_gate)
        act_scr[:, cols] = (_silu(conv(t_gate, f_gate)) * conv(t_val, f_val)).astype(BF16)
        for b in bounds:
            rows = slice(b - fix, b + fix)
            kp, kn = rid != fix, rid != fix - 1
            act_scr[rows, cols] = (_silu(conv(t_gate, f_gate, rows, kp, kn))
                                   * conv(t_val, f_val, rows, kp, kn)).astype(BF16)
    down = jnp.dot(act_scr[...], wd_ref[...], preferred_element_type=F32)
    o_ref[...] = x_ref[...] + mod_ref[5, pl.ds(r, 1), :] * down


def _ffn(x, mod_l, n2, w_up, f_conv, w_dn, layer, seq_len, latent, tm=1024, tn=256):
    rows = x.shape[0]
    hb = tm // FFN_HALO
    last_hb = rows // FFN_HALO - 1
    kern = functools.partial(_ffn_kernel, tm=tm, tn=tn, seq_len=seq_len, tiles_per_seq=max(seq_len // tm, 1),
                             latent=latent)
    return pl.pallas_call(
        kern,
        grid=(rows // tm,),
        in_specs=[pl.BlockSpec((tm, D_MODEL), lambda m: (m, 0)),
                  pl.BlockSpec((FFN_HALO, D_MODEL), lambda m: (jnp.maximum(m * hb - 1, 0), 0)),
                  pl.BlockSpec((FFN_HALO, D_MODEL), lambda m: (jnp.minimum((m + 1) * hb, last_hb), 0)),
                  pl.BlockSpec((6, 8, D_MODEL), lambda m: (0, 0, 0)),
                  pl.BlockSpec((1, D_MODEL), lambda m: (0, 0)),
                  _resident(w_up, layer), _resident(f_conv, layer), _resident(w_dn, layer)],
        out_specs=pl.BlockSpec((tm, D_MODEL), lambda m: (m, 0)),
        out_shape=jax.ShapeDtypeStruct((rows, D_MODEL), F32),
        scratch_shapes=[pltpu.VMEM((tm + 2 * FFN_HALO, D_MODEL), BF16), pltpu.VMEM((tm, D_FF), BF16)],
        compiler_params=_params("arbitrary"),
        name="ffn",
    )(x, x, x, mod_l, n2, w_up, f_conv, w_dn)


def _w_in_moves():
    src = np.cumsum([0, 512, 512, 512, 512, 512, 512, 512, 8, 8, 256, 256, 512, 512, 32, 1024, 1024, 1024])
    dk = GLA_KEY_DIM
    moves = [(0, 0, int(src[7]))]
    for h in range(GLA_HEADS):
        moves.append((OFF_GLA_QK + 2 * dk * h, int(src[9]) + dk * h, dk))
        moves.append((OFF_GLA_QK + 2 * dk * h + dk, int(src[10]) + dk * h, dk))
    moves += [(OFF_GLA_V, int(src[11]), 512), (OFF_GLA_GATE, int(src[12]), 512), (OFF_M, int(src[14]), 3072),
              (OFF_SMALL + SMALL_BETA, int(src[7]), 8), (OFF_SMALL + SMALL_DECAY, int(src[8]), 8),
              (OFF_SMALL + SMALL_LR, int(src[13]), 32)]
    return moves


def _permute_kernel(w_ref, o_ref):
    small = []
    for dst, src, width in _w_in_moves():
        if dst >= OFF_SMALL:
            small.append(w_ref[src:src + width, :])
        else:
            o_ref[dst:dst + width, :] = w_ref[src:src + width, :].astype(BF16)
    used = sum(s.shape[0] for s in small)
    small.append(jnp.zeros((PROJ_PAD - OFF_SMALL - used, o_ref.shape[1]), F32))
    o_ref[OFF_SMALL:, :] = jnp.concatenate(small, axis=0).astype(BF16)


def _permute_w_in(w_in_t, tc=256):
    proj = w_in_t.shape[1]
    return pl.pallas_call(
        _permute_kernel,
        grid=(DEPTH, D_MODEL // tc),
        in_specs=[pl.BlockSpec((None, proj, tc), lambda l, c: (l, 0, c))],
        out_specs=pl.BlockSpec((None, PROJ_PAD, tc), lambda l, c: (l, 0, c)),
        out_shape=jax.ShapeDtypeStruct((DEPTH, PROJ_PAD, D_MODEL), BF16),
        compiler_params=_params("arbitrary", "arbitrary"),
        name="permute_w_in",
    )(w_in_t)


def _gla_gate_weights(gla_w, gla_b):
    r, dk = GLA_GATE_RANK, GLA_KEY_DIM
    w = gla_w.reshape(DEPTH, 2, r, GLA_HEADS, dk)
    wg = jnp.zeros((DEPTH, GLA_HEADS, 128, 2, dk), F32)
    for d in range(2):
        wg = wg.at[:, :, SMALL_LR + d * r:SMALL_LR + (d + 1) * r, d, :].set(jnp.transpose(w[:, d], (0, 2, 1, 3)))
    bg = jnp.transpose(gla_b.reshape(DEPTH, 2, GLA_HEADS, dk), (0, 2, 1, 3)).reshape(DEPTH, 1, GLA_HEADS * 2 * dk)
    wg = jnp.transpose(wg, (0, 2, 1, 3, 4)).reshape(DEPTH, 128, GLA_HEADS * 2 * dk)
    return wg.astype(BF16), bg.astype(F32)


def _lane_row(values, offset):
    n = values.shape[-1]
    return jnp.zeros((DEPTH, 1, 128), F32).at[:, 0, offset:offset + n].set(values.astype(F32))


def kernel(x_prompt, x_sample, cache_k, cache_v, state_dn, state_gla, c, c_ctx, w_ada, b_ada, norm1, w_in,
           na_q_norm, na_k_norm, na_rpb, dn_conv, dn_a_log, dn_dt_bias, dn_out_norm, gla_w_gate, gla_b_gate,
           gla_out_norm, w_branch, w_out, norm2, w_up, ffn_conv, w_down):
    batch, seq, _ = x_prompt.shape
    dbatch, dseq, _ = x_sample.shape
    past = cache_k.shape[2]

    cvec8 = jnp.concatenate([c_ctx[None, :], c, jnp.zeros((8 - 1 - dbatch, D_MODEL), F32)], axis=0)
    mod = _adaln(cvec8, w_ada, b_ada)

    w_in_p = _permute_w_in(jnp.swapaxes(w_in, 1, 2))
    w_br = w_branch.astype(BF16)
    w_o = w_out.astype(BF16)
    w_up_b = w_up.astype(BF16)
    w_dn_b = w_down.astype(BF16)
    tiles = _na_bias_tiles(na_rpb)
    wg, bg = _gla_gate_weights(gla_w_gate, gla_b_gate)
    alog128 = _lane_row(dn_a_log.reshape(DEPTH, 2 * DN_HEADS), SMALL_DECAY)
    dtb128 = _lane_row(dn_dt_bias.reshape(DEPTH, 2 * DN_HEADS), SMALL_DECAY)
    ck = jnp.transpose(cache_k, (0, 1, 3, 4, 2))
    cv = jnp.transpose(cache_v, (0, 1, 3, 4, 2))

    y_p = x_prompt.reshape(batch * seq, D_MODEL)
    y_s = x_sample.reshape(dbatch * dseq, D_MODEL)
    new_kv, new_dn, new_gla = None, None, None
    for l in range(DEPTH):
        qg, kg = na_q_norm[l][None, :], na_k_norm[l][None, :]
        n1, n2 = norm1[l][None, :], norm2[l][None, :]
        dn_g, gla_g = dn_out_norm[l][None, :], gla_out_norm[l][None, :]
        for latent in (False, True):
            x = y_s if latent else y_p
            b_, t_ = (dbatch, dseq) if latent else (batch, seq)
            p, psm = _inproj(x, mod[l], n1, w_in_p, l, t_, latent)
            if latent:
                o_a = _lat_attn(p, ck, cv, tiles, qg, kg, l, b_, t_)
                o_b, _ = _deltanet(p, psm, dn_conv[l], alog128[l], dtb128[l], dn_g, state_dn, None, l, b_, t_,
                                   False, 2)
                o_c, _ = _gla(p, psm, wg[l], bg[l], gla_g, state_gla, None, l, b_, t_, False, 2)
            else:
                o_a, k_all, v_all = _ctx_attn(p, qg, kg, new_kv, l, b_, t_)
                new_kv = (k_all, v_all)
                o_b, new_dn = _deltanet(p, psm, dn_conv[l], alog128[l], dtb128[l], dn_g, None, new_dn, l, b_, t_,
                                        True, DN_HEADS)
                o_c, new_gla = _gla(p, psm, wg[l], bg[l], gla_g, None, new_gla, l, b_, t_, True, GLA_HEADS)
            x = _merge(x, o_a, o_b, o_c, p, mod[l], w_br, w_o, l, t_, latent)
            x = _ffn(x, mod[l], n2, w_up_b, ffn_conv, w_dn_b, l, t_, latent)
            if latent:
                y_s = x
            else:
                y_p = x
    cache_shape = (batch, DEPTH, seq, NA_HEADS, NA_HEAD_DIM)
    return (y_p.reshape(batch, seq, D_MODEL), y_s.reshape(dbatch, dseq, D_MODEL),
            new_kv[0].reshape(cache_shape), new_kv[1].reshape(cache_shape), new_dn, new_gla)
```

```python
import functools

import numpy as np
import jax
import jax.numpy as jnp
from jax import lax
from jax.experimental import pallas as pl
from jax.experimental.pallas import tpu as pltpu

F32 = jnp.float32
BF16 = jnp.bfloat16

D_MODEL = 1024
DEPTH = 2
GRID_W = 64
EPS = 1e-6
NA_HEADS = 8
NA_HEAD_DIM = 64
NA_WIN_H = 8
NA_WIN_W = 16
DN_HEADS = 4
DN_HEAD_DIM = 128
DN_CHUNK = 64
GLA_HEADS = 4
GLA_KEY_DIM = 64
GLA_VAL_DIM = 128
GLA_GATE_RANK = 16
GLA_TAU = 16.0
GLA_CHUNK = 32
D_FF = 2816

OFF_NA_Q, OFF_NA_K, OFF_NA_V = 0, 512, 1024
OFF_DN_Q, OFF_DN_K, OFF_DN_V, OFF_DN_GATE = 1536, 2048, 2560, 3072
OFF_GLA_QK, OFF_GLA_V, OFF_GLA_GATE = 3584, 4096, 4608
OFF_M = 5120
OFF_SMALL = 8192
PROJ_PAD = 8320
SMALL_BETA, SMALL_DECAY, SMALL_LR = 0, 8, 16

GROUP = 256
NEG = -1e30
VMEM_LIMIT = 56 * 1024 * 1024

NA_QROWS = 4
NA_KROWS = 12


def _bdot(a, b):
    return jnp.dot(a.astype(BF16), b.astype(BF16), preferred_element_type=F32)


def _bdot_nt(a, b):
    return lax.dot_general(a.astype(BF16), b.astype(BF16), (((1,), (1,)), ((), ())), preferred_element_type=F32)


def _bdot_tn(a, b):
    return lax.dot_general(a.astype(BF16), b.astype(BF16), (((0,), (0,)), ((), ())), preferred_element_type=F32)


def _split3(x):
    hi = x.astype(BF16)
    r = x - hi.astype(F32)
    mid = r.astype(BF16)
    lo = (r - mid.astype(F32)).astype(BF16)
    return hi, mid, lo


def _dot_mask_f32(t, x):
    hi, mid, lo = _split3(x)
    d = lambda p: jnp.dot(t, p, preferred_element_type=F32)
    return d(hi) + d(mid) + d(lo)


def _dot3(a, b):
    ah = a.astype(BF16)
    al = (a - ah.astype(F32)).astype(BF16)
    bh = b.astype(BF16)
    bl = (b - bh.astype(F32)).astype(BF16)
    d = lambda p, q: jnp.dot(p, q, preferred_element_type=F32)
    return d(ah, bh) + d(ah, bl) + d(al, bh)


def _sigmoid(x):
    return 1.0 / (1.0 + jnp.exp(-x))


def _silu(x):
    return x * _sigmoid(x)


def _softplus(x):
    return jnp.maximum(x, 0.0) + jnp.log(1.0 + jnp.exp(-jnp.abs(x)))


def _params(*sem):
    return pltpu.CompilerParams(dimension_semantics=sem, vmem_limit_bytes=VMEM_LIMIT)


def _adaln_kernel(c_ref, w_ref, b_ref, o_ref):
    o_ref[...] = _bdot(_silu(c_ref[...]), w_ref[...]) + b_ref[...]


def _adaln(cvec8, w_ada, b_ada):
    return pl.pallas_call(
        _adaln_kernel,
        grid=(DEPTH, 6),
        in_specs=[pl.BlockSpec((8, D_MODEL), lambda l, j: (0, 0)),
                  pl.BlockSpec((None, D_MODEL, D_MODEL), lambda l, j: (l, 0, j)),
                  pl.BlockSpec((None, None, 1, D_MODEL), lambda l, j: (l, j, 0, 0))],
        out_specs=pl.BlockSpec((None, None, 8, D_MODEL), lambda l, j: (l, j, 0, 0)),
        out_shape=jax.ShapeDtypeStruct((DEPTH, 6, 8, D_MODEL), F32),
        compiler_params=_params("arbitrary", "arbitrary"),
        name="adaln",
    )(cvec8, w_ada, b_ada.reshape(DEPTH, 6, 1, D_MODEL))


def _mod_row(tile, tiles_per_seq, latent):
    return (1 + tile // tiles_per_seq) if latent else 0


def _norm_mod(x, gain, shift, scale):
    y = x * lax.rsqrt(jnp.mean(x * x, axis=-1, keepdims=True) + EPS) * gain
    return y * (1.0 + scale) + shift


def _inproj_kernel(x_ref, mod_ref, n1_ref, w_ref, o_ref, osm_ref, *, tiles_per_seq, latent, tn):
    r = _mod_row(pl.program_id(0), tiles_per_seq, latent)
    h = _norm_mod(x_ref[...], n1_ref[...], mod_ref[0, pl.ds(r, 1), :], mod_ref[1, pl.ds(r, 1), :]).astype(BF16)
    for n in range(OFF_SMALL // tn):
        cols = slice(n * tn, (n + 1) * tn)
        o_ref[:, cols] = _bdot_nt(h, w_ref[cols, :]).astype(BF16)
    osm_ref[...] = _bdot_nt(h, w_ref[OFF_SMALL:, :])


def _resident(stacked, layer):
    shape = stacked.shape[1:]
    zeros = (0,) * len(shape)
    return pl.BlockSpec((None,) + shape, lambda *g: (layer,) + zeros, pipeline_mode=pl.Buffered(1))


def _inproj(x, mod_l, n1, w_in_p, layer, seq_len, latent, tm=512, tn=2048):
    rows = x.shape[0]
    kern = functools.partial(_inproj_kernel, tiles_per_seq=max(seq_len // tm, 1), latent=latent, tn=tn)
    return pl.pallas_call(
        kern,
        grid=(rows // tm,),
        in_specs=[pl.BlockSpec((tm, D_MODEL), lambda m: (m, 0)),
                  pl.BlockSpec((6, 8, D_MODEL), lambda m: (0, 0, 0)),
                  pl.BlockSpec((1, D_MODEL), lambda m: (0, 0)),
                  _resident(w_in_p, layer)],
        out_specs=[pl.BlockSpec((tm, OFF_SMALL), lambda m: (m, 0)),
                   pl.BlockSpec((tm, PROJ_PAD - OFF_SMALL), lambda m: (m, 0))],
        out_shape=[jax.ShapeDtypeStruct((rows, OFF_SMALL), BF16),
                   jax.ShapeDtypeStruct((rows, PROJ_PAD - OFF_SMALL), F32)],
        compiler_params=_params("arbitrary"),
        name="inproj",
    )(x, mod_l, n1, w_in_p)


def _head_rms(x, gain):
    return x * lax.rsqrt(jnp.mean(x * x, axis=-1, keepdims=True) + EPS) * gain


def _head_slices():
    return [slice(h * NA_HEAD_DIM, (h + 1) * NA_HEAD_DIM) for h in range(NA_HEADS)]


def _ctx_attn_kernel(*refs, has_prev, layer):
    q_ref, k_ref, v_ref, qg_ref, kg_ref = refs[:5]
    o_ref, ka_ref, va_ref = refs[-3:]
    ka_ref = _own_slab(ka_ref, layer, not has_prev)
    va_ref = _own_slab(va_ref, layer, not has_prev)
    hs = _head_slices()
    qn = [_head_rms(q_ref[:, sl].astype(F32), qg_ref[...]) * (NA_HEAD_DIM ** -0.5) for sl in hs]
    kn = [_head_rms(k_ref[:, sl].astype(F32), kg_ref[...]) for sl in hs]
    for sl, k in zip(hs, kn):
        ka_ref[:, sl] = k
    s = [_bdot_nt(q, k) for q, k in zip(qn, kn)]
    p = [jnp.exp(x - jnp.max(x, axis=-1, keepdims=True)) for x in s]
    pv = [_bdot(x, v_ref[:, sl]) for x, sl in zip(p, hs)]
    for sl, x, y in zip(hs, p, pv):
        o_ref[:, sl] = (y / jnp.sum(x, axis=-1, keepdims=True)).astype(BF16)
    va_ref[...] = v_ref[...].astype(F32)


def _layer_slab(shape_tail, index_tail, layer, first):
    if first:
        return pl.BlockSpec((None, DEPTH) + shape_tail, lambda b, *g: (b, 0) + index_tail(*g))
    return pl.BlockSpec((None, None) + shape_tail, lambda b, *g: (b, layer) + index_tail(*g))


def _own_slab(ref, layer, first):
    if not first:
        return ref
    for other in range(DEPTH):
        if other != layer:
            ref[other] = jnp.zeros(ref.shape[1:], ref.dtype)
    return ref.at[layer]


def _ctx_attn(p_ctx, qg, kg, prev_kv, layer, batch, seq):
    rows = batch * seq
    blk = lambda j: pl.BlockSpec((seq, 512), lambda b: (b, j))
    cache = jax.ShapeDtypeStruct((batch, DEPTH, seq, 512), F32)
    cache_spec = _layer_slab((seq, 512), lambda: (0, 0), layer, prev_kv is None)
    in_specs = [blk(OFF_NA_Q // 512), blk(OFF_NA_K // 512), blk(OFF_NA_V // 512),
                pl.BlockSpec((1, NA_HEAD_DIM), lambda b: (0, 0)),
                pl.BlockSpec((1, NA_HEAD_DIM), lambda b: (0, 0))]
    args = [p_ctx, p_ctx, p_ctx, qg, kg]
    aliases = {}
    if prev_kv is not None:
        in_specs += [pl.BlockSpec(memory_space=pl.ANY)] * 2
        args += list(prev_kv)
        aliases = {5: 1, 6: 2}
    return pl.pallas_call(
        functools.partial(_ctx_attn_kernel, has_prev=prev_kv is not None, layer=layer),
        grid=(batch,),
        in_specs=in_specs,
        out_specs=[blk(0), cache_spec, cache_spec],
        out_shape=[jax.ShapeDtypeStruct((rows, 512), BF16), cache, cache],
        input_output_aliases=aliases,
        compiler_params=_params("arbitrary"),
        name="ctx_attn",
    )(*args)


NA_NDR = 2 * NA_WIN_H - 1


def _na_bias_tiles(rpb):
    qc = np.arange(GRID_W)
    cs = np.clip(qc - NA_WIN_W // 2, 0, GRID_W - NA_WIN_W)
    col_ok = (qc[None, :] >= cs[:, None]) & (qc[None, :] < cs[:, None] + NA_WIN_W)
    nd = 2 * NA_WIN_W - 1
    dc = np.clip(qc[None, :] - qc[:, None] + NA_WIN_W - 1, 0, nd - 1)
    sel = (np.arange(nd)[:, None, None] == dc[None]).astype(np.float32).reshape(nd, GRID_W * GRID_W)
    t = jnp.einsum('lhrd,dn->lhrn', rpb.astype(F32), jnp.asarray(sel), precision=lax.Precision.HIGHEST)
    t = jnp.where(col_ok, t.reshape(rpb.shape[:3] + (GRID_W, GRID_W)), NEG)
    return jnp.concatenate([t, jnp.full(t.shape[:2] + (1, GRID_W, GRID_W), NEG, F32)], axis=2)


def _lat_attn_kernel(q_ref, k_ref, v_ref, ck_ref, cv_ref, tile_ref, qg_ref, kg_ref, o_ref, kn_scr, bias_scr, *, rows):
    m = pl.program_id(1)
    nblk = rows // NA_QROWS
    hs = _head_slices()

    @pl.when(m == 0)
    def _():
        for sl in hs:
            kn_scr[:, sl] = _head_rms(k_ref[:, sl].astype(F32), kg_ref[...]).astype(BF16)

    krow0 = jnp.clip(NA_QROWS * m - NA_WIN_H // 2, 0, rows - NA_KROWS)

    @pl.when((m == 0) | (m == 1) | (m == nblk - 1))
    def _():
        for i in range(NA_QROWS):
            r = NA_QROWS * m + i
            rs = jnp.clip(r - NA_WIN_H // 2, 0, rows - NA_WIN_H)
            for j in range(NA_KROWS):
                kr = krow0 + j
                idx = jnp.where((kr >= rs) & (kr < rs + NA_WIN_H), kr - r + NA_WIN_H - 1, NA_NDR)
                for h in range(NA_HEADS):
                    bias_scr[h, i * GRID_W:(i + 1) * GRID_W, j * GRID_W:(j + 1) * GRID_W] = tile_ref[h, idx]

    kstart = pl.multiple_of(krow0 * GRID_W, 256)
    nk = NA_KROWS * GRID_W
    qn = [(_head_rms(q_ref[:, sl].astype(F32), qg_ref[...]) * (NA_HEAD_DIM ** -0.5)).astype(BF16) for sl in hs]
    s_loc = [_bdot_nt(q, kn_scr[pl.ds(kstart, nk), sl]) + bias_scr[h] for h, (q, sl) in enumerate(zip(qn, hs))]
    s_ctx = [_bdot(q, ck_ref[h]) for h, q in enumerate(qn)]
    mx = [jnp.maximum(jnp.max(a, axis=-1, keepdims=True), jnp.max(c, axis=-1, keepdims=True))
          for a, c in zip(s_loc, s_ctx)]
    p_loc = [jnp.exp(a - x) for a, x in zip(s_loc, mx)]
    p_ctx = [jnp.exp(c - x) for c, x in zip(s_ctx, mx)]
    o_loc = [_bdot(p, v_ref[pl.ds(kstart, nk), sl]) for p, sl in zip(p_loc, hs)]
    o_ctx = [_bdot_nt(p, cv_ref[h]) for h, p in enumerate(p_ctx)]
    for h, sl in enumerate(hs):
        den = jnp.sum(p_loc[h], axis=-1, keepdims=True) + jnp.sum(p_ctx[h], axis=-1, keepdims=True)
        o_ref[:, sl] = ((o_loc[h] + o_ctx[h]) / den).astype(BF16)


def _lat_attn(p_lat, cache_k, cache_v, tiles, qg, kg, layer, batch, seq):
    rows = seq // GRID_W
    nblk = rows // NA_QROWS
    qtok = NA_QROWS * GRID_W
    past = cache_k.shape[-1]
    cache_spec = pl.BlockSpec((None, None, NA_HEADS, NA_HEAD_DIM, past), lambda b, m: (b, layer, 0, 0, 0))
    return pl.pallas_call(
        functools.partial(_lat_attn_kernel, rows=rows),
        grid=(batch, nblk),
        in_specs=[pl.BlockSpec((qtok, 512), lambda b, m: (b * nblk + m, OFF_NA_Q // 512)),
                  pl.BlockSpec((seq, 512), lambda b, m: (b, OFF_NA_K // 512)),
                  pl.BlockSpec((seq, 512), lambda b, m: (b, OFF_NA_V // 512)),
                  cache_spec, cache_spec,
                  pl.BlockSpec((None, NA_HEADS, NA_NDR + 1, GRID_W, GRID_W), lambda b, m: (layer, 0, 0, 0, 0)),
                  pl.BlockSpec((1, NA_HEAD_DIM), lambda b, m: (0, 0)),
                  pl.BlockSpec((1, NA_HEAD_DIM), lambda b, m: (0, 0))],
        out_specs=pl.BlockSpec((qtok, 512), lambda b, m: (b * nblk + m, 0)),
        out_shape=jax.ShapeDtypeStruct((batch * seq, 512), BF16),
        scratch_shapes=[pltpu.VMEM((seq, 512), BF16),
                        pltpu.VMEM((NA_HEADS, qtok, NA_KROWS * GRID_W), F32)],
        compiler_params=_params("arbitrary", "arbitrary"),
        name="lat_attn",
    )(p_lat, p_lat, p_lat, cache_k, cache_v, tiles, qg, kg)


def _group_masks(chunk):
    r = lax.broadcasted_iota(jnp.int32, (GROUP, GROUP), 0)
    c = lax.broadcasted_iota(jnp.int32, (GROUP, GROUP), 1)
    same = (r // chunk) == (c // chunk)
    return (same & (c <= r), same & (c >= r), same & (c < r), same & (c > r), same)


def _halving_masks(chunk, lower):
    r = lax.broadcasted_iota(jnp.int32, (GROUP, GROUP), 0)
    c = lax.broadcasted_iota(jnp.int32, (GROUP, GROUP), 1)
    if not lower:
        r, c = c, r
    masks = []
    s = 1
    while s < chunk:
        m = ((r // (2 * s)) == (c // (2 * s))) & ((r // s) % 2 == 1) & ((c // s) % 2 == 0)
        masks.append(jnp.where(m, 1.0, 0.0).astype(BF16))
        s *= 2
    return masks


def _group_eye():
    r = lax.broadcasted_iota(jnp.int32, (GROUP, GROUP), 0)
    c = lax.broadcasted_iota(jnp.int32, (GROUP, GROUP), 1)
    return jnp.where(r == c, 1.0, 0.0).astype(BF16)


def _sel_col(a, lane):
    idx = lax.broadcasted_iota(jnp.int32, a.shape, 1)
    return jnp.sum(jnp.where(idx == lane, a, 0.0), axis=1, keepdims=True)


def _sel_row(a, row):
    idx = lax.broadcasted_iota(jnp.int32, a.shape, 0)
    return jnp.sum(jnp.where(idx == row, a, 0.0), axis=0, keepdims=True)


def _conv3_silu(x, w):
    t = x.shape[0]
    row = lax.broadcasted_iota(jnp.int32, x.shape, 0)
    xm = jnp.where(row == 0, 0.0, pltpu.roll(x, 1, 0))
    xp = jnp.where(row == t - 1, 0.0, pltpu.roll(x, t - 1, 0))
    return _silu(xm * w[0:1, :] + x * w[1:2, :] + xp * w[2:3, :])


def _deltanet_kernel(*refs, hp, has_s0, emit_state, has_prev, layer):
    (q_ref, k_ref, v_ref, gate_ref, sm_ref, cq_ref, ck_ref, cv_ref, alog_ref, dtb_ref, og_ref), refs = refs[:11], refs[11:]
    if has_s0:
        s0_ref, refs = refs[0], refs[1:]
    if has_prev:
        refs = refs[1:]
    o_ref, refs = refs[0], refs[1:]
    if emit_state:
        sfin_ref, refs = refs[0], refs[1:]
    qn_scr, kn_scr, vc_scr, of_scr, ob_scr, s_scr = refs

    head0 = pl.program_id(1) * hp
    seq = q_ref.shape[0]
    hd = DN_HEAD_DIM
    ngroups = seq // GROUP
    nchunk = GROUP // DN_CHUNK

    q = _conv3_silu(q_ref[...].astype(F32), cq_ref[...])
    k = _conv3_silu(k_ref[...].astype(F32), ck_ref[...])
    vc_scr[...] = _conv3_silu(v_ref[...].astype(F32), cv_ref[...])
    for j in range(hp):
        sl = slice(j * hd, (j + 1) * hd)
        qj, kj = q[:, sl], k[:, sl]
        qn_scr[:, sl] = qj * lax.rsqrt(jnp.sum(qj * qj, axis=-1, keepdims=True) + EPS) * (hd ** -0.5)
        kn_scr[:, sl] = kj * lax.rsqrt(jnp.sum(kj * kj, axis=-1, keepdims=True) + EPS)

    if has_s0:
        s_scr[...] = s0_ref[...]
    else:
        s_scr[...] = jnp.zeros_like(s_scr)

    m_fi, m_bi, m_fs, m_bs, m_same = _group_masks(DN_CHUNK)
    incl = (m_fi, m_bi)
    strict = (m_fs, m_bs)
    tri = (m_fi.astype(BF16), m_bi.astype(BF16))
    ones_bd = m_same.astype(BF16)
    halving = (_halving_masks(DN_CHUNK, True), _halving_masks(DN_CHUNK, False))

    eye = _group_eye()
    chains = [(d, j) for d in range(2) for j in range(hp)]
    nc = len(chains)

    def group_step(t, carry):
        shared = []
        g128 = None
        for d in range(2):
            gi = t if d == 0 else ngroups - 1 - t
            r0 = pl.multiple_of(gi * GROUP, GROUP)
            if g128 is None or ngroups > 1:
                sm = sm_ref[pl.ds(r0, GROUP), :]
                g128 = -jnp.exp(alog_ref[...]) * _softplus(sm + dtb_ref[...])
                gtot = _dot_mask_f32(ones_bd, g128)
                beta128 = _sigmoid(sm)
            gcum = _dot_mask_f32(tri[d], g128)
            shared.append((r0, gcum, gtot, gcum.T, beta128))

        qg, kg, vg, beta, gc_col, gt_col, dec = [], [], [], [], [], [], []
        for d, j in chains:
            r0, gcum, gtot, gcum_t, beta128 = shared[d]
            sl = slice(j * hd, (j + 1) * hd)
            lane_g = SMALL_DECAY + DN_HEADS * d + head0 + j
            gc_col.append(_sel_col(gcum, lane_g))
            gt_col.append(_sel_col(gtot, lane_g))
            beta.append(_sel_col(beta128, SMALL_BETA + DN_HEADS * d + head0 + j))
            dec.append(jnp.exp(jnp.where(incl[d], gc_col[-1] - _sel_row(gcum_t, lane_g), NEG)))
            qg.append(qn_scr[pl.ds(r0, GROUP), sl])
            kg.append(kn_scr[pl.ds(r0, GROUP), sl])
            vg.append(vc_scr[pl.ds(r0, GROUP), sl])
        rng = range(nc)
        kk = [_bdot_nt(kg[i], kg[i]) for i in rng]
        qk = [_bdot_nt(qg[i], kg[i]) for i in rng]
        low = [jnp.where(strict[chains[i][0]], beta[i] * kk[i] * dec[i], 0.0).astype(BF16) for i in rng]
        attn = [(qk[i] * dec[i]).astype(BF16) for i in rng]
        e_gc = [jnp.exp(gc_col[i]) for i in rng]
        x = [jnp.concatenate([vg[i] * beta[i], kg[i] * (beta[i] * e_gc[i])], axis=1).astype(BF16) for i in rng]
        inv = [eye - low[i] * halving[chains[i][0]][0] for i in rng]
        for lvl in range(1, len(halving[0])):
            tmp = [_bdot(inv[i], low[i] * halving[chains[i][0]][lvl]).astype(BF16) for i in rng]
            upd = [_bdot(tmp[i], inv[i]).astype(BF16) for i in rng]
            inv = [inv[i] - upd[i] for i in rng]
        x = [_bdot(inv[i], x[i]) for i in rng]
        w = [x[i][:, hd:].astype(BF16) for i in rng]
        qd = [(qg[i] * e_gc[i]).astype(BF16) for i in rng]
        kd = [(kg[i] * jnp.exp(gt_col[i] - gc_col[i])).astype(BF16) for i in rng]
        s = [s_scr[d, j] for d, j in chains]
        o_parts = [[None] * nchunk for _ in rng]
        for cc in range(nchunk):
            rs = []
            for d, _ in chains:
                c = cc if d == 0 else nchunk - 1 - cc
                rs.append(slice(c * DN_CHUNK, (c + 1) * DN_CHUNK))
            ws = [_bdot(w[i][rs[i]], s[i]) for i in rng]
            qs = [_bdot(qd[i][rs[i]], s[i]) for i in rng]
            v_new = [x[i][rs[i], :hd] - ws[i] for i in rng]
            av = [_bdot(attn[i][rs[i], rs[i]], v_new[i]) for i in rng]
            kv = [_bdot_tn(kd[i][rs[i]], v_new[i]) for i in rng]
            for i in rng:
                o_parts[i][rs[i].start // DN_CHUNK] = qs[i] + av[i]
                s[i] = s[i] * jnp.exp(gt_col[i][rs[i].start:rs[i].start + 1, :]) + kv[i]
        for i, (d, j) in enumerate(chains):
            s_scr[d, j] = s[i]
            o_scr = of_scr if d == 0 else ob_scr
            o_scr[pl.ds(shared[d][0], GROUP), j * hd:(j + 1) * hd] = jnp.concatenate(o_parts[i], axis=0)
        return carry

    if ngroups == 1:
        group_step(0, 0)
    else:
        lax.fori_loop(0, ngroups, group_step, 0)

    for j in range(hp):
        sl = slice(j * hd, (j + 1) * hd)
        o = of_scr[:, sl] + ob_scr[:, sl]
        o = o * lax.rsqrt(jnp.mean(o * o, axis=-1, keepdims=True) + EPS) * og_ref[...]
        o_ref[:, sl] = (o * _silu(gate_ref[:, sl].astype(F32))).astype(BF16)
    if emit_state:
        _own_slab(sfin_ref, layer, not has_prev)[...] = s_scr[...]


def _deltanet(p, psm, conv_w, alog128, dtb128, out_g, s0, prev_state, layer, batch, seq, emit_state, hp):
    rows = batch * seq
    hd = DN_HEAD_DIM
    w = hp * hd
    col = lambda off: pl.BlockSpec((seq, w), lambda b, h: (b, off // w + h))
    cw = lambda part: pl.BlockSpec((3, w), lambda b, h: (0, part * (DN_HEADS // hp) + h))
    row128 = pl.BlockSpec((1, 128), lambda b, h: (0, 0))
    in_specs = [col(OFF_DN_Q), col(OFF_DN_K), col(OFF_DN_V), col(OFF_DN_GATE),
                pl.BlockSpec((seq, 128), lambda b, h: (b, 0)),
                cw(0), cw(1), cw(2), row128, row128, row128]
    args = [p, p, p, p, psm, conv_w, conv_w, conv_w, alog128, dtb128, out_g]
    if s0 is not None:
        in_specs.append(pl.BlockSpec((None, None, 2, hp, hd, hd), lambda b, h: (b, layer, 0, h, 0, 0)))
        args.append(s0)
    out_specs = [pl.BlockSpec((seq, w), lambda b, h: (b, h))]
    out_shape = [jax.ShapeDtypeStruct((rows, DN_HEADS * hd), BF16)]
    aliases = {}
    if emit_state:
        out_specs.append(_layer_slab((2, hp, hd, hd), lambda h: (0, h, 0, 0), layer, prev_state is None))
        out_shape.append(jax.ShapeDtypeStruct((batch, DEPTH, 2, DN_HEADS, hd, hd), F32))
        if prev_state is not None:
            in_specs.append(pl.BlockSpec(memory_space=pl.ANY))
            args.append(prev_state)
            aliases = {len(args) - 1: 1}
    res = pl.pallas_call(
        functools.partial(_deltanet_kernel, hp=hp, has_s0=s0 is not None, emit_state=emit_state,
                          has_prev=prev_state is not None, layer=layer),
        grid=(batch, DN_HEADS // hp),
        in_specs=in_specs,
        out_specs=out_specs,
        out_shape=out_shape,
        input_output_aliases=aliases,
        scratch_shapes=[pltpu.VMEM((seq, w), F32)] * 5 + [pltpu.VMEM((2, hp, hd, hd), F32)],
        compiler_params=_params("arbitrary", "arbitrary"),
        name="deltanet",
    )(*args)
    return res if emit_state else (res[0], None)


def _transpose_f32(x):
    n = x.shape[1]
    r = lax.broadcasted_iota(jnp.int32, (n, n), 0)
    c = lax.broadcasted_iota(jnp.int32, (n, n), 1)
    eye = jnp.where(r == c, 1.0, 0.0).astype(BF16)
    d = lambda p: lax.dot_general(eye, p, (((1,), (1,)), ((), ())), preferred_element_type=F32)
    hi, mid, lo = _split3(x)
    return d(hi) + d(mid) + d(lo)


def _gla_kernel(*refs, hp, has_s0, emit_state, has_prev, layer):
    (qk_ref, v_ref, gate_ref, sm_ref, wg_ref, bg_ref, og_ref), refs = refs[:7], refs[7:]
    if has_s0:
        s0_ref, refs = refs[0], refs[1:]
    if has_prev:
        refs = refs[1:]
    o_ref, refs = refs[0], refs[1:]
    if emit_state:
        sfin_ref, refs = refs[0], refs[1:]
    of_scr, ob_scr, s_scr = refs

    seq = qk_ref.shape[0]
    ngroups = seq // GROUP
    nchunk = GROUP // GLA_CHUNK
    dk = GLA_KEY_DIM

    for d in range(2):
        for j in range(hp):
            if has_s0:
                s_scr[d, j] = _transpose_f32(s0_ref[d, j])
            else:
                s_scr[d, j] = jnp.zeros(s_scr.shape[2:], F32)

    half = GLA_CHUNK // 2
    h_fi, h_bi, _, _, h_same = _group_masks(half)
    _, _, _, _, c_same = _group_masks(GLA_CHUNK)
    row = lax.broadcasted_iota(jnp.int32, (GROUP, GROUP), 0)
    col = lax.broadcasted_iota(jnp.int32, (GROUP, GROUP), 1)
    row_late = (row // half) % 2 == 1
    col_late = (col // half) % 2 == 1
    diag = (h_fi, h_bi)
    cross = (c_same & row_late & ~col_late, c_same & ~row_late & col_late)
    tri = (h_fi.astype(BF16), h_bi.astype(BF16))
    ones_bd = h_same.astype(BF16)
    late_rows = (lax.broadcasted_iota(jnp.int32, (GROUP, 1), 0) // half) % 2 == 1

    chains = [(d, j) for d in range(2) for j in range(hp)]
    rng = range(len(chains))

    def group_step(t, carry):
        shared = []
        log_a = None
        for d in range(2):
            gi = t if d == 0 else ngroups - 1 - t
            r0 = pl.multiple_of(gi * GROUP, GROUP)
            if log_a is None or ngroups > 1:
                gl = _bdot(sm_ref[pl.ds(r0, GROUP), :], wg_ref[...]) + bg_ref[...]
                log_a = -_softplus(-gl) / GLA_TAU
                t_all = _dot_mask_f32(ones_bd, log_a)
                t_prev = pltpu.roll(t_all, half, 0)
                t_next = pltpu.roll(t_all, GROUP - half, 0)
            c_all = _dot_mask_f32(tri[d], log_a)
            if d == 0:
                b_all = c_all + jnp.where(late_rows, t_prev, 0.0)
            else:
                b_all = c_all + jnp.where(late_rows, 0.0, t_next)
            btot_all = t_all + jnp.where(late_rows, t_prev, t_next)
            shared.append((r0, b_all, btot_all, c_all, t_all))
        b, btot, kg, vg, qe, qc, kc, kx = [], [], [], [], [], [], [], []
        for d, j in chains:
            r0, b_all, btot_all, c_all, t_all = shared[d]
            sl = slice(j * 128, (j + 1) * 128)
            lanes = slice(j * 128 + d * dk, j * 128 + (d + 1) * dk)
            b.append(b_all[:, lanes])
            btot.append(btot_all[:, lanes])
            c, tt = c_all[:, lanes], t_all[:, lanes]
            qk = qk_ref[pl.ds(r0, GROUP), sl].astype(F32)
            q = qk[:, :dk] * (dk ** -0.5)
            kg.append(qk[:, dk:])
            vg.append(v_ref[pl.ds(r0, GROUP), sl])
            qe.append((q * jnp.exp(b[-1])).astype(BF16))
            qc.append((q * jnp.exp(c)).astype(BF16))
            kc.append((kg[-1] * jnp.exp(-c)).astype(BF16))
            kx.append((kg[-1] * jnp.exp(tt - c)).astype(BF16))
        vg = [v.astype(BF16) for v in vg]
        a_diag = [_bdot_nt(qc[i], kc[i]) for i in rng]
        a_cross = [_bdot_nt(qc[i], kx[i]) for i in rng]
        attn = [jnp.where(diag[chains[i][0]], a_diag[i],
                          jnp.where(cross[chains[i][0]], a_cross[i], 0.0)).astype(BF16) for i in rng]
        o_in = [_bdot(attn[i], vg[i]) for i in rng]
        kd = [(kg[i] * jnp.exp(btot[i] - b[i])).astype(BF16) for i in rng]
        st = [s_scr[d, j] for d, j in chains]
        o_parts = [[None] * nchunk for _ in rng]
        for cc in range(nchunk):
            rs = []
            for d, _ in chains:
                c = cc if d == 0 else nchunk - 1 - cc
                rs.append(slice(c * GLA_CHUNK, (c + 1) * GLA_CHUNK))
            qs = [_bdot_nt(qe[i][rs[i]], st[i]) for i in rng]
            vk = [_bdot_tn(vg[i][rs[i]], kd[i][rs[i]]) for i in rng]
            for i in rng:
                o_parts[i][rs[i].start // GLA_CHUNK] = o_in[i][rs[i]] + qs[i]
                st[i] = st[i] * jnp.exp(btot[i][rs[i].start:rs[i].start + 1, :]) + vk[i]
        for i, (d, j) in enumerate(chains):
            s_scr[d, j] = st[i]
            o_scr = of_scr if d == 0 else ob_scr
            o_scr[pl.ds(shared[d][0], GROUP), j * 128:(j + 1) * 128] = jnp.concatenate(o_parts[i], axis=0)
        return carry

    if ngroups == 1:
        group_step(0, 0)
    else:
        lax.fori_loop(0, ngroups, group_step, 0)

    for j in range(hp):
        sl = slice(j * 128, (j + 1) * 128)
        o = of_scr[:, sl] + ob_scr[:, sl]
        o = o * lax.rsqrt(jnp.mean(o * o, axis=-1, keepdims=True) + EPS) * og_ref[...]
        o_ref[:, sl] = (o * _silu(gate_ref[:, sl].astype(F32))).astype(BF16)
    if emit_state:
        sfin = _own_slab(sfin_ref, layer, not has_prev)
        for d in range(2):
            for j in range(hp):
                sfin[d, j] = _transpose_f32(s_scr[d, j])


def _gla(p, psm, wg, bg, out_g, s0, prev_state, layer, batch, seq, emit_state, hp):
    rows = batch * seq
    dv, dk = GLA_VAL_DIM, GLA_KEY_DIM
    w = hp * 128
    col = lambda off: pl.BlockSpec((seq, w), lambda b, h: (b, off // w + h))
    in_specs = [col(OFF_GLA_QK), col(OFF_GLA_V), col(OFF_GLA_GATE),
                pl.BlockSpec((seq, 128), lambda b, h: (b, 0)),
                pl.BlockSpec((128, w), lambda b, h: (0, h)),
                pl.BlockSpec((1, w), lambda b, h: (0, h)),
                pl.BlockSpec((1, 128), lambda b, h: (0, 0))]
    args = [p, p, p, psm, wg, bg, out_g]
    if s0 is not None:
        in_specs.append(pl.BlockSpec((None, None, 2, hp, dk, dv), lambda b, h: (b, layer, 0, h, 0, 0)))
        args.append(s0)
    out_specs = [pl.BlockSpec((seq, w), lambda b, h: (b, h))]
    out_shape = [jax.ShapeDtypeStruct((rows, GLA_HEADS * dv), BF16)]
    aliases = {}
    if emit_state:
        out_specs.append(_layer_slab((2, hp, dk, dv), lambda h: (0, h, 0, 0), layer, prev_state is None))
        out_shape.append(jax.ShapeDtypeStruct((batch, DEPTH, 2, GLA_HEADS, dk, dv), F32))
        if prev_state is not None:
            in_specs.append(pl.BlockSpec(memory_space=pl.ANY))
            args.append(prev_state)
            aliases = {len(args) - 1: 1}
    res = pl.pallas_call(
        functools.partial(_gla_kernel, hp=hp, has_s0=s0 is not None, emit_state=emit_state,
                          has_prev=prev_state is not None, layer=layer),
        grid=(batch, GLA_HEADS // hp),
        in_specs=in_specs,
        out_specs=out_specs,
        out_shape=out_shape,
        input_output_aliases=aliases,
        scratch_shapes=[pltpu.VMEM((seq, w), F32)] * 2 + [pltpu.VMEM((2, hp, dv, dk), F32)],
        compiler_params=_params("arbitrary", "arbitrary"),
        name="gla",
    )(*args)
    return res if emit_state else (res[0], None)


def _merge_kernel(x_ref, oa_ref, ob_ref, oc_ref, ma_ref, mb_ref, mc_ref, mod_ref, wbr_ref, wo_ref, o_ref,
                  *, tiles_per_seq, latent):
    r = _mod_row(pl.program_id(0), tiles_per_seq, latent)
    merged = (_sigmoid(ma_ref[...].astype(F32)) * _bdot(oa_ref[...], wbr_ref[0])
              + _sigmoid(mb_ref[...].astype(F32)) * _bdot(ob_ref[...], wbr_ref[1])
              + _sigmoid(mc_ref[...].astype(F32)) * _bdot(oc_ref[...], wbr_ref[2]))
    o_ref[...] = x_ref[...] + mod_ref[2, pl.ds(r, 1), :] * _bdot(merged, wo_ref[...])


def _merge(x, o_a, o_b, o_c, p, mod_l, w_br, w_o, layer, seq_len, latent, tm=512):
    rows = x.shape[0]
    row_blk = lambda w: pl.BlockSpec((tm, w), lambda m: (m, 0))
    gate_blk = lambda j: pl.BlockSpec((tm, D_MODEL), lambda m: (m, OFF_M // D_MODEL + j))
    return pl.pallas_call(
        functools.partial(_merge_kernel, tiles_per_seq=max(seq_len // tm, 1), latent=latent),
        grid=(rows // tm,),
        in_specs=[row_blk(D_MODEL), row_blk(512), row_blk(512), row_blk(512),
                  gate_blk(0), gate_blk(1), gate_blk(2),
                  pl.BlockSpec((6, 8, D_MODEL), lambda m: (0, 0, 0)),
                  _resident(w_br, layer), _resident(w_o, layer)],
        out_specs=row_blk(D_MODEL),
        out_shape=jax.ShapeDtypeStruct((rows, D_MODEL), F32),
        compiler_params=_params("arbitrary"),
        name="merge",
    )(x, o_a, o_b, o_c, p, p, p, mod_l, w_br, w_o)


FFN_HALO = 16


def _ffn_kernel(x_ref, xp_ref, xn_ref, mod_ref, n2_ref, wu_ref, fc_ref, wd_ref, o_ref, h_scr, act_scr,
                *, tm, tn, seq_len, tiles_per_seq, latent):
    r = _mod_row(pl.program_id(0), tiles_per_seq, latent)
    shift = mod_ref[3, pl.ds(r, 1), :]
    scale = mod_ref[4, pl.ds(r, 1), :]
    nm = lambda x: _norm_mod(x, n2_ref[...], shift, scale)
    row0 = pl.program_id(0) * tm
    h_scr[0:FFN_HALO, :] = jnp.where(row0 % seq_len == 0, 0.0, nm(xp_ref[...])).astype(BF16)
    h_scr[FFN_HALO:FFN_HALO + tm, :] = nm(x_ref[...]).astype(BF16)
    h_scr[FFN_HALO + tm:, :] = jnp.where((row0 + tm) % seq_len == 0, 0.0, nm(xn_ref[...])).astype(BF16)

    ext = tm + 2 * FFN_HALO
    inner = slice(FFN_HALO, FFN_HALO + tm)
    bounds = list(range(seq_len, tm, seq_len))
    fix = 16
    rid = lax.broadcasted_iota(jnp.int32, (2 * fix, 1), 0)

    def up(n):
        hh = h_scr[...]
        cv = slice(n * tn, (n + 1) * tn)
        cg = slice(D_FF + n * tn, D_FF + (n + 1) * tn)
        return (jnp.dot(hh, wu_ref[:, cv], preferred_element_type=F32),
                jnp.dot(hh, wu_ref[:, cg], preferred_element_type=F32))

    def taps(u):
        return pltpu.roll(u, 1, 0)[inner], u[inner], pltpu.roll(u, ext - 1, 0)[inner]

    def conv(t, f, rows=slice(None), keep_prev=None, keep_next=None):
        prev, mid, nxt = (a[rows] for a in t)
        if keep_prev is not None:
            prev = jnp.where(keep_prev, prev, 0.0)
            nxt = jnp.where(keep_next, nxt, 0.0)
        return prev * f[0:1, :] + mid * f[1:2, :] + nxt * f[2:3, :]

    nt = D_FF // tn
    u_next = up(0)
    for n in range(nt):
        u_val, u_gate = u_next
        if n + 1 < nt:
            u_next = up(n + 1)
        cols = slice(n * tn, (n + 1) * tn)
        f_val = fc_ref[:, cols]
        f_gate = fc_ref[:, D_FF + n * tn:D_FF + (n + 1) * tn]
        t_val, t_gate = taps(u_val), taps(u_gate)
        act_scr[:, cols] = (_silu(conv(t_gate, f_gate)) * conv(t_val, f_val)).astype(BF16)
        for b in bounds:
            rows = slice(b - fix, b + fix)
            kp, kn = rid != fix, rid != fix - 1
            act_scr[rows, cols] = (_silu(conv(t_gate, f_gate, rows, kp, kn))
                                   * conv(t_val, f_val, rows, kp, kn)).astype(BF16)
    down = jnp.dot(act_scr[...], wd_ref[...], preferred_element_type=F32)
    o_ref[...] = x_ref[...] + mod_ref[5, pl.ds(r, 1), :] * down


def _ffn(x, mod_l, n2, w_up, f_conv, w_dn, layer, seq_len, latent, tm=1024, tn=256):
    rows = x.shape[0]
    hb = tm // FFN_HALO
    last_hb = rows // FFN_HALO - 1
    kern = functools.partial(_ffn_kernel, tm=tm, tn=tn, seq_len=seq_len, tiles_per_seq=max(seq_len // tm, 1),
                             latent=latent)
    return pl.pallas_call(
        kern,
        grid=(rows // tm,),
        in_specs=[pl.BlockSpec((tm, D_MODEL), lambda m: (m, 0)),
                  pl.BlockSpec((FFN_HALO, D_MODEL), lambda m: (jnp.maximum(m * hb - 1, 0), 0)),
                  pl.BlockSpec((FFN_HALO, D_MODEL), lambda m: (jnp.minimum((m + 1) * hb, last_hb), 0)),
                  pl.BlockSpec((6, 8, D_MODEL), lambda m: (0, 0, 0)),
                  pl.BlockSpec((1, D_MODEL), lambda m: (0, 0)),
                  _resident(w_up, layer), _resident(f_conv, layer), _resident(w_dn, layer)],
        out_specs=pl.BlockSpec((tm, D_MODEL), lambda m: (m, 0)),
        out_shape=jax.ShapeDtypeStruct((rows, D_MODEL), F32),
        scratch_shapes=[pltpu.VMEM((tm + 2 * FFN_HALO, D_MODEL), BF16), pltpu.VMEM((tm, D_FF), BF16)],
        compiler_params=_params("arbitrary"),
        name="ffn",
    )(x, x, x, mod_l, n2, w_up, f_conv, w_dn)


def _w_in_moves():
    src = np.cumsum([0, 512, 512, 512, 512, 512, 512, 512, 8, 8, 256, 256, 512, 512, 32, 1024, 1024, 1024])
    dk = GLA_KEY_DIM
    moves = [(0, 0, int(src[7]))]
    for h in range(GLA_HEADS):
        moves.append((OFF_GLA_QK + 2 * dk * h, int(src[9]) + dk * h, dk))
        moves.append((OFF_GLA_QK + 2 * dk * h + dk, int(src[10]) + dk * h, dk))
    moves += [(OFF_GLA_V, int(src[11]), 512), (OFF_GLA_GATE, int(src[12]), 512), (OFF_M, int(src[14]), 3072),
              (OFF_SMALL + SMALL_BETA, int(src[7]), 8), (OFF_SMALL + SMALL_DECAY, int(src[8]), 8),
              (OFF_SMALL + SMALL_LR, int(src[13]), 32)]
    return moves


def _permute_kernel(w_ref, o_ref):
    small = []
    for dst, src, width in _w_in_moves():
        if dst >= OFF_SMALL:
            small.append(w_ref[src:src + width, :])
        else:
            o_ref[dst:dst + width, :] = w_ref[src:src + width, :].astype(BF16)
    used = sum(s.shape[0] for s in small)
    small.append(jnp.zeros((PROJ_PAD - OFF_SMALL - used, o_ref.shape[1]), F32))
    o_ref[OFF_SMALL:, :] = jnp.concatenate(small, axis=0).astype(BF16)


def _permute_w_in(w_in_t, tc=256):
    proj = w_in_t.shape[1]
    return pl.pallas_call(
        _permute_kernel,
        grid=(DEPTH, D_MODEL // tc),
        in_specs=[pl.BlockSpec((None, proj, tc), lambda l, c: (l, 0, c))],
        out_specs=pl.BlockSpec((None, PROJ_PAD, tc), lambda l, c: (l, 0, c)),
        out_shape=jax.ShapeDtypeStruct((DEPTH, PROJ_PAD, D_MODEL), BF16),
        compiler_params=_params("arbitrary", "arbitrary"),
        name="permute_w_in",
    )(w_in_t)


def _gla_gate_weights(gla_w, gla_b):
    r, dk = GLA_GATE_RANK, GLA_KEY_DIM
    w = gla_w.reshape(DEPTH, 2, r, GLA_HEADS, dk)
    wg = jnp.zeros((DEPTH, GLA_HEADS, 128, 2, dk), F32)
    for d in range(2):
        wg = wg.at[:, :, SMALL_LR + d * r:SMALL_LR + (d + 1) * r, d, :].set(jnp.transpose(w[:, d], (0, 2, 1, 3)))
    bg = jnp.transpose(gla_b.reshape(DEPTH, 2, GLA_HEADS, dk), (0, 2, 1, 3)).reshape(DEPTH, 1, GLA_HEADS * 2 * dk)
    wg = jnp.transpose(wg, (0, 2, 1, 3, 4)).reshape(DEPTH, 128, GLA_HEADS * 2 * dk)
    return wg.astype(BF16), bg.astype(F32)


def _lane_row(values, offset):
    n = values.shape[-1]
    return jnp.zeros((DEPTH, 1, 128), F32).at[:, 0, offset:offset + n].set(values.astype(F32))


def kernel(x_prompt, x_sample, cache_k, cache_v, state_dn, state_gla, c, c_ctx, w_ada, b_ada, norm1, w_in,
           na_q_norm, na_k_norm, na_rpb, dn_conv, dn_a_log, dn_dt_bias, dn_out_norm, gla_w_gate, gla_b_gate,
           gla_out_norm, w_branch, w_out, norm2, w_up, ffn_conv, w_down):
    batch, seq, _ = x_prompt.shape
    dbatch, dseq, _ = x_sample.shape
    past = cache_k.shape[2]

    cvec8 = jnp.concatenate([c_ctx[None, :], c, jnp.zeros((8 - 1 - dbatch, D_MODEL), F32)], axis=0)
    mod = _adaln(cvec8, w_ada, b_ada)

    w_in_p = _permute_w_in(jnp.swapaxes(w_in, 1, 2))
    w_br = w_branch.astype(BF16)
    w_o = w_out.astype(BF16)
    w_up_b = w_up.astype(BF16)
    w_dn_b = w_down.astype(BF16)
    tiles = _na_bias_tiles(na_rpb)
    wg, bg = _gla_gate_weights(gla_w_gate, gla_b_gate)
    alog128 = _lane_row(dn_a_log.reshape(DEPTH, 2 * DN_HEADS), SMALL_DECAY)
    dtb128 = _lane_row(dn_dt_bias.reshape(DEPTH, 2 * DN_HEADS), SMALL_DECAY)
    ck = jnp.transpose(cache_k, (0, 1, 3, 4, 2))
    cv = jnp.transpose(cache_v, (0, 1, 3, 4, 2))

    y_p = x_prompt.reshape(batch * seq, D_MODEL)
    y_s = x_sample.reshape(dbatch * dseq, D_MODEL)
    new_kv, new_dn, new_gla = None, None, None
    for l in range(DEPTH):
        qg, kg = na_q_norm[l][None, :], na_k_norm[l][None, :]
        n1, n2 = norm1[l][None, :], norm2[l][None, :]
        dn_g, gla_g = dn_out_norm[l][None, :], gla_out_norm[l][None, :]
        for latent in (False, True):
            x = y_s if latent else y_p
            b_, t_ = (dbatch, dseq) if latent else (batch, seq)
            p, psm = _inproj(x, mod[l], n1, w_in_p, l, t_, latent)
            if latent:
                o_a = _lat_attn(p, ck, cv, tiles, qg, kg, l, b_, t_)
                o_b, _ = _deltanet(p, psm, dn_conv[l], alog128[l], dtb128[l], dn_g, state_dn, None, l, b_, t_,
                                   False, 2)
                o_c, _ = _gla(p, psm, wg[l], bg[l], gla_g, state_gla, None, l, b_, t_, False, 2)
            else:
                o_a, k_all, v_all = _ctx_attn(p, qg, kg, new_kv, l, b_, t_)
                new_kv = (k_all, v_all)
                o_b, new_dn = _deltanet(p, psm, dn_conv[l], alog128[l], dtb128[l], dn_g, None, new_dn, l, b_, t_,
                                        True, DN_HEADS)
                o_c, new_gla = _gla(p, psm, wg[l], bg[l], gla_g, None, new_gla, l, b_, t_, True, GLA_HEADS)
            x = _merge(x, o_a, o_b, o_c, p, mod[l], w_br, w_o, l, t_, latent)
            x = _ffn(x, mod[l], n2, w_up_b, ffn_conv, w_dn_b, l, t_, latent)
            if latent:
                y_s = x
            else:
                y_p = x
    cache_shape = (batch, DEPTH, seq, NA_HEADS, NA_HEAD_DIM)
    return (y_p.reshape(batch, seq, D_MODEL), y_s.reshape(dbatch, dseq, D_MODEL),
            new_kv[0].reshape(cache_shape), new_kv[1].reshape(cache_shape), new_dn, new_gla)
```

```python
import functools

import numpy as np
import jax
import jax.numpy as jnp
from jax import lax
from jax.experimental import pallas as pl
from jax.experimental.pallas import tpu as pltpu

F32 = jnp.float32
BF16 = jnp.bfloat16

D_MODEL = 1024
DEPTH = 2
GRID_W = 64
EPS = 1e-6
NA_HEADS = 8
NA_HEAD_DIM = 64
NA_WIN_H = 8
NA_WIN_W = 16
DN_HEADS = 4
DN_HEAD_DIM = 128
DN_CHUNK = 64
GLA_HEADS = 4
GLA_KEY_DIM = 64
GLA_VAL_DIM = 128
GLA_GATE_RANK = 16
GLA_TAU = 16.0
GLA_CHUNK = 32
D_FF = 2816

OFF_NA_Q, OFF_NA_K, OFF_NA_V = 0, 512, 1024
OFF_DN_Q, OFF_DN_K, OFF_DN_V, OFF_DN_GATE = 1536, 2048, 2560, 3072
OFF_GLA_QK, OFF_GLA_V, OFF_GLA_GATE = 3584, 4096, 4608
OFF_M = 5120
OFF_SMALL = 8192
PROJ_PAD = 8320
SMALL_BETA, SMALL_DECAY, SMALL_LR = 0, 8, 16

GROUP = 256
NEG = -1e30
VMEM_LIMIT = 56 * 1024 * 1024

NA_QROWS = 4
NA_KROWS = 12


def _bdot(a, b):
    return jnp.dot(a.astype(BF16), b.astype(BF16), preferred_element_type=F32)


def _bdot_nt(a, b):
    return lax.dot_general(a.astype(BF16), b.astype(BF16), (((1,), (1,)), ((), ())), preferred_element_type=F32)


def _bdot_tn(a, b):
    return lax.dot_general(a.astype(BF16), b.astype(BF16), (((0,), (0,)), ((), ())), preferred_element_type=F32)


def _split3(x):
    hi = x.astype(BF16)
    r = x - hi.astype(F32)
    mid = r.astype(BF16)
    lo = (r - mid.astype(F32)).astype(BF16)
    return hi, mid, lo


def _dot_mask_f32(t, x):
    hi, mid, lo = _split3(x)
    d = lambda p: jnp.dot(t, p, preferred_element_type=F32)
    return d(hi) + d(mid) + d(lo)


def _dot3(a, b):
    ah = a.astype(BF16)
    al = (a - ah.astype(F32)).astype(BF16)
    bh = b.astype(BF16)
    bl = (b - bh.astype(F32)).astype(BF16)
    d = lambda p, q: jnp.dot(p, q, preferred_element_type=F32)
    return d(ah, bh) + d(ah, bl) + d(al, bh)


def _sigmoid(x):
    return 1.0 / (1.0 + jnp.exp(-x))


def _silu(x):
    return x * _sigmoid(x)


def _softplus(x):
    return jnp.maximum(x, 0.0) + jnp.log(1.0 + jnp.exp(-jnp.abs(x)))


def _params(*sem):
    return pltpu.CompilerParams(dimension_semantics=sem, vmem_limit_bytes=VMEM_LIMIT)


def _adaln_kernel(c_ref, w_ref, b_ref, o_ref):
    o_ref[...] = _bdot(_silu(c_ref[...]), w_ref[...]) + b_ref[...]


def _adaln(cvec8, w_ada, b_ada):
    return pl.pallas_call(
        _adaln_kernel,
        grid=(DEPTH, 6),
        in_specs=[pl.BlockSpec((8, D_MODEL), lambda l, j: (0, 0)),
                  pl.BlockSpec((None, D_MODEL, D_MODEL), lambda l, j: (l, 0, j)),
                  pl.BlockSpec((None, None, 1, D_MODEL), lambda l, j: (l, j, 0, 0))],
        out_specs=pl.BlockSpec((None, None, 8, D_MODEL), lambda l, j: (l, j, 0, 0)),
        out_shape=jax.ShapeDtypeStruct((DEPTH, 6, 8, D_MODEL), F32),
        compiler_params=_params("arbitrary", "arbitrary"),
        name="adaln",
    )(cvec8, w_ada, b_ada.reshape(DEPTH, 6, 1, D_MODEL))


def _mod_row(tile, tiles_per_seq, latent):
    return (1 + tile // tiles_per_seq) if latent else 0


def _norm_mod(x, gain, shift, scale):
    y = x * lax.rsqrt(jnp.mean(x * x, axis=-1, keepdims=True) + EPS) * gain
    return y * (1.0 + scale) + shift


def _inproj_kernel(x_ref, mod_ref, n1_ref, w_ref, o_ref, osm_ref, *, tiles_per_seq, latent, tn):
    r = _mod_row(pl.program_id(0), tiles_per_seq, latent)
    h = _norm_mod(x_ref[...], n1_ref[...], mod_ref[0, pl.ds(r, 1), :], mod_ref[1, pl.ds(r, 1), :]).astype(BF16)
    for n in range(OFF_SMALL // tn):
        cols = slice(n * tn, (n + 1) * tn)
        o_ref[:, cols] = _bdot_nt(h, w_ref[cols, :]).astype(BF16)
    osm_ref[...] = _bdot_nt(h, w_ref[OFF_SMALL:, :])


def _resident(stacked, layer):
    shape = stacked.shape[1:]
    zeros = (0,) * len(shape)
    return pl.BlockSpec((None,) + shape, lambda *g: (layer,) + zeros, pipeline_mode=pl.Buffered(1))


def _inproj(x, mod_l, n1, w_in_p, layer, seq_len, latent, tm=512, tn=2048):
    rows = x.shape[0]
    kern = functools.partial(_inproj_kernel, tiles_per_seq=max(seq_len // tm, 1), latent=latent, tn=tn)
    return pl.pallas_call(
        kern,
        grid=(rows // tm,),
        in_specs=[pl.BlockSpec((tm, D_MODEL), lambda m: (m, 0)),
                  pl.BlockSpec((6, 8, D_MODEL), lambda m: (0, 0, 0)),
                  pl.BlockSpec((1, D_MODEL), lambda m: (0, 0)),
                  _resident(w_in_p, layer)],
        out_specs=[pl.BlockSpec((tm, OFF_SMALL), lambda m: (m, 0)),
                   pl.BlockSpec((tm, PROJ_PAD - OFF_SMALL), lambda m: (m, 0))],
        out_shape=[jax.ShapeDtypeStruct((rows, OFF_SMALL), BF16),
                   jax.ShapeDtypeStruct((rows, PROJ_PAD - OFF_SMALL), F32)],
        compiler_params=_params("arbitrary"),
        name="inproj",
    )(x, mod_l, n1, w_in_p)


def _head_rms(x, gain):
    return x * lax.rsqrt(jnp.mean(x * x, axis=-1, keepdims=True) + EPS) * gain


def _head_slices():
    return [slice(h * NA_HEAD_DIM, (h + 1) * NA_HEAD_DIM) for h in range(NA_HEADS)]


def _ctx_attn_kernel(*refs, has_prev, layer):
    q_ref, k_ref, v_ref, qg_ref, kg_ref = refs[:5]
    o_ref, ka_ref, va_ref = refs[-3:]
    ka_ref = _own_slab(ka_ref, layer, not has_prev)
    va_ref = _own_slab(va_ref, layer, not has_prev)
    hs = _head_slices()
    qn = [_head_rms(q_ref[:, sl].astype(F32), qg_ref[...]) * (NA_HEAD_DIM ** -0.5) for sl in hs]
    kn = [_head_rms(k_ref[:, sl].astype(F32), kg_ref[...]) for sl in hs]
    for sl, k in zip(hs, kn):
        ka_ref[:, sl] = k
    s = [_bdot_nt(q, k) for q, k in zip(qn, kn)]
    p = [jnp.exp(x - jnp.max(x, axis=-1, keepdims=True)) for x in s]
    pv = [_bdot(x, v_ref[:, sl]) for x, sl in zip(p, hs)]
    for sl, x, y in zip(hs, p, pv):
        o_ref[:, sl] = (y / jnp.sum(x, axis=-1, keepdims=True)).astype(BF16)
    va_ref[...] = v_ref[...].astype(F32)


def _layer_slab(shape_tail, index_tail, layer, first):
    if first:
        return pl.BlockSpec((None, DEPTH) + shape_tail, lambda b, *g: (b, 0) + index_tail(*g))
    return pl.BlockSpec((None, None) + shape_tail, lambda b, *g: (b, layer) + index_tail(*g))


def _own_slab(ref, layer, first):
    if not first:
        return ref
    for other in range(DEPTH):
        if other != layer:
            ref[other] = jnp.zeros(ref.shape[1:], ref.dtype)
    return ref.at[layer]


def _ctx_attn(p_ctx, qg, kg, prev_kv, layer, batch, seq):
    rows = batch * seq
    blk = lambda j: pl.BlockSpec((seq, 512), lambda b: (b, j))
    cache = jax.ShapeDtypeStruct((batch, DEPTH, seq, 512), F32)
    cache_spec = _layer_slab((seq, 512), lambda: (0, 0), layer, prev_kv is None)
    in_specs = [blk(OFF_NA_Q // 512), blk(OFF_NA_K // 512), blk(OFF_NA_V // 512),
                pl.BlockSpec((1, NA_HEAD_DIM), lambda b: (0, 0)),
                pl.BlockSpec((1, NA_HEAD_DIM), lambda b: (0, 0))]
    args = [p_ctx, p_ctx, p_ctx, qg, kg]
    aliases = {}
    if prev_kv is not None:
        in_specs += [pl.BlockSpec(memory_space=pl.ANY)] * 2
        args += list(prev_kv)
        aliases = {5: 1, 6: 2}
    return pl.pallas_call(
        functools.partial(_ctx_attn_kernel, has_prev=prev_kv is not None, layer=layer),
        grid=(batch,),
        in_specs=in_specs,
        out_specs=[blk(0), cache_spec, cache_spec],
        out_shape=[jax.ShapeDtypeStruct((rows, 512), BF16), cache, cache],
        input_output_aliases=aliases,
        compiler_params=_params("arbitrary"),
        name="ctx_attn",
    )(*args)


NA_NDR = 2 * NA_WIN_H - 1


def _na_bias_tiles(rpb):
    qc = np.arange(GRID_W)
    cs = np.clip(qc - NA_WIN_W // 2, 0, GRID_W - NA_WIN_W)
    col_ok = (qc[None, :] >= cs[:, None]) & (qc[None, :] < cs[:, None] + NA_WIN_W)
    nd = 2 * NA_WIN_W - 1
    dc = np.clip(qc[None, :] - qc[:, None] + NA_WIN_W - 1, 0, nd - 1)
    sel = (np.arange(nd)[:, None, None] == dc[None]).astype(np.float32).reshape(nd, GRID_W * GRID_W)
    t = jnp.einsum('lhrd,dn->lhrn', rpb.astype(F32), jnp.asarray(sel), precision=lax.Precision.HIGHEST)
    t = jnp.where(col_ok, t.reshape(rpb.shape[:3] + (GRID_W, GRID_W)), NEG)
    return jnp.concatenate([t, jnp.full(t.shape[:2] + (1, GRID_W, GRID_W), NEG, F32)], axis=2)


def _lat_attn_kernel(q_ref, k_ref, v_ref, ck_ref, cv_ref, tile_ref, qg_ref, kg_ref, o_ref, kn_scr, bias_scr, *, rows):
    m = pl.program_id(1)
    nblk = rows // NA_QROWS
    hs = _head_slices()

    @pl.when(m == 0)
    def _():
        for sl in hs:
            kn_scr[:, sl] = _head_rms(k_ref[:, sl].astype(F32), kg_ref[...]).astype(BF16)

    krow0 = jnp.clip(NA_QROWS * m - NA_WIN_H // 2, 0, rows - NA_KROWS)

    @pl.when((m == 0) | (m == 1) | (m == nblk - 1))
    def _():
        for i in range(NA_QROWS):
            r = NA_QROWS * m + i
            rs = jnp.clip(r - NA_WIN_H // 2, 0, rows - NA_WIN_H)
            for j in range(NA_KROWS):
                kr = krow0 + j
                idx = jnp.where((kr >= rs) & (kr < rs + NA_WIN_H), kr - r + NA_WIN_H - 1, NA_NDR)
                for h in range(NA_HEADS):
                    bias_scr[h, i * GRID_W:(i + 1) * GRID_W, j * GRID_W:(j + 1) * GRID_W] = tile_ref[h, idx]

    kstart = pl.multiple_of(krow0 * GRID_W, 256)
    nk = NA_KROWS * GRID_W
    qn = [(_head_rms(q_ref[:, sl].astype(F32), qg_ref[...]) * (NA_HEAD_DIM ** -0.5)).astype(BF16) for sl in hs]
    s_loc = [_bdot_nt(q, kn_scr[pl.ds(kstart, nk), sl]) + bias_scr[h] for h, (q, sl) in enumerate(zip(qn, hs))]
    s_ctx = [_bdot(q, ck_ref[h]) for h, q in enumerate(qn)]
    mx = [jnp.maximum(jnp.max(a, axis=-1, keepdims=True), jnp.max(c, axis=-1, keepdims=True))
          for a, c in zip(s_loc, s_ctx)]
    p_loc = [jnp.exp(a - x) for a, x in zip(s_loc, mx)]
    p_ctx = [jnp.exp(c - x) for c, x in zip(s_ctx, mx)]
    o_loc = [_bdot(p, v_ref[pl.ds(kstart, nk), sl]) for p, sl in zip(p_loc, hs)]
    o_ctx = [_bdot_nt(p, cv_ref[h]) for h, p in enumerate(p_ctx)]
    for h, sl in enumerate(hs):
        den = jnp.sum(p_loc[h], axis=-1, keepdims=True) + jnp.sum(p_ctx[h], axis=-1, keepdims=True)
        o_ref[:, sl] = ((o_loc[h] + o_ctx[h]) / den).astype(BF16)


def _lat_attn(p_lat, cache_k, cache_v, tiles, qg, kg, layer, batch, seq):
    rows = seq // GRID_W
    nblk = rows // NA_QROWS
    qtok = NA_QROWS * GRID_W
    past = cache_k.shape[-1]
    cache_spec = pl.BlockSpec((None, None, NA_HEADS, NA_HEAD_DIM, past), lambda b, m: (b, layer, 0, 0, 0))
    return pl.pallas_call(
        functools.partial(_lat_attn_kernel, rows=rows),
        grid=(batch, nblk),
        in_specs=[pl.BlockSpec((qtok, 512), lambda b, m: (b * nblk + m, OFF_NA_Q // 512)),
                  pl.BlockSpec((seq, 512), lambda b, m: (b, OFF_NA_K // 512)),
                  pl.BlockSpec((seq, 512), lambda b, m: (b, OFF_NA_V // 512)),
                  cache_spec, cache_spec,
                  pl.BlockSpec((None, NA_HEADS, NA_NDR + 1, GRID_W, GRID_W), lambda b, m: (layer, 0, 0, 0, 0)),
                  pl.BlockSpec((1, NA_HEAD_DIM), lambda b, m: (0, 0)),
                  pl.BlockSpec((1, NA_HEAD_DIM), lambda b, m: (0, 0))],
        out_specs=pl.BlockSpec((qtok, 512), lambda b, m: (b * nblk + m, 0)),
        out_shape=jax.ShapeDtypeStruct((batch * seq, 512), BF16),
        scratch_shapes=[pltpu.VMEM((seq, 512), BF16),
                        pltpu.VMEM((NA_HEADS, qtok, NA_KROWS * GRID_W), F32)],
        compiler_params=_params("arbitrary", "arbitrary"),
        name="lat_attn",
    )(p_lat, p_lat, p_lat, cache_k, cache_v, tiles, qg, kg)


def _group_masks(chunk):
    r = lax.broadcasted_iota(jnp.int32, (GROUP, GROUP), 0)
    c = lax.broadcasted_iota(jnp.int32, (GROUP, GROUP), 1)
    same = (r // chunk) == (c // chunk)
    return (same & (c <= r), same & (c >= r), same & (c < r), same & (c > r), same)


def _halving_masks(chunk, lower):
    r = lax.broadcasted_iota(jnp.int32, (GROUP, GROUP), 0)
    c = lax.broadcasted_iota(jnp.int32, (GROUP, GROUP), 1)
    if not lower:
        r, c = c, r
    masks = []
    s = 1
    while s < chunk:
        m = ((r // (2 * s)) == (c // (2 * s))) & ((r // s) % 2 == 1) & ((c // s) % 2 == 0)
        masks.append(jnp.where(m, 1.0, 0.0).astype(BF16))
        s *= 2
    return masks


def _group_eye():
    r = lax.broadcasted_iota(jnp.int32, (GROUP, GROUP), 0)
    c = lax.broadcasted_iota(jnp.int32, (GROUP, GROUP), 1)
    return jnp.where(r == c, 1.0, 0.0).astype(BF16)


def _sel_col(a, lane):
    idx = lax.broadcasted_iota(jnp.int32, a.shape, 1)
    return jnp.sum(jnp.where(idx == lane, a, 0.0), axis=1, keepdims=True)


def _sel_row(a, row):
    idx = lax.broadcasted_iota(jnp.int32, a.shape, 0)
    return jnp.sum(jnp.where(idx == row, a, 0.0), axis=0, keepdims=True)


def _conv3_silu(x, w):
    t = x.shape[0]
    row = lax.broadcasted_iota(jnp.int32, x.shape, 0)
    xm = jnp.where(row == 0, 0.0, pltpu.roll(x, 1, 0))
    xp = jnp.where(row == t - 1, 0.0, pltpu.roll(x, t - 1, 0))
    return _silu(xm * w[0:1, :] + x * w[1:2, :] + xp * w[2:3, :])


def _deltanet_kernel(*refs, hp, has_s0, emit_state, has_prev, layer):
    (q_ref, k_ref, v_ref, gate_ref, sm_ref, cq_ref, ck_ref, cv_ref, alog_ref, dtb_ref, og_ref), refs = refs[:11], refs[11:]
    if has_s0:
        s0_ref, refs = refs[0], refs[1:]
    if has_prev:
        refs = refs[1:]
    o_ref, refs = refs[0], refs[1:]
    if emit_state:
        sfin_ref, refs = refs[0], refs[1:]
    qn_scr, kn_scr, vc_scr, of_scr, ob_scr, s_scr = refs

    head0 = pl.program_id(1) * hp
    seq = q_ref.shape[0]
    hd = DN_HEAD_DIM
    ngroups = seq // GROUP
    nchunk = GROUP // DN_CHUNK

    q = _conv3_silu(q_ref[...].astype(F32), cq_ref[...])
    k = _conv3_silu(k_ref[...].astype(F32), ck_ref[...])
    vc_scr[...] = _conv3_silu(v_ref[...].astype(F32), cv_ref[...])
    for j in range(hp):
        sl = slice(j * hd, (j + 1) * hd)
        qj, kj = q[:, sl], k[:, sl]
        qn_scr[:, sl] = qj * lax.rsqrt(jnp.sum(qj * qj, axis=-1, keepdims=True) + EPS) * (hd ** -0.5)
        kn_scr[:, sl] = kj * lax.rsqrt(jnp.sum(kj * kj, axis=-1, keepdims=True) + EPS)

    if has_s0:
        s_scr[...] = s0_ref[...]
    else:
        s_scr[...] = jnp.zeros_like(s_scr)

    m_fi, m_bi, m_fs, m_bs, m_same = _group_masks(DN_CHUNK)
    incl = (m_fi, m_bi)
    strict = (m_fs, m_bs)
    tri = (m_fi.astype(BF16), m_bi.astype(BF16))
    ones_bd = m_same.astype(BF16)
    halving = (_halving_masks(DN_CHUNK, True), _halving_masks(DN_CHUNK, False))

    eye = _group_eye()
    chains = [(d, j) for d in range(2) for j in range(hp)]
    nc = len(chains)

    def group_step(t, carry):
        shared = []
        g128 = None
        for d in range(2):
            gi = t if d == 0 else ngroups - 1 - t
            r0 = pl.multiple_of(gi * GROUP, GROUP)
            if g128 is None or ngroups > 1:
                sm = sm_ref[pl.ds(r0, GROUP), :]
                g128 = -jnp.exp(alog_ref[...]) * _softplus(sm + dtb_ref[...])
                gtot = _dot_mask_f32(ones_bd, g128)
                beta128 = _sigmoid(sm)
            gcum = _dot_mask_f32(tri[d], g128)
            shared.append((r0, gcum, gtot, gcum.T, beta128))

        qg, kg, vg, beta, gc_col, gt_col, dec = [], [], [], [], [], [], []
        for d, j in chains:
            r0, gcum, gtot, gcum_t, beta128 = shared[d]
            sl = slice(j * hd, (j + 1) * hd)
            lane_g = SMALL_DECAY + DN_HEADS * d + head0 + j
            gc_col.append(_sel_col(gcum, lane_g))
            gt_col.append(_sel_col(gtot, lane_g))
            beta.append(_sel_col(beta128, SMALL_BETA + DN_HEADS * d + head0 + j))
            dec.append(jnp.exp(jnp.where(incl[d], gc_col[-1] - _sel_row(gcum_t, lane_g), NEG)))
            qg.append(qn_scr[pl.ds(r0, GROUP), sl])
            kg.append(kn_scr[pl.ds(r0, GROUP), sl])
            vg.append(vc_scr[pl.ds(r0, GROUP), sl])
        rng = range(nc)
        kk = [_bdot_nt(kg[i], kg[i]) for i in rng]
        qk = [_bdot_nt(qg[i], kg[i]) for i in rng]
        low = [jnp.where(strict[chains[i][0]], beta[i] * kk[i] * dec[i], 0.0).astype(BF16) for i in rng]
        attn = [(qk[i] * dec[i]).astype(BF16) for i in rng]
        e_gc = [jnp.exp(gc_col[i]) for i in rng]
        x = [jnp.concatenate([vg[i] * beta[i], kg[i] * (beta[i] * e_gc[i])], axis=1).astype(BF16) for i in rng]
        inv = [eye - low[i] * halving[chains[i][0]][0] for i in rng]
        for lvl in range(1, len(halving[0])):
            tmp = [_bdot(inv[i], low[i] * halving[chains[i][0]][lvl]).astype(BF16) for i in rng]
            upd = [_bdot(tmp[i], inv[i]).astype(BF16) for i in rng]
            inv = [inv[i] - upd[i] for i in rng]
        x = [_bdot(inv[i], x[i]) for i in rng]
        w = [x[i][:, hd:].astype(BF16) for i in rng]
        qd = [(qg[i] * e_gc[i]).astype(BF16) for i in rng]
        kd = [(kg[i] * jnp.exp(gt_col[i] - gc_col[i])).astype(BF16) for i in rng]
        s = [s_scr[d, j] for d, j in chains]
        o_parts = [[None] * nchunk for _ in rng]
        for cc in range(nchunk):
            rs = []
            for d, _ in chains:
                c = cc if d == 0 else nchunk - 1 - cc
                rs.append(slice(c * DN_CHUNK, (c + 1) * DN_CHUNK))
            ws = [_bdot(w[i][rs[i]], s[i]) for i in rng]
            qs = [_bdot(qd[i][rs[i]], s[i]) for i in rng]
            v_new = [x[i][rs[i], :hd] - ws[i] for i in rng]
            av = [_bdot(attn[i][rs[i], rs[i]], v_new[i]) for i in rng]
            kv = [_bdot_tn(kd[i][rs[i]], v_new[i]) for i in rng]
            for i in rng:
                o_parts[i][rs[i].start // DN_CHUNK] = qs[i] + av[i]
                s[i] = s[i] * jnp.exp(gt_col[i][rs[i].start:rs[i].start + 1, :]) + kv[i]
        for i, (d, j) in enumerate(chains):
            s_scr[d, j] = s[i]
            o_scr = of_scr if d == 0 else ob_scr
            o_scr[pl.ds(shared[d][0], GROUP), j * hd:(j + 1) * hd] = jnp.concatenate(o_parts[i], axis=0)
        return carry

    if ngroups == 1:
        group_step(0, 0)
    else:
        lax.fori_loop(0, ngroups, group_step, 0)

    for j in range(hp):
        sl = slice(j * hd, (j + 1) * hd)
        o = of_scr[:, sl] + ob_scr[:, sl]
        o = o * lax.rsqrt(jnp.mean(o * o, axis=-1, keepdims=True) + EPS) * og_ref[...]
        o_ref[:, sl] = (o * _silu(gate_ref[:, sl].astype(F32))).astype(BF16)
    if emit_state:
        _own_slab(sfin_ref, layer, not has_prev)[...] = s_scr[...]


def _deltanet(p, psm, conv_w, alog128, dtb128, out_g, s0, prev_state, layer, batch, seq, emit_state, hp):
    rows = batch * seq
    hd = DN_HEAD_DIM
    w = hp * hd
    col = lambda off: pl.BlockSpec((seq, w), lambda b, h: (b, off // w + h))
    cw = lambda part: pl.BlockSpec((3, w), lambda b, h: (0, part * (DN_HEADS // hp) + h))
    row128 = pl.BlockSpec((1, 128), lambda b, h: (0, 0))
    in_specs = [col(OFF_DN_Q), col(OFF_DN_K), col(OFF_DN_V), col(OFF_DN_GATE),
                pl.BlockSpec((seq, 128), lambda b, h: (b, 0)),
                cw(0), cw(1), cw(2), row128, row128, row128]
    args = [p, p, p, p, psm, conv_w, conv_w, conv_w, alog128, dtb128, out_g]
    if s0 is not None:
        in_specs.append(pl.BlockSpec((None, None, 2, hp, hd, hd), lambda b, h: (b, layer, 0, h, 0, 0)))
        args.append(s0)
    out_specs = [pl.BlockSpec((seq, w), lambda b, h: (b, h))]
    out_shape = [jax.ShapeDtypeStruct((rows, DN_HEADS * hd), BF16)]
    aliases = {}
    if emit_state:
        out_specs.append(_layer_slab((2, hp, hd, hd), lambda h: (0, h, 0, 0), layer, prev_state is None))
        out_shape.append(jax.ShapeDtypeStruct((batch, DEPTH, 2, DN_HEADS, hd, hd), F32))
        if prev_state is not None:
            in_specs.append(pl.BlockSpec(memory_space=pl.ANY))
            args.append(prev_state)
            aliases = {len(args) - 1: 1}
    res = pl.pallas_call(
        functools.partial(_deltanet_kernel, hp=hp, has_s0=s0 is not None, emit_state=emit_state,
                          has_prev=prev_state is not None, layer=layer),
        grid=(batch, DN_HEADS // hp),
        in_specs=in_specs,
        out_specs=out_specs,
        out_shape=out_shape,
        input_output_aliases=aliases,
        scratch_shapes=[pltpu.VMEM((seq, w), F32)] * 5 + [pltpu.VMEM((2, hp, hd, hd), F32)],
        compiler_params=_params("arbitrary", "arbitrary"),
        name="deltanet",
    )(*args)
    return res if emit_state else (res[0], None)


def _transpose_f32(x):
    n = x.shape[1]
    r = lax.broadcasted_iota(jnp.int32, (n, n), 0)
    c = lax.broadcasted_iota(jnp.int32, (n, n), 1)
    eye = jnp.where(r == c, 1.0, 0.0).astype(BF16)
    d = lambda p: lax.dot_general(eye, p, (((1,), (1,)), ((), ())), preferred_element_type=F32)
    hi, mid, lo = _split3(x)
    return d(hi) + d(mid) + d(lo)


def _gla_kernel(*refs, hp, has_s0, emit_state, has_prev, layer):
    (qk_ref, v_ref, gate_ref, sm_ref, wg_ref, bg_ref, og_ref), refs = refs[:7], refs[7:]
    if has_s0:
        s0_ref, refs = refs[0], refs[1:]
    if has_prev:
        refs = refs[1:]
    o_ref, refs = refs[0], refs[1:]
    if emit_state:
        sfin_ref, refs = refs[0], refs[1:]
    of_scr, ob_scr, s_scr = refs

    seq = qk_ref.shape[0]
    ngroups = seq // GROUP
    nchunk = GROUP // GLA_CHUNK
    dk = GLA_KEY_DIM

    for d in range(2):
        for j in range(hp):
            if has_s0:
                s_scr[d, j] = _transpose_f32(s0_ref[d, j])
            else:
                s_scr[d, j] = jnp.zeros(s_scr.shape[2:], F32)

    half = GLA_CHUNK // 2
    h_fi, h_bi, _, _, h_same = _group_masks(half)
    _, _, _, _, c_same = _group_masks(GLA_CHUNK)
    row = lax.broadcasted_iota(jnp.int32, (GROUP, GROUP), 0)
    col = lax.broadcasted_iota(jnp.int32, (GROUP, GROUP), 1)
    row_late = (row // half) % 2 == 1
    col_late = (col // half) % 2 == 1
    allowed = (h_fi | (c_same & row_late & ~col_late), h_bi | (c_same & ~row_late & col_late))
    tri = (h_fi.astype(BF16), h_bi.astype(BF16))
    ones_bd = h_same.astype(BF16)
    late_rows = (lax.broadcasted_iota(jnp.int32, (GROUP, 1), 0) // half) % 2 == 1

    chains = [(d, j) for d in range(2) for j in range(hp)]
    rng = range(len(chains))

    def group_step(t, carry):
        shared = []
        log_a = None
        for d in range(2):
            gi = t if d == 0 else ngroups - 1 - t
            r0 = pl.multiple_of(gi * GROUP, GROUP)
            if log_a is None or ngroups > 1:
                gl = _bdot(sm_ref[pl.ds(r0, GROUP), :], wg_ref[...]) + bg_ref[...]
                log_a = -_softplus(-gl) / GLA_TAU
                t_all = _dot_mask_f32(ones_bd, log_a)
                t_prev = pltpu.roll(t_all, half, 0)
                t_next = pltpu.roll(t_all, GROUP - half, 0)
            c_all = _dot_mask_f32(tri[d], log_a)
            if d == 0:
                b_all = c_all + jnp.where(late_rows, t_prev, 0.0)
            else:
                b_all = c_all + jnp.where(late_rows, 0.0, t_next)
            btot_all = t_all + jnp.where(late_rows, t_prev, t_next)
            shared.append((r0, b_all, btot_all, c_all, t_all))
        b, btot, kg, vg, qe, qc, kc = [], [], [], [], [], [], []
        for d, j in chains:
            r0, b_all, btot_all, c_all, t_all = shared[d]
            sl = slice(j * 128, (j + 1) * 128)
            lanes = slice(j * 128 + d * dk, j * 128 + (d + 1) * dk)
            b.append(b_all[:, lanes])
            btot.append(btot_all[:, lanes])
            c, tt = c_all[:, lanes], t_all[:, lanes]
            qk = qk_ref[pl.ds(r0, GROUP), sl].astype(F32)
            q = qk[:, :dk] * (dk ** -0.5)
            kg.append(qk[:, dk:])
            vg.append(v_ref[pl.ds(r0, GROUP), sl])
            qe.append((q * jnp.exp(b[-1])).astype(BF16))
            qcf = q * jnp.exp(c)
            kcf = kg[-1] * jnp.exp(-c)
            kxf = kg[-1] * jnp.exp(tt - c)
            second = late_rows if d == 0 else ~late_rows
            qc.append(jnp.concatenate([jnp.where(second, 0.0, qcf), jnp.where(second, qcf, 0.0)],
                                      axis=1).astype(BF16))
            kc.append(jnp.concatenate([jnp.where(second, 0.0, kcf), jnp.where(second, kcf, kxf)],
                                      axis=1).astype(BF16))
        vg = [v.astype(BF16) for v in vg]
        attn = [jnp.where(allowed[chains[i][0]], _bdot_nt(qc[i], kc[i]), 0.0).astype(BF16) for i in rng]
        o_in = [_bdot(attn[i], vg[i]) for i in rng]
        kd = [(kg[i] * jnp.exp(btot[i] - b[i])).astype(BF16) for i in rng]
        st = [s_scr[d, j] for d, j in chains]
        o_parts = [[None] * nchunk for _ in rng]
        for cc in range(nchunk):
            rs = []
            for d, _ in chains:
                c = cc if d == 0 else nchunk - 1 - cc
                rs.append(slice(c * GLA_CHUNK, (c + 1) * GLA_CHUNK))
            qs = [_bdot_nt(qe[i][rs[i]], st[i]) for i in rng]
            vk = [_bdot_tn(vg[i][rs[i]], kd[i][rs[i]]) for i in rng]
            for i in rng:
                o_parts[i][rs[i].start // GLA_CHUNK] = o_in[i][rs[i]] + qs[i]
                st[i] = st[i] * jnp.exp(btot[i][rs[i].start:rs[i].start + 1, :]) + vk[i]
        for i, (d, j) in enumerate(chains):
            s_scr[d, j] = st[i]
            o_scr = of_scr if d == 0 else ob_scr
            o_scr[pl.ds(shared[d][0], GROUP), j * 128:(j + 1) * 128] = jnp.concatenate(o_parts[i], axis=0)
        return carry

    if ngroups == 1:
        group_step(0, 0)
    else:
        lax.fori_loop(0, ngroups, group_step, 0)

    for j in range(hp):
        sl = slice(j * 128, (j + 1) * 128)
        o = of_scr[:, sl] + ob_scr[:, sl]
        o = o * lax.rsqrt(jnp.mean(o * o, axis=-1, keepdims=True) + EPS) * og_ref[...]
        o_ref[:, sl] = (o * _silu(gate_ref[:, sl].astype(F32))).astype(BF16)
    if emit_state:
        sfin = _own_slab(sfin_ref, layer, not has_prev)
        for d in range(2):
            for j in range(hp):
                sfin[d, j] = _transpose_f32(s_scr[d, j])


def _gla(p, psm, wg, bg, out_g, s0, prev_state, layer, batch, seq, emit_state, hp):
    rows = batch * seq
    dv, dk = GLA_VAL_DIM, GLA_KEY_DIM
    w = hp * 128
    col = lambda off: pl.BlockSpec((seq, w), lambda b, h: (b, off // w + h))
    in_specs = [col(OFF_GLA_QK), col(OFF_GLA_V), col(OFF_GLA_GATE),
                pl.BlockSpec((seq, 128), lambda b, h: (b, 0)),
                pl.BlockSpec((128, w), lambda b, h: (0, h)),
                pl.BlockSpec((1, w), lambda b, h: (0, h)),
                pl.BlockSpec((1, 128), lambda b, h: (0, 0))]
    args = [p, p, p, psm, wg, bg, out_g]
    if s0 is not None:
        in_specs.append(pl.BlockSpec((None, None, 2, hp, dk, dv), lambda b, h: (b, layer, 0, h, 0, 0)))
        args.append(s0)
    out_specs = [pl.BlockSpec((seq, w), lambda b, h: (b, h))]
    out_shape = [jax.ShapeDtypeStruct((rows, GLA_HEADS * dv), BF16)]
    aliases = {}
    if emit_state:
        out_specs.append(_layer_slab((2, hp, dk, dv), lambda h: (0, h, 0, 0), layer, prev_state is None))
        out_shape.append(jax.ShapeDtypeStruct((batch, DEPTH, 2, GLA_HEADS, dk, dv), F32))
        if prev_state is not None:
            in_specs.append(pl.BlockSpec(memory_space=pl.ANY))
            args.append(prev_state)
            aliases = {len(args) - 1: 1}
    res = pl.pallas_call(
        functools.partial(_gla_kernel, hp=hp, has_s0=s0 is not None, emit_state=emit_state,
                          has_prev=prev_state is not None, layer=layer),
        grid=(batch, GLA_HEADS // hp),
        in_specs=in_specs,
        out_specs=out_specs,
        out_shape=out_shape,
        input_output_aliases=aliases,
        scratch_shapes=[pltpu.VMEM((seq, w), F32)] * 2 + [pltpu.VMEM((2, hp, dv, dk), F32)],
        compiler_params=_params("arbitrary", "arbitrary"),
        name="gla",
    )(*args)
    return res if emit_state else (res[0], None)


def _merge_kernel(x_ref, oa_ref, ob_ref, oc_ref, ma_ref, mb_ref, mc_ref, mod_ref, wbr_ref, wo_ref, o_ref,
                  *, tiles_per_seq, latent):
    r = _mod_row(pl.program_id(0), tiles_per_seq, latent)
    merged = (_sigmoid(ma_ref[...].astype(F32)) * _bdot(oa_ref[...], wbr_ref[0])
              + _sigmoid(mb_ref[...].astype(F32)) * _bdot(ob_ref[...], wbr_ref[1])
              + _sigmoid(mc_ref[...].astype(F32)) * _bdot(oc_ref[...], wbr_ref[2]))
    o_ref[...] = x_ref[...] + mod_ref[2, pl.ds(r, 1), :] * _bdot(merged, wo_ref[...])


def _merge(x, o_a, o_b, o_c, p, mod_l, w_br, w_o, layer, seq_len, latent, tm=512):
    rows = x.shape[0]
    row_blk = lambda w: pl.BlockSpec((tm, w), lambda m: (m, 0))
    gate_blk = lambda j: pl.BlockSpec((tm, D_MODEL), lambda m: (m, OFF_M // D_MODEL + j))
    return pl.pallas_call(
        functools.partial(_merge_kernel, tiles_per_seq=max(seq_len // tm, 1), latent=latent),
        grid=(rows // tm,),
        in_specs=[row_blk(D_MODEL), row_blk(512), row_blk(512), row_blk(512),
                  gate_blk(0), gate_blk(1), gate_blk(2),
                  pl.BlockSpec((6, 8, D_MODEL), lambda m: (0, 0, 0)),
                  _resident(w_br, layer), _resident(w_o, layer)],
        out_specs=row_blk(D_MODEL),
        out_shape=jax.ShapeDtypeStruct((rows, D_MODEL), F32),
        compiler_params=_params("arbitrary"),
        name="merge",
    )(x, o_a, o_b, o_c, p, p, p, mod_l, w_br, w_o)


FFN_HALO = 16


def _ffn_kernel(x_ref, xp_ref, xn_ref, mod_ref, n2_ref, wu_ref, fc_ref, wd_ref, o_ref, h_scr, act_scr,
                *, tm, tn, seq_len, tiles_per_seq, latent):
    r = _mod_row(pl.program_id(0), tiles_per_seq, latent)
    shift = mod_ref[3, pl.ds(r, 1), :]
    scale = mod_ref[4, pl.ds(r, 1), :]
    nm = lambda x: _norm_mod(x, n2_ref[...], shift, scale)
    row0 = pl.program_id(0) * tm
    h_scr[0:FFN_HALO, :] = jnp.where(row0 % seq_len == 0, 0.0, nm(xp_ref[...])).astype(BF16)
    h_scr[FFN_HALO:FFN_HALO + tm, :] = nm(x_ref[...]).astype(BF16)
    h_scr[FFN_HALO + tm:, :] = jnp.where((row0 + tm) % seq_len == 0, 0.0, nm(xn_ref[...])).astype(BF16)

    ext = tm + 2 * FFN_HALO
    inner = slice(FFN_HALO, FFN_HALO + tm)
    bounds = list(range(seq_len, tm, seq_len))
    fix = 16
    rid = lax.broadcasted_iota(jnp.int32, (2 * fix, 1), 0)

    def up(n):
        hh = h_scr[...]
        cv = slice(n * tn, (n + 1) * tn)
        cg = slice(D_FF + n * tn, D_FF + (n + 1) * tn)
        return (jnp.dot(hh, wu_ref[:, cv], preferred_element_type=F32),
                jnp.dot(hh, wu_ref[:, cg], preferred_element_type=F32))

    def taps(u):
        return pltpu.roll(u, 1, 0)[inner], u[inner], pltpu.roll(u, ext - 1, 0)[inner]

    def conv(t, f, rows=slice(None), keep_prev=None, keep_next=None):
        prev, mid, nxt = (a[rows] for a in t)
        if keep_prev is not None:
            prev = jnp.where(keep_prev, prev, 0.0)
            nxt = jnp.where(keep_next, nxt, 0.0)
        return prev * f[0:1, :] + mid * f[1:2, :] + nxt * f[2:3, :]

    nt = D_FF // tn
    u_next = up(0)
    for n in range(nt):
        u_val, u_gate = u_next
        if n + 1 < nt:
            u_next = up(n + 1)
        cols = slice(n * tn, (n + 1) * tn)
        f_val = fc_ref[:, cols]
        f_gate = fc_ref[:, D_FF + n * tn:D_FF + (n + 1) * tn]
        t_val, t_gate = taps(u_val), taps(u_gate)
        act_scr[:, cols] = (_silu(conv(t_gate, f_gate)) * conv(t_val, f_val)).astype(BF16)
        for b in bounds:
            rows = slice(b - fix, b + fix)
            kp, kn = rid != fix, rid != fix - 1
            act_scr[rows, cols] = (_silu(conv(t_gate, f_gate, rows, kp, kn))
                                   * conv(t_val, f_val, rows, kp, kn)).astype(BF16)
    down = jnp.dot(act_scr[...], wd_ref[...], preferred_element_type=F32)
    o_ref[...] = x_ref[...] + mod_ref[5, pl.ds(r, 1), :] * down


def _ffn(x, mod_l, n2, w_up, f_conv, w_dn, layer, seq_len, latent, tm=1024, tn=256):
    rows = x.shape[0]
    hb = tm // FFN_HALO
    last_hb = rows // FFN_HALO - 1
    kern = functools.partial(_ffn_kernel, tm=tm, tn=tn, seq_len=seq_len, tiles_per_seq=max(seq_len // tm, 1),
                             latent=latent)
    return pl.pallas_call(
        kern,
        grid=(rows // tm,),
        in_specs=[pl.BlockSpec((tm, D_MODEL), lambda m: (m, 0)),
                  pl.BlockSpec((FFN_HALO, D_MODEL), lambda m: (jnp.maximum(m * hb - 1, 0), 0)),
                  pl.BlockSpec((FFN_HALO, D_MODEL), lambda m: (jnp.minimum((m + 1) * hb, last_hb), 0)),
                  pl.BlockSpec((6, 8, D_MODEL), lambda m: (0, 0, 0)),
                  pl.BlockSpec((1, D_MODEL), lambda m: (0, 0)),
                  _resident(w_up, layer), _resident(f_conv, layer), _resident(w_dn, layer)],
        out_specs=pl.BlockSpec((tm, D_MODEL), lambda m: (m, 0)),
        out_shape=jax.ShapeDtypeStruct((rows, D_MODEL), F32),
        scratch_shapes=[pltpu.VMEM((tm + 2 * FFN_HALO, D_MODEL), BF16), pltpu.VMEM((tm, D_FF), BF16)],
        compiler_params=_params("arbitrary"),
        name="ffn",
    )(x, x, x, mod_l, n2, w_up, f_conv, w_dn)


def _w_in_moves():
    src = np.cumsum([0, 512, 512, 512, 512, 512, 512, 512, 8, 8, 256, 256, 512, 512, 32, 1024, 1024, 1024])
    dk = GLA_KEY_DIM
    moves = [(0, 0, int(src[7]))]
    for h in range(GLA_HEADS):
        moves.append((OFF_GLA_QK + 2 * dk * h, int(src[9]) + dk * h, dk))
        moves.append((OFF_GLA_QK + 2 * dk * h + dk, int(src[10]) + dk * h, dk))
    moves += [(OFF_GLA_V, int(src[11]), 512), (OFF_GLA_GATE, int(src[12]), 512), (OFF_M, int(src[14]), 3072),
              (OFF_SMALL + SMALL_BETA, int(src[7]), 8), (OFF_SMALL + SMALL_DECAY, int(src[8]), 8),
              (OFF_SMALL + SMALL_LR, int(src[13]), 32)]
    return moves


def _permute_kernel(w_ref, o_ref):
    small = []
    for dst, src, width in _w_in_moves():
        if dst >= OFF_SMALL:
            small.append(w_ref[src:src + width, :])
        else:
            o_ref[dst:dst + width, :] = w_ref[src:src + width, :].astype(BF16)
    used = sum(s.shape[0] for s in small)
    small.append(jnp.zeros((PROJ_PAD - OFF_SMALL - used, o_ref.shape[1]), F32))
    o_ref[OFF_SMALL:, :] = jnp.concatenate(small, axis=0).astype(BF16)


def _permute_w_in(w_in_t, tc=256):
    proj = w_in_t.shape[1]
    return pl.pallas_call(
        _permute_kernel,
        grid=(DEPTH, D_MODEL // tc),
        in_specs=[pl.BlockSpec((None, proj, tc), lambda l, c: (l, 0, c))],
        out_specs=pl.BlockSpec((None, PROJ_PAD, tc), lambda l, c: (l, 0, c)),
        out_shape=jax.ShapeDtypeStruct((DEPTH, PROJ_PAD, D_MODEL), BF16),
        compiler_params=_params("arbitrary", "arbitrary"),
        name="permute_w_in",
    )(w_in_t)


def _gla_gate_weights(gla_w, gla_b):
    r, dk = GLA_GATE_RANK, GLA_KEY_DIM
    w = gla_w.reshape(DEPTH, 2, r, GLA_HEADS, dk)
    wg = jnp.zeros((DEPTH, GLA_HEADS, 128, 2, dk), F32)
    for d in range(2):
        wg = wg.at[:, :, SMALL_LR + d * r:SMALL_LR + (d + 1) * r, d, :].set(jnp.transpose(w[:, d], (0, 2, 1, 3)))
    bg = jnp.transpose(gla_b.reshape(DEPTH, 2, GLA_HEADS, dk), (0, 2, 1, 3)).reshape(DEPTH, 1, GLA_HEADS * 2 * dk)
    wg = jnp.transpose(wg, (0, 2, 1, 3, 4)).reshape(DEPTH, 128, GLA_HEADS * 2 * dk)
    return wg.astype(BF16), bg.astype(F32)


def _lane_row(values, offset):
    n = values.shape[-1]
    return jnp.zeros((DEPTH, 1, 128), F32).at[:, 0, offset:offset + n].set(values.astype(F32))


def kernel(x_prompt, x_sample, cache_k, cache_v, state_dn, state_gla, c, c_ctx, w_ada, b_ada, norm1, w_in,
           na_q_norm, na_k_norm, na_rpb, dn_conv, dn_a_log, dn_dt_bias, dn_out_norm, gla_w_gate, gla_b_gate,
           gla_out_norm, w_branch, w_out, norm2, w_up, ffn_conv, w_down):
    batch, seq, _ = x_prompt.shape
    dbatch, dseq, _ = x_sample.shape
    past = cache_k.shape[2]

    cvec8 = jnp.concatenate([c_ctx[None, :], c, jnp.zeros((8 - 1 - dbatch, D_MODEL), F32)], axis=0)
    mod = _adaln(cvec8, w_ada, b_ada)

    w_in_p = _permute_w_in(jnp.swapaxes(w_in, 1, 2))
    w_br = w_branch.astype(BF16)
    w_o = w_out.astype(BF16)
    w_up_b = w_up.astype(BF16)
    w_dn_b = w_down.astype(BF16)
    tiles = _na_bias_tiles(na_rpb)
    wg, bg = _gla_gate_weights(gla_w_gate, gla_b_gate)
    alog128 = _lane_row(dn_a_log.reshape(DEPTH, 2 * DN_HEADS), SMALL_DECAY)
    dtb128 = _lane_row(dn_dt_bias.reshape(DEPTH, 2 * DN_HEADS), SMALL_DECAY)
    ck = jnp.transpose(cache_k, (0, 1, 3, 4, 2))
    cv = jnp.transpose(cache_v, (0, 1, 3, 4, 2))

    y_p = x_prompt.reshape(batch * seq, D_MODEL)
    y_s = x_sample.reshape(dbatch * dseq, D_MODEL)
    new_kv, new_dn, new_gla = None, None, None
    for l in range(DEPTH):
        qg, kg = na_q_norm[l][None, :], na_k_norm[l][None, :]
        n1, n2 = norm1[l][None, :], norm2[l][None, :]
        dn_g, gla_g = dn_out_norm[l][None, :], gla_out_norm[l][None, :]
        for latent in (False, True):
            x = y_s if latent else y_p
            b_, t_ = (dbatch, dseq) if latent else (batch, seq)
            p, psm = _inproj(x, mod[l], n1, w_in_p, l, t_, latent)
            if latent:
                o_a = _lat_attn(p, ck, cv, tiles, qg, kg, l, b_, t_)
                o_b, _ = _deltanet(p, psm, dn_conv[l], alog128[l], dtb128[l], dn_g, state_dn, None, l, b_, t_,
                                   False, 2)
                o_c, _ = _gla(p, psm, wg[l], bg[l], gla_g, state_gla, None, l, b_, t_, False, 2)
            else:
                o_a, k_all, v_all = _ctx_attn(p, qg, kg, new_kv, l, b_, t_)
                new_kv = (k_all, v_all)
                o_b, new_dn = _deltanet(p, psm, dn_conv[l], alog128[l], dtb128[l], dn_g, None, new_dn, l, b_, t_,
                                        True, DN_HEADS)
                o_c, new_gla = _gla(p, psm, wg[l], bg[l], gla_g, None, new_gla, l, b_, t_, True, GLA_HEADS)
            x = _merge(x, o_a, o_b, o_c, p, mod[l], w_br, w_o, l, t_, latent)
            x = _ffn(x, mod[l], n2, w_up_b, ffn_conv, w_dn_b, l, t_, latent)
            if latent:
                y_s = x
            else:
                y_p = x
    cache_shape = (batch, DEPTH, seq, NA_HEADS, NA_HEAD_DIM)
    return (y_p.reshape(batch, seq, D_MODEL), y_s.reshape(dbatch, dseq, D_MODEL),
            new_kv[0].reshape(cache_shape), new_kv[1].reshape(cache_shape), new_dn, new_gla)
```

```python
import functools

import numpy as np
import jax
import jax.numpy as jnp
from jax import lax
from jax.experimental import pallas as pl
from jax.experimental.pallas import tpu as pltpu

F32 = jnp.float32
BF16 = jnp.bfloat16

D_MODEL = 1024
DEPTH = 2
GRID_W = 64
EPS = 1e-6
NA_HEADS = 8
NA_HEAD_DIM = 64
NA_WIN_H = 8
NA_WIN_W = 16
DN_HEADS = 4
DN_HEAD_DIM = 128
DN_CHUNK = 64
GLA_HEADS = 4
GLA_KEY_DIM = 64
GLA_VAL_DIM = 128
GLA_GATE_RANK = 16
GLA_TAU = 16.0
GLA_CHUNK = 32
D_FF = 2816

OFF_NA_Q, OFF_NA_K, OFF_NA_V = 0, 512, 1024
OFF_DN_Q, OFF_DN_K, OFF_DN_V, OFF_DN_GATE = 1536, 2048, 2560, 3072
OFF_GLA_QK, OFF_GLA_V, OFF_GLA_GATE = 3584, 4096, 4608
OFF_M = 5120
OFF_SMALL = 8192
PROJ_PAD = 8320
SMALL_BETA, SMALL_DECAY, SMALL_LR = 0, 8, 16

GROUP = 256
NEG = -1e30
VMEM_LIMIT = 56 * 1024 * 1024

NA_QROWS = 4
NA_KROWS = 12


def _bdot(a, b):
    return jnp.dot(a.astype(BF16), b.astype(BF16), preferred_element_type=F32)


def _bdot_nt(a, b):
    return lax.dot_general(a.astype(BF16), b.astype(BF16), (((1,), (1,)), ((), ())), preferred_element_type=F32)


def _bdot_tn(a, b):
    return lax.dot_general(a.astype(BF16), b.astype(BF16), (((0,), (0,)), ((), ())), preferred_element_type=F32)


def _split3(x):
    hi = x.astype(BF16)
    r = x - hi.astype(F32)
    mid = r.astype(BF16)
    lo = (r - mid.astype(F32)).astype(BF16)
    return hi, mid, lo


def _dot_mask_f32(t, x):
    hi, mid, lo = _split3(x)
    d = lambda p: jnp.dot(t, p, preferred_element_type=F32)
    return d(hi) + d(mid) + d(lo)


def _dot3(a, b):
    ah = a.astype(BF16)
    al = (a - ah.astype(F32)).astype(BF16)
    bh = b.astype(BF16)
    bl = (b - bh.astype(F32)).astype(BF16)
    d = lambda p, q: jnp.dot(p, q, preferred_element_type=F32)
    return d(ah, bh) + d(ah, bl) + d(al, bh)


def _sigmoid(x):
    return 1.0 / (1.0 + jnp.exp(-x))


def _silu(x):
    return x * _sigmoid(x)


def _softplus(x):
    return jnp.maximum(x, 0.0) + jnp.log(1.0 + jnp.exp(-jnp.abs(x)))


def _params(*sem):
    return pltpu.CompilerParams(dimension_semantics=sem, vmem_limit_bytes=VMEM_LIMIT)


def _adaln_kernel(c_ref, w_ref, b_ref, o_ref):
    o_ref[...] = _bdot(_silu(c_ref[...]), w_ref[...]) + b_ref[...]


def _adaln(cvec8, w_ada, b_ada):
    return pl.pallas_call(
        _adaln_kernel,
        grid=(DEPTH, 6),
        in_specs=[pl.BlockSpec((8, D_MODEL), lambda l, j: (0, 0)),
                  pl.BlockSpec((None, D_MODEL, D_MODEL), lambda l, j: (l, 0, j)),
                  pl.BlockSpec((None, None, 1, D_MODEL), lambda l, j: (l, j, 0, 0))],
        out_specs=pl.BlockSpec((None, None, 8, D_MODEL), lambda l, j: (l, j, 0, 0)),
        out_shape=jax.ShapeDtypeStruct((DEPTH, 6, 8, D_MODEL), F32),
        compiler_params=_params("arbitrary", "arbitrary"),
        name="adaln",
    )(cvec8, w_ada, b_ada.reshape(DEPTH, 6, 1, D_MODEL))


def _mod_row(tile, tiles_per_seq, latent):
    return (1 + tile // tiles_per_seq) if latent else 0


def _norm_mod(x, gain, shift, scale):
    y = x * lax.rsqrt(jnp.mean(x * x, axis=-1, keepdims=True) + EPS) * gain
    return y * (1.0 + scale) + shift


def _inproj_kernel(x_ref, mod_ref, n1_ref, w_ref, o_ref, osm_ref, *, tiles_per_seq, latent, tn):
    r = _mod_row(pl.program_id(0), tiles_per_seq, latent)
    h = _norm_mod(x_ref[...], n1_ref[...], mod_ref[0, pl.ds(r, 1), :], mod_ref[1, pl.ds(r, 1), :]).astype(BF16)
    for n in range(OFF_SMALL // tn):
        cols = slice(n * tn, (n + 1) * tn)
        o_ref[:, cols] = _bdot_nt(h, w_ref[cols, :]).astype(BF16)
    osm_ref[...] = _bdot_nt(h, w_ref[OFF_SMALL:, :])


def _resident(stacked, layer):
    shape = stacked.shape[1:]
    zeros = (0,) * len(shape)
    return pl.BlockSpec((None,) + shape, lambda *g: (layer,) + zeros, pipeline_mode=pl.Buffered(1))


def _inproj(x, mod_l, n1, w_in_p, layer, seq_len, latent, tm=512, tn=2048):
    rows = x.shape[0]
    kern = functools.partial(_inproj_kernel, tiles_per_seq=max(seq_len // tm, 1), latent=latent, tn=tn)
    return pl.pallas_call(
        kern,
        grid=(rows // tm,),
        in_specs=[pl.BlockSpec((tm, D_MODEL), lambda m: (m, 0)),
                  pl.BlockSpec((6, 8, D_MODEL), lambda m: (0, 0, 0)),
                  pl.BlockSpec((1, D_MODEL), lambda m: (0, 0)),
                  _resident(w_in_p, layer)],
        out_specs=[pl.BlockSpec((tm, OFF_SMALL), lambda m: (m, 0)),
                   pl.BlockSpec((tm, PROJ_PAD - OFF_SMALL), lambda m: (m, 0))],
        out_shape=[jax.ShapeDtypeStruct((rows, OFF_SMALL), BF16),
                   jax.ShapeDtypeStruct((rows, PROJ_PAD - OFF_SMALL), F32)],
        compiler_params=_params("arbitrary"),
        name="inproj",
    )(x, mod_l, n1, w_in_p)


def _head_rms(x, gain):
    return x * lax.rsqrt(jnp.mean(x * x, axis=-1, keepdims=True) + EPS) * gain


def _head_slices():
    return [slice(h * NA_HEAD_DIM, (h + 1) * NA_HEAD_DIM) for h in range(NA_HEADS)]


def _ctx_attn_kernel(*refs, has_prev, layer):
    q_ref, k_ref, v_ref, qg_ref, kg_ref = refs[:5]
    o_ref, ka_ref, va_ref = refs[-3:]
    ka_ref = _own_slab(ka_ref, layer, not has_prev)
    va_ref = _own_slab(va_ref, layer, not has_prev)
    hs = _head_slices()
    qn = [_head_rms(q_ref[:, sl].astype(F32), qg_ref[...]) * (NA_HEAD_DIM ** -0.5) for sl in hs]
    kn = [_head_rms(k_ref[:, sl].astype(F32), kg_ref[...]) for sl in hs]
    for sl, k in zip(hs, kn):
        ka_ref[:, sl] = k
    s = [_bdot_nt(q, k) for q, k in zip(qn, kn)]
    p = [jnp.exp(x - jnp.max(x, axis=-1, keepdims=True)) for x in s]
    pv = [_bdot(x, v_ref[:, sl]) for x, sl in zip(p, hs)]
    for sl, x, y in zip(hs, p, pv):
        o_ref[:, sl] = (y / jnp.sum(x, axis=-1, keepdims=True)).astype(BF16)
    va_ref[...] = v_ref[...].astype(F32)


def _layer_slab(shape_tail, index_tail, layer, first):
    if first:
        return pl.BlockSpec((None, DEPTH) + shape_tail, lambda b, *g: (b, 0) + index_tail(*g))
    return pl.BlockSpec((None, None) + shape_tail, lambda b, *g: (b, layer) + index_tail(*g))


def _own_slab(ref, layer, first):
    if not first:
        return ref
    for other in range(DEPTH):
        if other != layer:
            ref[other] = jnp.zeros(ref.shape[1:], ref.dtype)
    return ref.at[layer]


def _ctx_attn(p_ctx, qg, kg, prev_kv, layer, batch, seq):
    rows = batch * seq
    blk = lambda j: pl.BlockSpec((seq, 512), lambda b: (b, j))
    cache = jax.ShapeDtypeStruct((batch, DEPTH, seq, 512), F32)
    cache_spec = _layer_slab((seq, 512), lambda: (0, 0), layer, prev_kv is None)
    in_specs = [blk(OFF_NA_Q // 512), blk(OFF_NA_K // 512), blk(OFF_NA_V // 512),
                pl.BlockSpec((1, NA_HEAD_DIM), lambda b: (0, 0)),
                pl.BlockSpec((1, NA_HEAD_DIM), lambda b: (0, 0))]
    args = [p_ctx, p_ctx, p_ctx, qg, kg]
    aliases = {}
    if prev_kv is not None:
        in_specs += [pl.BlockSpec(memory_space=pl.ANY)] * 2
        args += list(prev_kv)
        aliases = {5: 1, 6: 2}
    return pl.pallas_call(
        functools.partial(_ctx_attn_kernel, has_prev=prev_kv is not None, layer=layer),
        grid=(batch,),
        in_specs=in_specs,
        out_specs=[blk(0), cache_spec, cache_spec],
        out_shape=[jax.ShapeDtypeStruct((rows, 512), BF16), cache, cache],
        input_output_aliases=aliases,
        compiler_params=_params("arbitrary"),
        name="ctx_attn",
    )(*args)


NA_NDR = 2 * NA_WIN_H - 1


def _na_bias_tiles(rpb):
    qc = np.arange(GRID_W)
    cs = np.clip(qc - NA_WIN_W // 2, 0, GRID_W - NA_WIN_W)
    col_ok = (qc[None, :] >= cs[:, None]) & (qc[None, :] < cs[:, None] + NA_WIN_W)
    nd = 2 * NA_WIN_W - 1
    dc = np.clip(qc[None, :] - qc[:, None] + NA_WIN_W - 1, 0, nd - 1)
    sel = (np.arange(nd)[:, None, None] == dc[None]).astype(np.float32).reshape(nd, GRID_W * GRID_W)
    t = jnp.einsum('lhrd,dn->lhrn', rpb.astype(F32), jnp.asarray(sel), precision=lax.Precision.HIGHEST)
    t = jnp.where(col_ok, t.reshape(rpb.shape[:3] + (GRID_W, GRID_W)), NEG)
    return jnp.concatenate([t, jnp.full(t.shape[:2] + (1, GRID_W, GRID_W), NEG, F32)], axis=2)


def _lat_attn_kernel(q_ref, k_ref, v_ref, ck_ref, cv_ref, tile_ref, qg_ref, kg_ref, o_ref, kn_scr, bias_scr, *, rows):
    m = pl.program_id(1)
    nblk = rows // NA_QROWS
    hs = _head_slices()

    @pl.when(m == 0)
    def _():
        for sl in hs:
            kn_scr[:, sl] = _head_rms(k_ref[:, sl].astype(F32), kg_ref[...]).astype(BF16)

    krow0 = jnp.clip(NA_QROWS * m - NA_WIN_H // 2, 0, rows - NA_KROWS)

    @pl.when((m == 0) | (m == 1) | (m == nblk - 1))
    def _():
        for i in range(NA_QROWS):
            r = NA_QROWS * m + i
            rs = jnp.clip(r - NA_WIN_H // 2, 0, rows - NA_WIN_H)
            for j in range(NA_KROWS):
                kr = krow0 + j
                idx = jnp.where((kr >= rs) & (kr < rs + NA_WIN_H), kr - r + NA_WIN_H - 1, NA_NDR)
                for h in range(NA_HEADS):
                    bias_scr[h, i * GRID_W:(i + 1) * GRID_W, j * GRID_W:(j + 1) * GRID_W] = tile_ref[h, idx]

    kstart = pl.multiple_of(krow0 * GRID_W, 256)
    nk = NA_KROWS * GRID_W
    qn = [(_head_rms(q_ref[:, sl].astype(F32), qg_ref[...]) * (NA_HEAD_DIM ** -0.5)).astype(BF16) for sl in hs]
    s_loc = [_bdot_nt(q, kn_scr[pl.ds(kstart, nk), sl]) + bias_scr[h] for h, (q, sl) in enumerate(zip(qn, hs))]
    s_ctx = [_bdot(q, ck_ref[h]) for h, q in enumerate(qn)]
    mx = [jnp.maximum(jnp.max(a, axis=-1, keepdims=True), jnp.max(c, axis=-1, keepdims=True))
          for a, c in zip(s_loc, s_ctx)]
    p_loc = [jnp.exp(a - x) for a, x in zip(s_loc, mx)]
    p_ctx = [jnp.exp(c - x) for c, x in zip(s_ctx, mx)]
    o_loc = [_bdot(p, v_ref[pl.ds(kstart, nk), sl]) for p, sl in zip(p_loc, hs)]
    o_ctx = [_bdot_nt(p, cv_ref[h]) for h, p in enumerate(p_ctx)]
    for h, sl in enumerate(hs):
        den = jnp.sum(p_loc[h], axis=-1, keepdims=True) + jnp.sum(p_ctx[h], axis=-1, keepdims=True)
        o_ref[:, sl] = ((o_loc[h] + o_ctx[h]) / den).astype(BF16)


def _lat_attn(p_lat, cache_k, cache_v, tiles, qg, kg, layer, batch, seq):
    rows = seq // GRID_W
    nblk = rows // NA_QROWS
    qtok = NA_QROWS * GRID_W
    past = cache_k.shape[-1]
    cache_spec = pl.BlockSpec((None, None, NA_HEADS, NA_HEAD_DIM, past), lambda b, m: (b, layer, 0, 0, 0))
    return pl.pallas_call(
        functools.partial(_lat_attn_kernel, rows=rows),
        grid=(batch, nblk),
        in_specs=[pl.BlockSpec((qtok, 512), lambda b, m: (b * nblk + m, OFF_NA_Q // 512)),
                  pl.BlockSpec((seq, 512), lambda b, m: (b, OFF_NA_K // 512)),
                  pl.BlockSpec((seq, 512), lambda b, m: (b, OFF_NA_V // 512)),
                  cache_spec, cache_spec,
                  pl.BlockSpec((None, NA_HEADS, NA_NDR + 1, GRID_W, GRID_W), lambda b, m: (layer, 0, 0, 0, 0)),
                  pl.BlockSpec((1, NA_HEAD_DIM), lambda b, m: (0, 0)),
                  pl.BlockSpec((1, NA_HEAD_DIM), lambda b, m: (0, 0))],
        out_specs=pl.BlockSpec((qtok, 512), lambda b, m: (b * nblk + m, 0)),
        out_shape=jax.ShapeDtypeStruct((batch * seq, 512), BF16),
        scratch_shapes=[pltpu.VMEM((seq, 512), BF16),
                        pltpu.VMEM((NA_HEADS, qtok, NA_KROWS * GRID_W), F32)],
        compiler_params=_params("arbitrary", "arbitrary"),
        name="lat_attn",
    )(p_lat, p_lat, p_lat, cache_k, cache_v, tiles, qg, kg)


def _group_masks(chunk):
    r = lax.broadcasted_iota(jnp.int32, (GROUP, GROUP), 0)
    c = lax.broadcasted_iota(jnp.int32, (GROUP, GROUP), 1)
    same = (r // chunk) == (c // chunk)
    return (same & (c <= r), same & (c >= r), same & (c < r), same & (c > r), same)


def _halving_masks(chunk, lower):
    r = lax.broadcasted_iota(jnp.int32, (GROUP, GROUP), 0)
    c = lax.broadcasted_iota(jnp.int32, (GROUP, GROUP), 1)
    if not lower:
        r, c = c, r
    masks = []
    s = 1
    while s < chunk:
        m = ((r // (2 * s)) == (c // (2 * s))) & ((r // s) % 2 == 1) & ((c // s) % 2 == 0)
        masks.append(jnp.where(m, 1.0, 0.0).astype(BF16))
        s *= 2
    return masks


def _group_eye():
    r = lax.broadcasted_iota(jnp.int32, (GROUP, GROUP), 0)
    c = lax.broadcasted_iota(jnp.int32, (GROUP, GROUP), 1)
    return jnp.where(r == c, 1.0, 0.0).astype(BF16)


def _sel_col(a, lane):
    idx = lax.broadcasted_iota(jnp.int32, a.shape, 1)
    return jnp.sum(jnp.where(idx == lane, a, 0.0), axis=1, keepdims=True)


def _sel_row(a, row):
    idx = lax.broadcasted_iota(jnp.int32, a.shape, 0)
    return jnp.sum(jnp.where(idx == row, a, 0.0), axis=0, keepdims=True)


def _conv3_silu(x, w):
    t = x.shape[0]
    row = lax.broadcasted_iota(jnp.int32, x.shape, 0)
    xm = jnp.where(row == 0, 0.0, pltpu.roll(x, 1, 0))
    xp = jnp.where(row == t - 1, 0.0, pltpu.roll(x, t - 1, 0))
    return _silu(xm * w[0:1, :] + x * w[1:2, :] + xp * w[2:3, :])


def _deltanet_kernel(*refs, hp, has_s0, emit_state, has_prev, layer):
    (q_ref, k_ref, v_ref, gate_ref, sm_ref, cq_ref, ck_ref, cv_ref, alog_ref, dtb_ref, og_ref), refs = refs[:11], refs[11:]
    if has_s0:
        s0_ref, refs = refs[0], refs[1:]
    if has_prev:
        refs = refs[1:]
    o_ref, refs = refs[0], refs[1:]
    if emit_state:
        sfin_ref, refs = refs[0], refs[1:]
    qn_scr, kn_scr, vc_scr, of_scr, ob_scr, s_scr = refs

    head0 = pl.program_id(1) * hp
    seq = q_ref.shape[0]
    hd = DN_HEAD_DIM
    ngroups = seq // GROUP
    nchunk = GROUP // DN_CHUNK

    q = _conv3_silu(q_ref[...].astype(F32), cq_ref[...])
    k = _conv3_silu(k_ref[...].astype(F32), ck_ref[...])
    vc_scr[...] = _conv3_silu(v_ref[...].astype(F32), cv_ref[...])
    for j in range(hp):
        sl = slice(j * hd, (j + 1) * hd)
        qj, kj = q[:, sl], k[:, sl]
        qn_scr[:, sl] = qj * lax.rsqrt(jnp.sum(qj * qj, axis=-1, keepdims=True) + EPS) * (hd ** -0.5)
        kn_scr[:, sl] = kj * lax.rsqrt(jnp.sum(kj * kj, axis=-1, keepdims=True) + EPS)

    if has_s0:
        s_scr[...] = s0_ref[...]
    else:
        s_scr[...] = jnp.zeros_like(s_scr)

    m_fi, m_bi, m_fs, m_bs, m_same = _group_masks(DN_CHUNK)
    incl = (m_fi, m_bi)
    strict = (m_fs, m_bs)
    tri = (m_fi.astype(BF16), m_bi.astype(BF16))
    ones_bd = m_same.astype(BF16)
    halving = (_halving_masks(DN_CHUNK, True), _halving_masks(DN_CHUNK, False))

    eye = _group_eye()
    chains = [(d, j) for d in range(2) for j in range(hp)]
    nc = len(chains)

    def group_step(t, carry):
        shared = []
        g128 = None
        for d in range(2):
            gi = t if d == 0 else ngroups - 1 - t
            r0 = pl.multiple_of(gi * GROUP, GROUP)
            if g128 is None or ngroups > 1:
                sm = sm_ref[pl.ds(r0, GROUP), :]
                g128 = -jnp.exp(alog_ref[...]) * _softplus(sm + dtb_ref[...])
                gtot = _dot_mask_f32(ones_bd, g128)
                beta128 = _sigmoid(sm)
            gcum = _dot_mask_f32(tri[d], g128)
            shared.append((r0, gcum, gtot, gcum.T, beta128))

        qg, kg, vg, beta, gc_col, gt_col, dec = [], [], [], [], [], [], []
        for d, j in chains:
            r0, gcum, gtot, gcum_t, beta128 = shared[d]
            sl = slice(j * hd, (j + 1) * hd)
            lane_g = SMALL_DECAY + DN_HEADS * d + head0 + j
            gc_col.append(_sel_col(gcum, lane_g))
            gt_col.append(_sel_col(gtot, lane_g))
            beta.append(_sel_col(beta128, SMALL_BETA + DN_HEADS * d + head0 + j))
            dec.append(jnp.exp(jnp.where(incl[d], gc_col[-1] - _sel_row(gcum_t, lane_g), NEG)))
            qg.append(qn_scr[pl.ds(r0, GROUP), sl])
            kg.append(kn_scr[pl.ds(r0, GROUP), sl])
            vg.append(vc_scr[pl.ds(r0, GROUP), sl])
        rng = range(nc)
        kk = [_bdot_nt(kg[i], kg[i]) for i in rng]
        qk = [_bdot_nt(qg[i], kg[i]) for i in rng]
        low = [jnp.where(strict[chains[i][0]], beta[i] * kk[i] * dec[i], 0.0).astype(BF16) for i in rng]
        attn = [(qk[i] * dec[i]).astype(BF16) for i in rng]
        e_gc = [jnp.exp(gc_col[i]) for i in rng]
        x = [jnp.concatenate([vg[i] * beta[i], kg[i] * (beta[i] * e_gc[i])], axis=1).astype(BF16) for i in rng]
        inv = [eye - low[i] * halving[chains[i][0]][0] for i in rng]
        for lvl in range(1, len(halving[0])):
            tmp = [_bdot(inv[i], low[i] * halving[chains[i][0]][lvl]).astype(BF16) for i in rng]
            upd = [_bdot(tmp[i], inv[i]).astype(BF16) for i in rng]
            inv = [inv[i] - upd[i] for i in rng]
        x = [_bdot(inv[i], x[i]) for i in rng]
        w = [x[i][:, hd:].astype(BF16) for i in rng]
        qd = [(qg[i] * e_gc[i]).astype(BF16) for i in rng]
        kd = [(kg[i] * jnp.exp(gt_col[i] - gc_col[i])).astype(BF16) for i in rng]
        s = [s_scr[d, j] for d, j in chains]
        o_parts = [[None] * nchunk for _ in rng]
        for cc in range(nchunk):
            rs = []
            for d, _ in chains:
                c = cc if d == 0 else nchunk - 1 - cc
                rs.append(slice(c * DN_CHUNK, (c + 1) * DN_CHUNK))
            ws = [_bdot(w[i][rs[i]], s[i]) for i in rng]
            qs = [_bdot(qd[i][rs[i]], s[i]) for i in rng]
            v_new = [x[i][rs[i], :hd] - ws[i] for i in rng]
            av = [_bdot(attn[i][rs[i], rs[i]], v_new[i]) for i in rng]
            kv = [_bdot_tn(kd[i][rs[i]], v_new[i]) for i in rng]
            for i in rng:
                o_parts[i][rs[i].start // DN_CHUNK] = qs[i] + av[i]
                s[i] = s[i] * jnp.exp(gt_col[i][rs[i].start:rs[i].start + 1, :]) + kv[i]
        for i, (d, j) in enumerate(chains):
            s_scr[d, j] = s[i]
            o_scr = of_scr if d == 0 else ob_scr
            o_scr[pl.ds(shared[d][0], GROUP), j * hd:(j + 1) * hd] = jnp.concatenate(o_parts[i], axis=0)
        return carry

    if ngroups == 1:
        group_step(0, 0)
    else:
        lax.fori_loop(0, ngroups, group_step, 0)

    for j in range(hp):
        sl = slice(j * hd, (j + 1) * hd)
        o = of_scr[:, sl] + ob_scr[:, sl]
        o = o * lax.rsqrt(jnp.mean(o * o, axis=-1, keepdims=True) + EPS) * og_ref[...]
        o_ref[:, sl] = (o * _silu(gate_ref[:, sl].astype(F32))).astype(BF16)
    if emit_state:
        _own_slab(sfin_ref, layer, not has_prev)[...] = s_scr[...]


def _deltanet(p, psm, conv_w, alog128, dtb128, out_g, s0, prev_state, layer, batch, seq, emit_state, hp):
    rows = batch * seq
    hd = DN_HEAD_DIM
    w = hp * hd
    col = lambda off: pl.BlockSpec((seq, w), lambda b, h: (b, off // w + h))
    cw = lambda part: pl.BlockSpec((3, w), lambda b, h: (0, part * (DN_HEADS // hp) + h))
    row128 = pl.BlockSpec((1, 128), lambda b, h: (0, 0))
    in_specs = [col(OFF_DN_Q), col(OFF_DN_K), col(OFF_DN_V), col(OFF_DN_GATE),
                pl.BlockSpec((seq, 128), lambda b, h: (b, 0)),
                cw(0), cw(1), cw(2), row128, row128, row128]
    args = [p, p, p, p, psm, conv_w, conv_w, conv_w, alog128, dtb128, out_g]
    if s0 is not None:
        in_specs.append(pl.BlockSpec((None, None, 2, hp, hd, hd), lambda b, h: (b, layer, 0, h, 0, 0)))
        args.append(s0)
    out_specs = [pl.BlockSpec((seq, w), lambda b, h: (b, h))]
    out_shape = [jax.ShapeDtypeStruct((rows, DN_HEADS * hd), BF16)]
    aliases = {}
    if emit_state:
        out_specs.append(_layer_slab((2, hp, hd, hd), lambda h: (0, h, 0, 0), layer, prev_state is None))
        out_shape.append(jax.ShapeDtypeStruct((batch, DEPTH, 2, DN_HEADS, hd, hd), F32))
        if prev_state is not None:
            in_specs.append(pl.BlockSpec(memory_space=pl.ANY))
            args.append(prev_state)
            aliases = {len(args) - 1: 1}
    res = pl.pallas_call(
        functools.partial(_deltanet_kernel, hp=hp, has_s0=s0 is not None, emit_state=emit_state,
                          has_prev=prev_state is not None, layer=layer),
        grid=(batch, DN_HEADS // hp),
        in_specs=in_specs,
        out_specs=out_specs,
        out_shape=out_shape,
        input_output_aliases=aliases,
        scratch_shapes=[pltpu.VMEM((seq, w), F32)] * 5 + [pltpu.VMEM((2, hp, hd, hd), F32)],
        compiler_params=_params("arbitrary", "arbitrary"),
        name="deltanet",
    )(*args)
    return res if emit_state else (res[0], None)


def _transpose_f32(x):
    n = x.shape[1]
    r = lax.broadcasted_iota(jnp.int32, (n, n), 0)
    c = lax.broadcasted_iota(jnp.int32, (n, n), 1)
    eye = jnp.where(r == c, 1.0, 0.0).astype(BF16)
    d = lambda p: lax.dot_general(eye, p, (((1,), (1,)), ((), ())), preferred_element_type=F32)
    hi, mid, lo = _split3(x)
    return d(hi) + d(mid) + d(lo)


def _gla_kernel(*refs, hp, has_s0, emit_state, has_prev, layer):
    (qk_ref, v_ref, gate_ref, sm_ref, wg_ref, bg_ref, og_ref), refs = refs[:7], refs[7:]
    if has_s0:
        s0_ref, refs = refs[0], refs[1:]
    if has_prev:
        refs = refs[1:]
    o_ref, refs = refs[0], refs[1:]
    if emit_state:
        sfin_ref, refs = refs[0], refs[1:]
    of_scr, ob_scr, s_scr = refs

    seq = qk_ref.shape[0]
    ngroups = seq // GROUP
    nchunk = GROUP // GLA_CHUNK
    dk = GLA_KEY_DIM

    for d in range(2):
        for j in range(hp):
            if has_s0:
                s_scr[d, j] = _transpose_f32(s0_ref[d, j])
            else:
                s_scr[d, j] = jnp.zeros(s_scr.shape[2:], F32)

    half = GLA_CHUNK // 2
    h_fi, h_bi, _, _, h_same = _group_masks(half)
    _, _, _, _, c_same = _group_masks(GLA_CHUNK)
    row = lax.broadcasted_iota(jnp.int32, (GROUP, GROUP), 0)
    col = lax.broadcasted_iota(jnp.int32, (GROUP, GROUP), 1)
    row_late = (row // half) % 2 == 1
    col_late = (col // half) % 2 == 1
    allowed = (h_fi | (c_same & row_late & ~col_late), h_bi | (c_same & ~row_late & col_late))
    tri = (h_fi.astype(BF16), h_bi.astype(BF16))
    ones_bd = h_same.astype(BF16)
    late_rows = (lax.broadcasted_iota(jnp.int32, (GROUP, 1), 0) // half) % 2 == 1

    chains = [(d, j) for d in range(2) for j in range(hp)]
    rng = range(len(chains))

    def group_step(t, carry):
        shared = []
        log_a = None
        for d in range(2):
            gi = t if d == 0 else ngroups - 1 - t
            r0 = pl.multiple_of(gi * GROUP, GROUP)
            if log_a is None or ngroups > 1:
                gl = _bdot(sm_ref[pl.ds(r0, GROUP), :], wg_ref[...]) + bg_ref[...]
                log_a = -_softplus(-gl) / GLA_TAU
                t_all = _dot_mask_f32(ones_bd, log_a)
                t_prev = pltpu.roll(t_all, half, 0)
                t_next = pltpu.roll(t_all, GROUP - half, 0)
            c_all = _dot_mask_f32(tri[d], log_a)
            if d == 0:
                b_all = c_all + jnp.where(late_rows, t_prev, 0.0)
            else:
                b_all = c_all + jnp.where(late_rows, 0.0, t_next)
            btot_all = t_all + jnp.where(late_rows, t_prev, t_next)
            shared.append((r0, b_all, btot_all, c_all, t_all))
        b, btot, kg, vg, qe, qc, kc = [], [], [], [], [], [], []
        for d, j in chains:
            r0, b_all, btot_all, c_all, t_all = shared[d]
            sl = slice(j * 128, (j + 1) * 128)
            lanes = slice(j * 128 + d * dk, j * 128 + (d + 1) * dk)
            b.append(b_all[:, lanes])
            btot.append(btot_all[:, lanes])
            c, tt = c_all[:, lanes], t_all[:, lanes]
            qk = qk_ref[pl.ds(r0, GROUP), sl].astype(F32)
            q = qk[:, :dk] * (dk ** -0.5)
            kg.append(qk[:, dk:])
            vg.append(v_ref[pl.ds(r0, GROUP), sl])
            qe.append((q * jnp.exp(b[-1])).astype(BF16))
            qcf = q * jnp.exp(c)
            kcf = kg[-1] * jnp.exp(-c)
            kxf = kg[-1] * jnp.exp(tt - c)
            second = late_rows if d == 0 else ~late_rows
            qc.append(jnp.concatenate([jnp.where(second, 0.0, qcf), jnp.where(second, qcf, 0.0)],
                                      axis=1).astype(BF16))
            kc.append(jnp.concatenate([jnp.where(second, 0.0, kcf), jnp.where(second, kcf, kxf)],
                                      axis=1).astype(BF16))
        vg = [v.astype(BF16) for v in vg]
        attn = [jnp.where(allowed[chains[i][0]], _bdot_nt(qc[i], kc[i]), 0.0).astype(BF16) for i in rng]
        o_in = [_bdot(attn[i], vg[i]) for i in rng]
        kd = [(kg[i] * jnp.exp(btot[i] - b[i])).astype(BF16) for i in rng]
        st = [s_scr[d, j] for d, j in chains]
        o_parts = [[None] * nchunk for _ in rng]
        for cc in range(nchunk):
            rs = []
            for d, _ in chains:
                c = cc if d == 0 else nchunk - 1 - cc
                rs.append(slice(c * GLA_CHUNK, (c + 1) * GLA_CHUNK))
            qs = [_bdot_nt(qe[i][rs[i]], st[i]) for i in rng]
            vk = [_bdot_tn(vg[i][rs[i]], kd[i][rs[i]]) for i in rng]
            for i in rng:
                o_parts[i][rs[i].start // GLA_CHUNK] = o_in[i][rs[i]] + qs[i]
                st[i] = st[i] * jnp.exp(btot[i][rs[i].start:rs[i].start + 1, :]) + vk[i]
        for i, (d, j) in enumerate(chains):
            s_scr[d, j] = st[i]
            o_scr = of_scr if d == 0 else ob_scr
            o_scr[pl.ds(shared[d][0], GROUP), j * 128:(j + 1) * 128] = jnp.concatenate(o_parts[i], axis=0)
        return carry

    if ngroups == 1:
        group_step(0, 0)
    else:
        lax.fori_loop(0, ngroups, group_step, 0)

    for j in range(hp):
        sl = slice(j * 128, (j + 1) * 128)
        o = of_scr[:, sl] + ob_scr[:, sl]
        o = o * lax.rsqrt(jnp.mean(o * o, axis=-1, keepdims=True) + EPS) * og_ref[...]
        o_ref[:, sl] = (o * _silu(gate_ref[:, sl].astype(F32))).astype(BF16)
    if emit_state:
        sfin = _own_slab(sfin_ref, layer, not has_prev)
        for d in range(2):
            for j in range(hp):
                sfin[d, j] = _transpose_f32(s_scr[d, j])


def _gla(p, psm, wg, bg, out_g, s0, prev_state, layer, batch, seq, emit_state, hp):
    rows = batch * seq
    dv, dk = GLA_VAL_DIM, GLA_KEY_DIM
    w = hp * 128
    col = lambda off: pl.BlockSpec((seq, w), lambda b, h: (b, off // w + h))
    in_specs = [col(OFF_GLA_QK), col(OFF_GLA_V), col(OFF_GLA_GATE),
                pl.BlockSpec((seq, 128), lambda b, h: (b, 0)),
                pl.BlockSpec((128, w), lambda b, h: (0, h)),
                pl.BlockSpec((1, w), lambda b, h: (0, h)),
                pl.BlockSpec((1, 128), lambda b, h: (0, 0))]
    args = [p, p, p, psm, wg, bg, out_g]
    if s0 is not None:
        in_specs.append(pl.BlockSpec((None, None, 2, hp, dk, dv), lambda b, h: (b, layer, 0, h, 0, 0)))
        args.append(s0)
    out_specs = [pl.BlockSpec((seq, w), lambda b, h: (b, h))]
    out_shape = [jax.ShapeDtypeStruct((rows, GLA_HEADS * dv), BF16)]
    aliases = {}
    if emit_state:
        out_specs.append(_layer_slab((2, hp, dk, dv), lambda h: (0, h, 0, 0), layer, prev_state is None))
        out_shape.append(jax.ShapeDtypeStruct((batch, DEPTH, 2, GLA_HEADS, dk, dv), F32))
        if prev_state is not None:
            in_specs.append(pl.BlockSpec(memory_space=pl.ANY))
            args.append(prev_state)
            aliases = {len(args) - 1: 1}
    res = pl.pallas_call(
        functools.partial(_gla_kernel, hp=hp, has_s0=s0 is not None, emit_state=emit_state,
                          has_prev=prev_state is not None, layer=layer),
        grid=(batch, GLA_HEADS // hp),
        in_specs=in_specs,
        out_specs=out_specs,
        out_shape=out_shape,
        input_output_aliases=aliases,
        scratch_shapes=[pltpu.VMEM((seq, w), F32)] * 2 + [pltpu.VMEM((2, hp, dv, dk), F32)],
        compiler_params=_params("arbitrary", "arbitrary"),
        name="gla",
    )(*args)
    return res if emit_state else (res[0], None)


def _merge_kernel(x_ref, oa_ref, ob_ref, oc_ref, ma_ref, mb_ref, mc_ref, mod_ref, wbr_ref, wo_ref, o_ref,
                  *, tiles_per_seq, latent):
    r = _mod_row(pl.program_id(0), tiles_per_seq, latent)
    merged = (_sigmoid(ma_ref[...].astype(F32)) * _bdot(oa_ref[...], wbr_ref[0])
              + _sigmoid(mb_ref[...].astype(F32)) * _bdot(ob_ref[...], wbr_ref[1])
              + _sigmoid(mc_ref[...].astype(F32)) * _bdot(oc_ref[...], wbr_ref[2]))
    o_ref[...] = x_ref[...] + mod_ref[2, pl.ds(r, 1), :] * _bdot(merged, wo_ref[...])


def _merge(x, o_a, o_b, o_c, p, mod_l, w_br, w_o, layer, seq_len, latent, tm=512):
    rows = x.shape[0]
    row_blk = lambda w: pl.BlockSpec((tm, w), lambda m: (m, 0))
    gate_blk = lambda j: pl.BlockSpec((tm, D_MODEL), lambda m: (m, OFF_M // D_MODEL + j))
    return pl.pallas_call(
        functools.partial(_merge_kernel, tiles_per_seq=max(seq_len // tm, 1), latent=latent),
        grid=(rows // tm,),
        in_specs=[row_blk(D_MODEL), row_blk(512), row_blk(512), row_blk(512),
                  gate_blk(0), gate_blk(1), gate_blk(2),
                  pl.BlockSpec((6, 8, D_MODEL), lambda m: (0, 0, 0)),
                  _resident(w_br, layer), _resident(w_o, layer)],
        out_specs=row_blk(D_MODEL),
        out_shape=jax.ShapeDtypeStruct((rows, D_MODEL), F32),
        compiler_params=_params("arbitrary"),
        name="merge",
    )(x, o_a, o_b, o_c, p, p, p, mod_l, w_br, w_o)


FFN_HALO = 16


def _ffn_kernel(x_ref, xp_ref, xn_ref, mod_ref, n2_ref, wu_ref, fc_ref, wd_ref, o_ref, h_scr, act_scr,
                uv_scr, ug_scr, *, tm, tn, seq_len, tiles_per_seq, latent):
    r = _mod_row(pl.program_id(0), tiles_per_seq, latent)
    shift = mod_ref[3, pl.ds(r, 1), :]
    scale = mod_ref[4, pl.ds(r, 1), :]
    nm = lambda x: _norm_mod(x, n2_ref[...], shift, scale)
    row0 = pl.program_id(0) * tm
    h_scr[0:FFN_HALO, :] = jnp.where(row0 % seq_len == 0, 0.0, nm(xp_ref[...])).astype(BF16)
    h_scr[FFN_HALO:FFN_HALO + tm, :] = nm(x_ref[...]).astype(BF16)
    h_scr[FFN_HALO + tm:, :] = jnp.where((row0 + tm) % seq_len == 0, 0.0, nm(xn_ref[...])).astype(BF16)

    ext = tm + 2 * FFN_HALO
    inner = slice(FFN_HALO, FFN_HALO + tm)
    bounds = list(range(seq_len, tm, seq_len))
    fix = 16
    rid = lax.broadcasted_iota(jnp.int32, (2 * fix, 1), 0)

    def up(n):
        hh = h_scr[...]
        cv = slice(n * tn, (n + 1) * tn)
        cg = slice(D_FF + n * tn, D_FF + (n + 1) * tn)
        return (jnp.dot(hh, wu_ref[:, cv], preferred_element_type=F32),
                jnp.dot(hh, wu_ref[:, cg], preferred_element_type=F32))

    def taps(u, scr):
        scr[...] = u
        return (scr[FFN_HALO - 1:FFN_HALO - 1 + tm, :], scr[FFN_HALO:FFN_HALO + tm, :],
                scr[FFN_HALO + 1:FFN_HALO + 1 + tm, :])

    def conv(t, f, rows=slice(None), keep_prev=None, keep_next=None):
        prev, mid, nxt = (a[rows] for a in t)
        if keep_prev is not None:
            prev = jnp.where(keep_prev, prev, 0.0)
            nxt = jnp.where(keep_next, nxt, 0.0)
        return prev * f[0:1, :] + mid * f[1:2, :] + nxt * f[2:3, :]

    nt = D_FF // tn
    u_next = up(0)
    for n in range(nt):
        u_val, u_gate = u_next
        if n + 1 < nt:
            u_next = up(n + 1)
        cols = slice(n * tn, (n + 1) * tn)
        f_val = fc_ref[:, cols]
        f_gate = fc_ref[:, D_FF + n * tn:D_FF + (n + 1) * tn]
        t_val, t_gate = taps(u_val, uv_scr), taps(u_gate, ug_scr)
        act_scr[:, cols] = (_silu(conv(t_gate, f_gate)) * conv(t_val, f_val)).astype(BF16)
        for b in bounds:
            rows = slice(b - fix, b + fix)
            kp, kn = rid != fix, rid != fix - 1
            act_scr[rows, cols] = (_silu(conv(t_gate, f_gate, rows, kp, kn))
                                   * conv(t_val, f_val, rows, kp, kn)).astype(BF16)
    down = jnp.dot(act_scr[...], wd_ref[...], preferred_element_type=F32)
    o_ref[...] = x_ref[...] + mod_ref[5, pl.ds(r, 1), :] * down


def _ffn(x, mod_l, n2, w_up, f_conv, w_dn, layer, seq_len, latent, tm=512, tn=256):
    rows = x.shape[0]
    hb = tm // FFN_HALO
    last_hb = rows // FFN_HALO - 1
    kern = functools.partial(_ffn_kernel, tm=tm, tn=tn, seq_len=seq_len, tiles_per_seq=max(seq_len // tm, 1),
                             latent=latent)
    return pl.pallas_call(
        kern,
        grid=(rows // tm,),
        in_specs=[pl.BlockSpec((tm, D_MODEL), lambda m: (m, 0)),
                  pl.BlockSpec((FFN_HALO, D_MODEL), lambda m: (jnp.maximum(m * hb - 1, 0), 0)),
                  pl.BlockSpec((FFN_HALO, D_MODEL), lambda m: (jnp.minimum((m + 1) * hb, last_hb), 0)),
                  pl.BlockSpec((6, 8, D_MODEL), lambda m: (0, 0, 0)),
                  pl.BlockSpec((1, D_MODEL), lambda m: (0, 0)),
                  _resident(w_up, layer), _resident(f_conv, layer), _resident(w_dn, layer)],
        out_specs=pl.BlockSpec((tm, D_MODEL), lambda m: (m, 0)),
        out_shape=jax.ShapeDtypeStruct((rows, D_MODEL), F32),
        scratch_shapes=[pltpu.VMEM((tm + 2 * FFN_HALO, D_MODEL), BF16), pltpu.VMEM((tm, D_FF), BF16),
                        pltpu.VMEM((tm + 2 * FFN_HALO, tn), F32), pltpu.VMEM((tm + 2 * FFN_HALO, tn), F32)],
        compiler_params=_params("arbitrary"),
        name="ffn",
    )(x, x, x, mod_l, n2, w_up, f_conv, w_dn)


def _w_in_moves():
    src = np.cumsum([0, 512, 512, 512, 512, 512, 512, 512, 8, 8, 256, 256, 512, 512, 32, 1024, 1024, 1024])
    dk = GLA_KEY_DIM
    moves = [(0, 0, int(src[7]))]
    for h in range(GLA_HEADS):
        moves.append((OFF_GLA_QK + 2 * dk * h, int(src[9]) + dk * h, dk))
        moves.append((OFF_GLA_QK + 2 * dk * h + dk, int(src[10]) + dk * h, dk))
    moves += [(OFF_GLA_V, int(src[11]), 512), (OFF_GLA_GATE, int(src[12]), 512), (OFF_M, int(src[14]), 3072),
              (OFF_SMALL + SMALL_BETA, int(src[7]), 8), (OFF_SMALL + SMALL_DECAY, int(src[8]), 8),
              (OFF_SMALL + SMALL_LR, int(src[13]), 32)]
    return moves


def _permute_kernel(w_ref, o_ref):
    small = []
    for dst, src, width in _w_in_moves():
        if dst >= OFF_SMALL:
            small.append(w_ref[src:src + width, :])
        else:
            o_ref[dst:dst + width, :] = w_ref[src:src + width, :].astype(BF16)
    used = sum(s.shape[0] for s in small)
    small.append(jnp.zeros((PROJ_PAD - OFF_SMALL - used, o_ref.shape[1]), F32))
    o_ref[OFF_SMALL:, :] = jnp.concatenate(small, axis=0).astype(BF16)


def _permute_w_in(w_in_t, tc=256):
    proj = w_in_t.shape[1]
    return pl.pallas_call(
        _permute_kernel,
        grid=(DEPTH, D_MODEL // tc),
        in_specs=[pl.BlockSpec((None, proj, tc), lambda l, c: (l, 0, c))],
        out_specs=pl.BlockSpec((None, PROJ_PAD, tc), lambda l, c: (l, 0, c)),
        out_shape=jax.ShapeDtypeStruct((DEPTH, PROJ_PAD, D_MODEL), BF16),
        compiler_params=_params("arbitrary", "arbitrary"),
        name="permute_w_in",
    )(w_in_t)


def _gla_gate_weights(gla_w, gla_b):
    r, dk = GLA_GATE_RANK, GLA_KEY_DIM
    w = gla_w.reshape(DEPTH, 2, r, GLA_HEADS, dk)
    wg = jnp.zeros((DEPTH, GLA_HEADS, 128, 2, dk), F32)
    for d in range(2):
        wg = wg.at[:, :, SMALL_LR + d * r:SMALL_LR + (d + 1) * r, d, :].set(jnp.transpose(w[:, d], (0, 2, 1, 3)))
    bg = jnp.transpose(gla_b.reshape(DEPTH, 2, GLA_HEADS, dk), (0, 2, 1, 3)).reshape(DEPTH, 1, GLA_HEADS * 2 * dk)
    wg = jnp.transpose(wg, (0, 2, 1, 3, 4)).reshape(DEPTH, 128, GLA_HEADS * 2 * dk)
    return wg.astype(BF16), bg.astype(F32)


def _lane_row(values, offset):
    n = values.shape[-1]
    return jnp.zeros((DEPTH, 1, 128), F32).at[:, 0, offset:offset + n].set(values.astype(F32))


def kernel(x_prompt, x_sample, cache_k, cache_v, state_dn, state_gla, c, c_ctx, w_ada, b_ada, norm1, w_in,
           na_q_norm, na_k_norm, na_rpb, dn_conv, dn_a_log, dn_dt_bias, dn_out_norm, gla_w_gate, gla_b_gate,
           gla_out_norm, w_branch, w_out, norm2, w_up, ffn_conv, w_down):
    batch, seq, _ = x_prompt.shape
    dbatch, dseq, _ = x_sample.shape
    past = cache_k.shape[2]

    cvec8 = jnp.concatenate([c_ctx[None, :], c, jnp.zeros((8 - 1 - dbatch, D_MODEL), F32)], axis=0)
    mod = _adaln(cvec8, w_ada, b_ada)

    w_in_p = _permute_w_in(jnp.swapaxes(w_in, 1, 2))
    w_br = w_branch.astype(BF16)
    w_o = w_out.astype(BF16)
    w_up_b = w_up.astype(BF16)
    w_dn_b = w_down.astype(BF16)
    tiles = _na_bias_tiles(na_rpb)
    wg, bg = _gla_gate_weights(gla_w_gate, gla_b_gate)
    alog128 = _lane_row(dn_a_log.reshape(DEPTH, 2 * DN_HEADS), SMALL_DECAY)
    dtb128 = _lane_row(dn_dt_bias.reshape(DEPTH, 2 * DN_HEADS), SMALL_DECAY)
    ck = jnp.transpose(cache_k, (0, 1, 3, 4, 2))
    cv = jnp.transpose(cache_v, (0, 1, 3, 4, 2))

    y_p = x_prompt.reshape(batch * seq, D_MODEL)
    y_s = x_sample.reshape(dbatch * dseq, D_MODEL)
    new_kv, new_dn, new_gla = None, None, None
    for l in range(DEPTH):
        qg, kg = na_q_norm[l][None, :], na_k_norm[l][None, :]
        n1, n2 = norm1[l][None, :], norm2[l][None, :]
        dn_g, gla_g = dn_out_norm[l][None, :], gla_out_norm[l][None, :]
        for latent in (False, True):
            x = y_s if latent else y_p
            b_, t_ = (dbatch, dseq) if latent else (batch, seq)
            p, psm = _inproj(x, mod[l], n1, w_in_p, l, t_, latent)
            if latent:
                o_a = _lat_attn(p, ck, cv, tiles, qg, kg, l, b_, t_)
                o_b, _ = _deltanet(p, psm, dn_conv[l], alog128[l], dtb128[l], dn_g, state_dn, None, l, b_, t_,
                                   False, 2)
                o_c, _ = _gla(p, psm, wg[l], bg[l], gla_g, state_gla, None, l, b_, t_, False, 2)
            else:
                o_a, k_all, v_all = _ctx_attn(p, qg, kg, new_kv, l, b_, t_)
                new_kv = (k_all, v_all)
                o_b, new_dn = _deltanet(p, psm, dn_conv[l], alog128[l], dtb128[l], dn_g, None, new_dn, l, b_, t_,
                                        True, DN_HEADS)
                o_c, new_gla = _gla(p, psm, wg[l], bg[l], gla_g, None, new_gla, l, b_, t_, True, GLA_HEADS)
            x = _merge(x, o_a, o_b, o_c, p, mod[l], w_br, w_o, l, t_, latent)
            x = _ffn(x, mod[l], n2, w_up_b, ffn_conv, w_dn_b, l, t_, latent)
            if latent:
                y_s = x
            else:
                y_p = x
    cache_shape = (batch, DEPTH, seq, NA_HEADS, NA_HEAD_DIM)
    return (y_p.reshape(batch, seq, D_MODEL), y_s.reshape(dbatch, dseq, D_MODEL),
            new_kv[0].reshape(cache_shape), new_kv[1].reshape(cache_shape), new_dn, new_gla)
```
